```python
import jax
import jax.numpy as jnp
from jax import lax
import numpy as np

D_MODEL = 1024
BATCH = 2
SEQ = 16384
DEPTH = 2
DEC_BATCH = 8
DEC_SEQ = 64
PAST_LEN = 4096

CHUNK = 64
N_BRANCH = 4
BR_WIDTH = D_MODEL // N_BRANCH
HEAD_DIM = 64
N_HEADS = BR_WIDTH // HEAD_DIM
GLA_DK = HEAD_DIM // 2
GLA_KW = N_HEADS * GLA_DK
GLA_GATE_RANK = 16
GLA_TAU = 16.0
RWKV_W_LORA = 32
RWKV_A_LORA = 32
RWKV_V_LORA = 16
RWKV_G_LORA = 64
RWKV_GN_EPS = 64e-5
ROPE_THETA = 10000.0
LN_EPS = 1e-5
D_FF = 2816
N_EXPERTS = 8
TOP_K = 2
MOE_BLOCK = 128
N_DENSE = (DEPTH + 1) // 2
N_MOE = DEPTH // 2
N_VRES = DEPTH - 1
ALPHA = (2.0 * DEPTH) ** 0.25
BETA = (8.0 * DEPTH) ** -0.25
RWKV_COLS = 3 * BR_WIDTH
RET_COLS = 4 * BR_WIDTH
HGRN_COLS = 4 * BR_WIDTH
GLA_COLS = 2 * GLA_KW + 2 * BR_WIDTH
IN_COLS = RWKV_COLS + RET_COLS + HGRN_COLS + GLA_COLS
SPLIT_AT = (RWKV_COLS, RWKV_COLS + RET_COLS, RWKV_COLS + RET_COLS + HGRN_COLS)

kernel_name = 'hybrid_streaming_encoder_step'


def split_heads(z, d):
    return z.reshape(z.shape[:-1] + (z.shape[-1] // d, d))


def layer_norm(x, g, b):
    xf = x.astype(jnp.float32)
    mu = jnp.mean(xf, -1, keepdims=True)
    var = jnp.mean(jnp.square(xf - mu), -1, keepdims=True)
    return ((xf - mu) * lax.rsqrt(var + LN_EPS) * g.astype(jnp.float32) + b.astype(jnp.float32)).astype(x.dtype)


def head_layer_norm(z, g, b, eps):
    zf = z.astype(jnp.float32)
    mu = jnp.mean(zf, -1, keepdims=True)
    var = jnp.mean(jnp.square(zf - mu), -1, keepdims=True)
    out = ((zf - mu) * lax.rsqrt(var + eps)).reshape(z.shape[:-2] + (z.shape[-2] * z.shape[-1],))
    return out * g.astype(jnp.float32) + b.astype(jnp.float32)


def head_rms_norm(z, g, eps=1e-6):
    zf = z.astype(jnp.float32)
    out = zf * lax.rsqrt(jnp.mean(jnp.square(zf), -1, keepdims=True) + eps)
    return out.reshape(z.shape[:-2] + (z.shape[-2] * z.shape[-1],)) * g.astype(jnp.float32)


def rope(z, pos):
    half = z.shape[-1] // 2
    inv = ROPE_THETA ** (-jnp.arange(half, dtype=jnp.float32) / half)
    ang = pos[:, None] * inv[None]
    cos = jnp.cos(ang)[None, :, None]
    sin = jnp.sin(ang)[None, :, None]
    z1 = z[..., :half].astype(jnp.float32)
    z2 = z[..., half:].astype(jnp.float32)
    return jnp.concatenate([z1 * cos - z2 * sin, z1 * sin + z2 * cos], -1).astype(z.dtype)


def chunk_gated_linear_attention(q, k, v, log_a, s0, chunk):
    B, T, H, K = q.shape
    V = v.shape[-1]
    n = T // chunk
    f32 = jnp.float32

    def blocks(z):
        return jnp.moveaxis(z.astype(f32).reshape(B, n, chunk, H, z.shape[-1]), 1, 0)

    causal = jnp.tril(jnp.ones((chunk, chunk), dtype=bool))[None, :, :, None, None]

    def body(S, inp):
        qc, kc, vc, gc = inp
        b = jnp.cumsum(gc, axis=1)
        o = jnp.einsum('blhk,bhkv->blhv', qc * jnp.exp(b), S)
        diff = b[:, :, None] - b[:, None, :]
        rel = jnp.where(causal, jnp.exp(jnp.where(causal, diff, 0.0)), 0.0)
        att = jnp.einsum('bthk,bshk,btshk->bhts', qc, kc, rel)
        o = o + jnp.einsum('bhts,bshv->bthv', att, vc)
        b_end = b[:, -1]
        S = S * jnp.exp(b_end)[..., None] + jnp.einsum('blhk,blhv->bhkv', kc * jnp.exp(b_end[:, None] - b), vc)
        return S, o

    S, o = lax.scan(body, s0.astype(f32), (blocks(q), blocks(k), blocks(v), blocks(log_a)))
    return jnp.moveaxis(o, 0, 1).reshape(B, T, H, V), S


def rwkv7_recurrence(r, w, k, v, kk, a, s0):
    def step(S, inp):
        r_t, w_t, k_t, v_t, kk_t, a_t = inp
        s_kk = jnp.einsum('bhij,bhj->bhi', S, kk_t)
        S = (S * w_t[:, :, None, :] - s_kk[..., None] * (kk_t * a_t)[:, :, None, :]
             + v_t[..., None] * k_t[:, :, None, :])
        return S, jnp.einsum('bhij,bhj->bhi', S, r_t)
    xs = tuple(jnp.moveaxis(z, 1, 0) for z in (r, w, k, v, kk, a))
    S, y = lax.scan(step, s0, xs)
    return jnp.moveaxis(y, 0, 1), S


def rwkv7_branch(x, x_prev, rkv, rkv_prev, v_first, s0, l, prm):
    B, T, _ = x.shape
    f32 = jnp.float32
    mixed = rkv + (rkv_prev - rkv) * prm['rwkv_mu_rkv'][l].reshape(RWKV_COLS)
    r, k, v = jnp.split(mixed, 3, axis=-1)
    xx = x_prev - x
    mu = prm['rwkv_mu_x'][l]
    xw = x + xx * mu[0]
    xa = x + xx * mu[1]
    xg = x + xx * mu[2]
    w_log = -jax.nn.softplus(-(prm['rwkv_w0'][l] + jnp.tanh(xw @ prm['rwkv_w1'][l]) @ prm['rwkv_w2'][l]).astype(f32)) - 0.5
    decay = jnp.exp(-jnp.exp(w_log))
    a = jax.nn.sigmoid(prm['rwkv_a0'][l] + (xa @ prm['rwkv_a1'][l]) @ prm['rwkv_a2'][l])
    g = jax.nn.sigmoid(xg @ prm['rwkv_g1'][l]) @ prm['rwkv_g2'][l]
    if l == 0:
        v_first = v
    else:
        xv = x + xx * prm['rwkv_mu_v'][l - 1]
        v = v + (v_first - v) * jax.nn.sigmoid(prm['rwkv_v0'][l - 1] + (xv @ prm['rwkv_v1'][l - 1]) @ prm['rwkv_v2'][l - 1])
    kk = split_heads((k * prm['rwkv_k_k'][l]).astype(f32), HEAD_DIM)
    kk = kk * lax.rsqrt(jnp.maximum(jnp.sum(jnp.square(kk), -1, keepdims=True), 1e-24))
    k = k * (1 + (a - 1) * prm['rwkv_k_a'][l])
    rh = split_heads(r.astype(f32), HEAD_DIM)
    kh = split_heads(k.astype(f32), HEAD_DIM)
    vh = split_heads(v.astype(f32), HEAD_DIM)
    y, S = rwkv7_recurrence(rh, split_heads(decay, HEAD_DIM), kh, vh, kk,
                            split_heads(a.astype(f32), HEAD_DIM), s0.astype(f32))
    y = head_layer_norm(y, prm['rwkv_ln_g'][l], prm['rwkv_ln_b'][l], RWKV_GN_EPS)
    r_k = prm['rwkv_r_k'][l].reshape(N_HEADS, HEAD_DIM).astype(f32)
    bonus = jnp.sum(rh * kh * r_k, -1, keepdims=True) * vh
    y = (y + bonus.reshape(B, T, BR_WIDTH)) * g.astype(f32)
    return y.astype(x.dtype), S, v_first


def retention_branch(p, pos, s0, chunk, l, prm):
    B, T, _ = p.shape
    q, k, v, g = jnp.split(p, 4, axis=-1)
    q = rope(split_heads(q, HEAD_DIM), pos)
    k = rope(split_heads(k, HEAD_DIM), pos) * HEAD_DIM ** -0.5
    log_gamma = jnp.log1p(-jnp.exp2(-5.0 - jnp.arange(N_HEADS, dtype=jnp.float32)))
    log_a = jnp.broadcast_to(log_gamma[:, None], (B, T, N_HEADS, HEAD_DIM))
    o, S = chunk_gated_linear_attention(q, k, split_heads(v, HEAD_DIM), log_a, s0, chunk)
    o = head_layer_norm(o, prm['ret_gn_g'][l], prm['ret_gn_b'][l], LN_EPS) * jax.nn.silu(g.astype(jnp.float32))
    return o.astype(p.dtype), S


def hgrn2_branch(p, lb, s0, chunk, l, prm):
    q, fz, i, g = jnp.split(p, 4, axis=-1)
    fzf = fz.astype(jnp.float32)
    f = lb + (1.0 - lb) * jax.nn.sigmoid(fzf)
    log_f = jnp.log(f)
    k = (1.0 - lb) * jax.nn.sigmoid(-fzf)
    o, S = chunk_gated_linear_attention(split_heads(jax.nn.silu(q), HEAD_DIM), split_heads(k, HEAD_DIM),
                                        split_heads(i, HEAD_DIM), split_heads(log_f, HEAD_DIM), s0, chunk)
    o = head_rms_norm(o, prm['hgrn_norm_g'][l]) * jax.nn.silu(g.astype(jnp.float32))
    return o.astype(p.dtype), S


def gla_branch(x, p, s0, chunk, l, prm):
    q, k, v, g = jnp.split(p, (GLA_KW, 2 * GLA_KW, 2 * GLA_KW + BR_WIDTH), axis=-1)
    log_a = jax.nn.log_sigmoid(((x @ prm['gla_w1'][l]) @ prm['gla_w2'][l] + prm['gla_b'][l]).astype(jnp.float32)) / GLA_TAU
    o, S = chunk_gated_linear_attention(split_heads(q, GLA_DK) * GLA_DK ** -0.5, split_heads(k, GLA_DK),
                                        split_heads(v, HEAD_DIM), split_heads(log_a, GLA_DK), s0, chunk)
    o = head_rms_norm(o, prm['gla_norm_g'][l]) * jax.nn.silu(g.astype(jnp.float32))
    return o.astype(x.dtype), S


def swiglu(x, w_gate, w_up, w_down):
    return (jax.nn.silu(x @ w_gate) * (x @ w_up)) @ w_down


def moe_swiglu(x, w_r, b_r, wg, wu, wd):
    B, T, D = x.shape
    N = B * T
    xt = x.reshape(N, D)
    logits = xt.astype(jnp.float32) @ w_r.astype(jnp.float32) + b_r.astype(jnp.float32)
    top_l, top_e = lax.top_k(logits, TOP_K)
    top_w = jax.nn.softmax(top_l, axis=-1)
    n_pairs = N * TOP_K
    flat_e = top_e.reshape(-1)
    flat_w = top_w.reshape(-1)
    flat_tok = jnp.repeat(jnp.arange(N, dtype=jnp.int32), TOP_K)
    order = jnp.argsort(flat_e)
    se, stok, sw = flat_e[order], flat_tok[order], flat_w[order]
    counts = jnp.bincount(flat_e, length=N_EXPERTS)
    padded = (counts + MOE_BLOCK - 1) // MOE_BLOCK * MOE_BLOCK
    pad_end = jnp.cumsum(padded)
    pad_start = pad_end - padded
    grp_start = jnp.cumsum(counts) - counts
    dest = pad_start[se] + jnp.arange(n_pairs, dtype=jnp.int32) - grp_start[se]
    n_blocks = -(-(n_pairs + N_EXPERTS * (MOE_BLOCK - 1)) // MOE_BLOCK)
    slot_tok = jnp.full((n_blocks * MOE_BLOCK,), N, jnp.int32).at[dest].set(stok)
    blk_e = jnp.minimum(jnp.searchsorted(pad_end, jnp.arange(n_blocks) * MOE_BLOCK, side='right'), N_EXPERTS - 1)
    xpad = jnp.concatenate([xt, jnp.zeros((1, D), xt.dtype)], axis=0)
    xb = xpad[slot_tok].reshape(n_blocks, MOE_BLOCK, D)

    def expert_block(args):
        xblk, e = args
        return (jax.nn.silu(xblk @ wg[e]) * (xblk @ wu[e])) @ wd[e]

    yb = lax.map(expert_block, (xb, blk_e)).reshape(n_blocks * MOE_BLOCK, D)
    y = jnp.zeros((N, D), yb.dtype).at[stok].add(yb[dest] * sw[:, None].astype(yb.dtype))
    return y.reshape(B, T, D)


def run_trunk(x, pos0, s_rwkv, c_shift, s_ret, s_hgrn, s_gla, prm):
    B, T, _ = x.shape
    dt = x.dtype
    chunk = CHUNK if T % CHUNK == 0 else T
    pos = pos0 + jnp.arange(T, dtype=jnp.float32)
    lb_w = jax.nn.softmax(prm['hgrn_lb_logits'].astype(jnp.float32), axis=0)
    lower_bounds = jnp.cumsum(lb_w, axis=0) - lb_w[0]
    v_first = None
    new_rwkv, new_shift, new_ret, new_hgrn, new_gla = [], [], [], [], []
    for l in range(DEPTH):
        x_in = x
        w_in_l = prm['w_in'][l]
        proj = x @ w_in_l
        rkv, ret_p, hgrn_p, gla_p = jnp.split(proj, SPLIT_AT, axis=-1)
        x_last = c_shift[l].astype(dt)
        x_prev = jnp.concatenate([x_last[:, None], x[:, :-1]], axis=1)
        rkv_prev = jnp.concatenate([(x_last @ w_in_l[:, :RWKV_COLS])[:, None], rkv[:, :-1]], axis=1)
        o_rwkv, S_rwkv, v_first = rwkv7_branch(x, x_prev, rkv, rkv_prev, v_first, s_rwkv[l], l, prm)
        o_ret, S_ret = retention_branch(ret_p, pos, s_ret[l], chunk, l, prm)
        o_hgrn, S_hgrn = hgrn2_branch(hgrn_p, lower_bounds[l], s_hgrn[l], chunk, l, prm)
        o_gla, S_gla = gla_branch(x, gla_p, s_gla[l], chunk, l, prm)
        merged = jnp.zeros_like(x)
        for m, o in enumerate((o_rwkv, o_ret, o_hgrn, o_gla)):
            gate = jax.nn.sigmoid(x @ prm['w_gate'][l, m] + prm['b_gate'][l, m])
            merged = merged + gate * (o @ prm['w_br'][l, m])
        x = layer_norm(ALPHA * x + merged @ prm['w_o'][l], prm['ln1_g'][l], prm['ln1_b'][l])
        j = l // 2
        if l % 2 == 0:
            f = swiglu(x, prm['ffn_w_gate'][j], prm['ffn_w_up'][j], prm['ffn_w_down'][j])
        else:
            f = moe_swiglu(x, prm['router_w'][j], prm['router_b'][j], prm['moe_w_gate'][j],
                           prm['moe_w_up'][j], prm['moe_w_down'][j])
        x = layer_norm(ALPHA * x + f, prm['ln2_g'][l], prm['ln2_b'][l])
        new_rwkv.append(S_rwkv)
        new_shift.append(x_in[:, -1])
        new_ret.append(S_ret)
        new_hgrn.append(S_hgrn)
        new_gla.append(S_gla)
    return (x, jnp.stack(new_rwkv).astype(dt), jnp.stack(new_shift).astype(dt), jnp.stack(new_ret).astype(dt),
            jnp.stack(new_hgrn).astype(dt), jnp.stack(new_gla).astype(dt))


def setup_inputs(seed: int = 0) -> dict:
    key = jax.random.key(seed)
    f32 = jnp.float32
    counter = [0]

    def draw(shape, scale):
        counter[0] += 1
        return jax.random.normal(jax.random.fold_in(key, counter[0]), shape, f32) * scale

    def draw_unif(shape):
        counter[0] += 1
        return jax.random.uniform(jax.random.fold_in(key, counter[0]), shape, f32)

    def gain(shape):
        return 1.0 + draw(shape, 0.02)

    D, BR, H, HD = D_MODEL, BR_WIDTH, N_HEADS, HEAD_DIM
    return {
        'x_prompt': draw((BATCH, SEQ, D), 1.0),
        'x_sample': draw((DEC_BATCH, DEC_SEQ, D), 1.0),
        'state_rwkv': draw((DEPTH, DEC_BATCH, H, HD, HD), 0.3),
        'cache_shift': draw((DEPTH, DEC_BATCH, D), 1.0),
        'state_ret': draw((DEPTH, DEC_BATCH, H, HD, HD), 1.0),
        'state_hgrn': draw((DEPTH, DEC_BATCH, H, HD, HD), 0.5),
        'state_gla': draw((DEPTH, DEC_BATCH, H, GLA_DK, HD), 0.5),
        'w_in': draw((DEPTH, D, IN_COLS), D ** -0.5),
        'rwkv_mu_rkv': draw_unif((DEPTH, 3, BR)),
        'rwkv_mu_x': draw_unif((DEPTH, 3, D)),
        'rwkv_mu_v': draw_unif((N_VRES, D)),
        'rwkv_w0': jnp.linspace(-6.0, -1.0, BR, dtype=f32)[None] + draw((DEPTH, BR), 0.1),
        'rwkv_w1': draw((DEPTH, D, RWKV_W_LORA), D ** -0.5),
        'rwkv_w2': draw((DEPTH, RWKV_W_LORA, BR), 0.1 * RWKV_W_LORA ** -0.5),
        'rwkv_a0': draw((DEPTH, BR), 0.1),
        'rwkv_a1': draw((DEPTH, D, RWKV_A_LORA), D ** -0.5),
        'rwkv_a2': draw((DEPTH, RWKV_A_LORA, BR), 0.1 * RWKV_A_LORA ** -0.5),
        'rwkv_v0': draw((N_VRES, BR), 0.1),
        'rwkv_v1': draw((N_VRES, D, RWKV_V_LORA), D ** -0.5),
        'rwkv_v2': draw((N_VRES, RWKV_V_LORA, BR), 0.1 * RWKV_V_LORA ** -0.5),
        'rwkv_g1': draw((DEPTH, D, RWKV_G_LORA), D ** -0.5),
        'rwkv_g2': draw((DEPTH, RWKV_G_LORA, BR), RWKV_G_LORA ** -0.5),
        'rwkv_k_k': 0.85 + draw((DEPTH, BR), 0.02),
        'rwkv_k_a': gain((DEPTH, BR)),
        'rwkv_r_k': draw((DEPTH, BR), 0.1),
        'rwkv_ln_g': gain((DEPTH, BR)),
        'rwkv_ln_b': draw((DEPTH, BR), 0.02),
        'ret_gn_g': gain((DEPTH, BR)),
        'ret_gn_b': draw((DEPTH, BR), 0.02),
        'hgrn_lb_logits': draw((DEPTH, BR), 0.5),
        'hgrn_norm_g': gain((DEPTH, BR)),
        'gla_w1': draw((DEPTH, D, GLA_GATE_RANK), D ** -0.5),
        'gla_w2': draw((DEPTH, GLA_GATE_RANK, GLA_KW), GLA_GATE_RANK ** -0.5),
        'gla_b': draw((DEPTH, GLA_KW), 0.5),
        'gla_norm_g': gain((DEPTH, BR)),
        'w_br': draw((DEPTH, N_BRANCH, BR, D), BETA * BR ** -0.5),
        'w_gate': draw((DEPTH, N_BRANCH, D, D), D ** -0.5),
        'b_gate': draw((DEPTH, N_BRANCH, D), 0.02),
        'w_o': draw((DEPTH, D, D), BETA * D ** -0.5),
        'ln1_g': gain((DEPTH, D)),
        'ln1_b': draw((DEPTH, D), 0.02),
        'ln2_g': gain((DEPTH, D)),
        'ln2_b': draw((DEPTH, D), 0.02),
        'ffn_w_gate': draw((N_DENSE, D, D_FF), D ** -0.5),
        'ffn_w_up': draw((N_DENSE, D, D_FF), D ** -0.5),
        'ffn_w_down': draw((N_DENSE, D_FF, D), BETA * D_FF ** -0.5),
        'router_w': draw((N_MOE, D, N_EXPERTS), D ** -0.5),
        'router_b': draw((N_MOE, N_EXPERTS), 0.01),
        'moe_w_gate': draw((N_MOE, N_EXPERTS, D, D_FF), D ** -0.5),
        'moe_w_up': draw((N_MOE, N_EXPERTS, D, D_FF), D ** -0.5),
        'moe_w_down': draw((N_MOE, N_EXPERTS, D_FF, D), BETA * D_FF ** -0.5),
    }


def reference(x_prompt, x_sample, state_rwkv, cache_shift, state_ret, state_hgrn, state_gla,
              w_in, rwkv_mu_rkv, rwkv_mu_x, rwkv_mu_v, rwkv_w0, rwkv_w1, rwkv_w2, rwkv_a0, rwkv_a1, rwkv_a2,
              rwkv_v0, rwkv_v1, rwkv_v2, rwkv_g1, rwkv_g2, rwkv_k_k, rwkv_k_a, rwkv_r_k, rwkv_ln_g, rwkv_ln_b,
              ret_gn_g, ret_gn_b, hgrn_lb_logits, hgrn_norm_g, gla_w1, gla_w2, gla_b, gla_norm_g,
              w_br, w_gate, b_gate, w_o, ln1_g, ln1_b, ln2_g, ln2_b,
              ffn_w_gate, ffn_w_up, ffn_w_down, router_w, router_b, moe_w_gate, moe_w_up, moe_w_down):
    prm = {
        'w_in': w_in, 'rwkv_mu_rkv': rwkv_mu_rkv, 'rwkv_mu_x': rwkv_mu_x, 'rwkv_mu_v': rwkv_mu_v,
        'rwkv_w0': rwkv_w0, 'rwkv_w1': rwkv_w1, 'rwkv_w2': rwkv_w2,
        'rwkv_a0': rwkv_a0, 'rwkv_a1': rwkv_a1, 'rwkv_a2': rwkv_a2,
        'rwkv_v0': rwkv_v0, 'rwkv_v1': rwkv_v1, 'rwkv_v2': rwkv_v2,
        'rwkv_g1': rwkv_g1, 'rwkv_g2': rwkv_g2, 'rwkv_k_k': rwkv_k_k, 'rwkv_k_a': rwkv_k_a,
        'rwkv_r_k': rwkv_r_k, 'rwkv_ln_g': rwkv_ln_g, 'rwkv_ln_b': rwkv_ln_b,
        'ret_gn_g': ret_gn_g, 'ret_gn_b': ret_gn_b, 'hgrn_lb_logits': hgrn_lb_logits,
        'hgrn_norm_g': hgrn_norm_g, 'gla_w1': gla_w1, 'gla_w2': gla_w2, 'gla_b': gla_b,
        'gla_norm_g': gla_norm_g, 'w_br': w_br, 'w_gate': w_gate, 'b_gate': b_gate, 'w_o': w_o,
        'ln1_g': ln1_g, 'ln1_b': ln1_b, 'ln2_g': ln2_g, 'ln2_b': ln2_b,
        'ffn_w_gate': ffn_w_gate, 'ffn_w_up': ffn_w_up, 'ffn_w_down': ffn_w_down,
        'router_w': router_w, 'router_b': router_b,
        'moe_w_gate': moe_w_gate, 'moe_w_up': moe_w_up, 'moe_w_down': moe_w_down,
    }
    dt = x_prompt.dtype
    bp = x_prompt.shape[0]
    zero_hd = jnp.zeros((DEPTH, bp, N_HEADS, HEAD_DIM, HEAD_DIM), dt)
    zero_shift = jnp.zeros((DEPTH, bp, D_MODEL), dt)
    zero_gla = jnp.zeros((DEPTH, bp, N_HEADS, GLA_DK, HEAD_DIM), dt)
    y_prompt, p_rwkv, p_shift, p_ret, p_hgrn, p_gla = run_trunk(
        x_prompt, 0, zero_hd, zero_shift, zero_hd, zero_hd, zero_gla, prm)
    y_sample, s_rwkv, s_shift, s_ret, s_hgrn, s_gla = run_trunk(
        x_sample, PAST_LEN, state_rwkv, cache_shift, state_ret, state_hgrn, state_gla, prm)
    return (y_prompt, y_sample, p_rwkv, p_shift, p_ret, p_hgrn, p_gla, s_rwkv, s_shift, s_ret, s_hgrn, s_gla)
```

```python
import functools
import math

import numpy as np
import jax
import jax.numpy as jnp
from jax import lax
from jax.experimental import pallas as pl
from jax.experimental.pallas import tpu as pltpu

F32 = jnp.float32
BF16 = jnp.bfloat16

D_MODEL = 1024
DEPTH = 2
PAST_LEN = 4096
CHUNK = 64
N_BRANCH = 4
BR_WIDTH = D_MODEL // N_BRANCH
HEAD_DIM = 64
N_HEADS = BR_WIDTH // HEAD_DIM
GLA_DK = HEAD_DIM // 2
GLA_KW = N_HEADS * GLA_DK
GLA_GATE_RANK = 16
GLA_TAU = 16.0
RWKV_W_LORA = 32
RWKV_A_LORA = 32
RWKV_V_LORA = 16
RWKV_G_LORA = 64
RWKV_GN_EPS = 64e-5
ROPE_THETA = 10000.0
LN_EPS = 1e-5
D_FF = 2816
N_EXPERTS = 8
ALPHA = (2.0 * DEPTH) ** 0.25
RWKV_COLS = 3 * BR_WIDTH
IN_COLS = 3584
LORA_COLS = 256
AUX_COLS = 4 * BR_WIDTH + GLA_KW

V7X_VMEM_BYTES = 64 * 1024 * 1024
LANE = 128
SUBLANE = 8

NN = (((1,), (0,)), ((), ()))
NT = (((1,), (1,)), ((), ()))
TN = (((0,), (0,)), ((), ()))


def _params(sem, vmem_mib):
    return pltpu.CompilerParams(dimension_semantics=sem, vmem_limit_bytes=vmem_mib * 1024 * 1024)


def _const_spec(shape):
    nd = len(shape)
    return pl.BlockSpec(shape, lambda *_: (0,) * nd, pipeline_mode=pl.Buffered(1))


def _dot(a, b, dims=NN):
    return lax.dot_general(a.astype(BF16), b.astype(BF16), dims, preferred_element_type=F32)


def _split(x):
    hi = x.astype(BF16)
    lo = (x - hi.astype(F32)).astype(BF16)
    return hi, lo


def _dot_exact_lhs(a_bf, x):
    hi, lo = _split(x)
    return (jnp.dot(a_bf, hi, preferred_element_type=F32)
            + jnp.dot(a_bf, lo, preferred_element_type=F32))


def _dot_exact_rhs(x, b_bf):
    hi, lo = _split(x)
    return (jnp.dot(hi, b_bf, preferred_element_type=F32)
            + jnp.dot(lo, b_bf, preferred_element_type=F32))


def _dot3(a, b, dims=NN):
    ah, al = _split(a)
    bh, bl = _split(b)
    d = functools.partial(lax.dot_general, dimension_numbers=dims, preferred_element_type=F32)
    return d(ah, bh) + (d(ah, bl) + d(al, bh))


def _sigmoid(x):
    return 1.0 / (1.0 + jnp.exp(-x))


def _softplus(x):
    return jnp.maximum(x, 0.0) + jnp.log(1.0 + jnp.exp(-jnp.abs(x)))


def _silu(x):
    return x * _sigmoid(x)


def _layer_norm_rows(y, g, b):
    mu = jnp.mean(y, axis=-1, keepdims=True)
    yc = y - mu
    var = jnp.mean(yc * yc, axis=-1, keepdims=True)
    return yc * lax.rsqrt(var + LN_EPS) * g + b


def _np_block_mask(rows_per_head, cols_per_head):
    r = np.arange(N_HEADS * rows_per_head)[:, None] // rows_per_head
    c = np.arange(N_HEADS * cols_per_head)[None, :] // cols_per_head
    return (r == c).astype(np.float32)


def _np_tril(n):
    return np.tril(np.ones((n, n), np.float32))


def _np_causal_side_by_side(strict):
    t = np.arange(CHUNK)[:, None]
    s = np.arange(N_HEADS * CHUNK)[None, :] % CHUNK
    return ((s < t) if strict else (s <= t)).astype(np.float32)


def _in_proj_kernel(x_ref, xp_ref, win_ref, wl1_ref, wl1mu_ref, w2_ref, bias_ref, p_ref, aux_ref):
    x = x_ref[...]
    xx = xp_ref[...] - x
    xb = x.astype(BF16)
    p_ref[...] = jnp.dot(xb, win_ref[...], preferred_element_type=F32)
    h = (jnp.dot(xb, wl1_ref[...], preferred_element_type=F32)
         + jnp.dot(xx.astype(BF16), wl1mu_ref[...], preferred_element_type=F32))
    lane = lax.broadcasted_iota(jnp.int32, h.shape, 1)
    act = jnp.where(lane < RWKV_W_LORA, jnp.tanh(h), h)
    g_lo = RWKV_W_LORA + RWKV_A_LORA
    in_g = jnp.where(lane >= g_lo, jnp.where(lane < g_lo + RWKV_G_LORA, 1.0, 0.0), 0.0)
    act = jnp.where(in_g > 0.5, _sigmoid(h), act)
    z = jnp.dot(act.astype(BF16), w2_ref[...], preferred_element_type=F32) + bias_ref[...]
    bw = BR_WIDTH
    w_log = -_softplus(-z[:, 0:bw]) - 0.5
    aux_ref[:, 0:bw] = -jnp.exp(w_log)
    aux_ref[:, bw:2 * bw] = _sigmoid(z[:, bw:2 * bw])
    aux_ref[:, 2 * bw:3 * bw] = z[:, 2 * bw:3 * bw]
    aux_ref[:, 3 * bw:4 * bw] = _sigmoid(z[:, 3 * bw:4 * bw])
    zg = z[:, 4 * bw:]
    aux_ref[:, 4 * bw:] = (jnp.minimum(zg, 0.0) - jnp.log(1.0 + jnp.exp(-jnp.abs(zg)))) * (1.0 / GLA_TAU)


def _in_proj(x2, xp2, win, wl1, wl1mu, w2, bias, tm):
    n = x2.shape[0]
    return pl.pallas_call(
        _in_proj_kernel,
        grid=(n // tm,),
        in_specs=[
            pl.BlockSpec((tm, D_MODEL), lambda i: (i, 0)),
            pl.BlockSpec((tm, D_MODEL), lambda i: (i, 0)),
            _const_spec(win.shape), _const_spec(wl1.shape), _const_spec(wl1mu.shape),
            _const_spec(w2.shape), _const_spec(bias.shape),
        ],
        out_specs=[pl.BlockSpec((tm, IN_COLS), lambda i: (i, 0)),
                   pl.BlockSpec((tm, AUX_COLS), lambda i: (i, 0))],
        out_shape=[jax.ShapeDtypeStruct((n, IN_COLS), F32), jax.ShapeDtypeStruct((n, AUX_COLS), F32)],
        compiler_params=_params(("arbitrary",), 56),
        name="in_proj",
    )(x2, xp2, win, wl1, wl1mu, w2, bias)


def _rows_matmul_kernel(x_ref, w_ref, o_ref):
    o_ref[...] = jnp.dot(x_ref[...].astype(BF16), w_ref[...], preferred_element_type=F32)


def _rows_matmul(x, w):
    return pl.pallas_call(
        _rows_matmul_kernel,
        out_shape=jax.ShapeDtypeStruct((x.shape[0], w.shape[1]), F32),
        name="rows_matmul",
    )(x, w)


def _stack_heads(x, mask):
    return jnp.concatenate([x] * N_HEADS, axis=0) * mask


def _head_sum(x, ones_bf):
    return _dot_exact_rhs(x, ones_bf)


def _head_layer_norm(y, ones_bf, g, b, eps):
    inv = 1.0 / HEAD_DIM
    mu = _head_sum(y, ones_bf) * inv
    yc = y - mu
    var = _head_sum(yc * yc, ones_bf) * inv
    return yc * lax.rsqrt(var + eps) * g + b


def _head_rms_norm(y, ones_bf, g):
    ms = _head_sum(y * y, ones_bf) * (1.0 / HEAD_DIM)
    return y * lax.rsqrt(ms + 1e-6) * g


def _gla_chunk(q, k, v, glog, st, tril_bf, pair_ones_bf, st_mask):
    b = _dot_exact_lhs(tril_bf, glog)
    o = _dot(q * jnp.exp(b), st, NT)
    groups = CHUNK // SUBLANE
    for r in range(groups):
        lo = r * SUBLANE
        rows = CHUNK - lo
        qs = q[lo:, :]
        bs = b[lo:, :]
        rid = lax.broadcasted_iota(jnp.int32, (rows, q.shape[1]), 0)
        parts = []
        for j in range(SUBLANE):
            s = lo + j
            e = jnp.exp(jnp.minimum(bs - b[s:s + 1, :], 0.0))
            p = qs * e * k[s:s + 1, :]
            if j:
                p = jnp.where(rid >= j, p, 0.0)
            parts.append(p.astype(BF16))
        att = jnp.dot(jnp.concatenate(parts, axis=0), pair_ones_bf, preferred_element_type=F32)
        contrib = att[0:rows, :] * v[lo:lo + 1, :]
        for j in range(1, SUBLANE):
            contrib = contrib + att[j * rows:(j + 1) * rows, :] * v[lo + j:lo + j + 1, :]
        if lo:
            contrib = jnp.concatenate([jnp.zeros((lo, v.shape[1]), F32), contrib], axis=0)
        o = o + contrib
    b_last = b[CHUNK - 1:CHUNK, :]
    st_new = st * jnp.exp(b_last) + st_mask * _dot(v, k * jnp.exp(b_last - b), TN)
    return o, st_new


def _rwkv_chunk(r, k, v, kk, bv, lw, st, tril_bf, bdm, strict, incl, eye):
    l = _dot_exact_lhs(tril_bf, lw)
    l_last = l[CHUNK - 1:CHUNK, :]
    e_neg = jnp.exp(-l)
    a_t = kk * jnp.exp(l - lw)
    r_t = r * jnp.exp(l)
    k_h = k * e_neg
    b_h = bv * e_neg
    lhs = jnp.concatenate([a_t, r_t], axis=0)
    rhs = jnp.concatenate([_stack_heads(k_h, bdm), _stack_heads(b_h, bdm)], axis=0)
    amat = _dot3(lhs, rhs, NT)
    w = N_HEADS * CHUNK
    a_ak = amat[:CHUNK, :w] * strict
    a_ab = amat[:CHUNK, w:] * strict
    a_rk = amat[CHUNK:, :w] * incl
    a_rb = amat[CHUNK:, w:] * incl
    x = eye + a_ab
    m = a_ab
    for _ in range(int(math.log2(CHUNK)) - 1):
        m = _dot3(m, _stack_heads(m, bdm))
        x = x + _dot3(x, _stack_heads(m, bdm))
    from_state = _dot3(lhs, st, NT)
    rhs_u = from_state[:CHUNK] + _dot3(a_ak, _stack_heads(v, bdm))
    u = _dot3(x, _stack_heads(rhs_u, bdm))
    y = from_state[CHUNK:] + _dot3(jnp.concatenate([a_rk, a_rb], axis=1),
                                   jnp.concatenate([_stack_heads(v, bdm), _stack_heads(u, bdm)], axis=0))
    e_end = jnp.exp(l_last - l)
    upd = _dot3(jnp.concatenate([v, u], axis=0),
                jnp.concatenate([k * e_end, bv * e_end], axis=0), TN)
    st_new = st * jnp.exp(l_last) + upd * bdm
    return y, st_new


def _rwkv_kernel(has_vres, *refs):
    if has_vres:
        (rkv_ref, lw_ref, a_ref, g_ref, vg_ref, vf_ref, last_ref, st0_ref, prm_ref, tril_ref, bdm_ref,
         strict_ref, incl_ref, eye_ref, o_ref, st_out_ref, st_sc, prev_sc) = refs
    else:
        (rkv_ref, lw_ref, a_ref, g_ref, last_ref, st0_ref, prm_ref, tril_ref, bdm_ref,
         strict_ref, incl_ref, eye_ref, o_ref, v_out_ref, st_out_ref, st_sc, prev_sc) = refs
    tb = pl.program_id(1)

    @pl.when(tb == 0)
    def _():
        st_sc[...] = st0_ref[0]
        prev_sc[...] = last_ref[0]

    bw = BR_WIDTH
    rkv = rkv_ref[0]
    tt = rkv.shape[0]
    row = lax.broadcasted_iota(jnp.int32, rkv.shape, 0)
    prev = jnp.where(row == 0, prev_sc[...], pltpu.roll(rkv, 1, axis=0))
    prev_sc[...] = rkv[tt - 1:tt, :]
    mixed = rkv + (prev - rkv) * prm_ref[0:1, :]
    r = mixed[:, 0:bw]
    k = mixed[:, bw:2 * bw]
    v = mixed[:, 2 * bw:]
    k_k = prm_ref[1:2, 0:bw]
    k_a = prm_ref[2:3, 0:bw]
    r_k = prm_ref[3:4, 0:bw]
    ln_g = prm_ref[4:5, 0:bw]
    ln_b = prm_ref[5:6, 0:bw]
    a = a_ref[0]
    lw = lw_ref[0]
    if has_vres:
        v = v + (vf_ref[0] - v) * vg_ref[0]
    else:
        v_out_ref[0] = v
    bdm = bdm_ref[...]
    ones_bf = bdm.astype(BF16)
    kk = k * k_k
    kk = kk * lax.rsqrt(jnp.maximum(_head_sum(kk * kk, ones_bf), 1e-24))
    k = k * (1.0 + (a - 1.0) * k_a)
    bv = -(kk * a)
    tril_bf = tril_ref[...]
    strict = strict_ref[...]
    incl = incl_ref[...]
    eye = eye_ref[...]
    st = st_sc[...]
    ys = []
    for c in range(tt // CHUNK):
        sl = slice(c * CHUNK, (c + 1) * CHUNK)
        y_c, st = _rwkv_chunk(r[sl], k[sl], v[sl], kk[sl], bv[sl], lw[sl], st, tril_bf, bdm,
                              strict, incl, eye)
        ys.append(y_c)
    st_sc[...] = st
    y = ys[0] if len(ys) == 1 else jnp.concatenate(ys, axis=0)
    y = _head_layer_norm(y, ones_bf, ln_g, ln_b, RWKV_GN_EPS)
    bonus = _head_sum(r * k * r_k, ones_bf) * v
    o_ref[0] = (y + bonus) * g_ref[0]

    @pl.when(tb == pl.num_programs(1) - 1)
    def _():
        st_out_ref[0] = st


def _rwkv_mixer(p3, aux3, v_first, rkv_last, st0, prm, consts, tt):
    b, t, _ = p3.shape
    bw = BR_WIDTH
    has_vres = v_first is not None
    tok = lambda j: pl.BlockSpec((1, tt, bw), lambda bi, ti, j=j: (bi, ti, j))
    in_specs = [pl.BlockSpec((1, tt, RWKV_COLS), lambda bi, ti: (bi, ti, 0)),
                tok(0), tok(1), tok(2)]
    args = [p3, aux3, aux3, aux3]
    if has_vres:
        in_specs += [tok(3), pl.BlockSpec((1, tt, bw), lambda bi, ti: (bi, ti, 0))]
        args += [aux3, v_first]
    in_specs += [pl.BlockSpec((1, 1, RWKV_COLS), lambda bi, ti: (bi, 0, 0)),
                 pl.BlockSpec((1, bw, bw), lambda bi, ti: (bi, 0, 0)),
                 _const_spec(prm.shape)]
    args += [rkv_last, st0, prm]
    for name in ("tril", "bdm", "strict", "incl", "eye"):
        in_specs.append(_const_spec(consts[name].shape))
        args.append(consts[name])
    seq = pl.BlockSpec((1, tt, bw), lambda bi, ti: (bi, ti, 0))
    st_spec = pl.BlockSpec((1, bw, bw), lambda bi, ti: (bi, 0, 0))
    seq_shape = jax.ShapeDtypeStruct((b, t, bw), F32)
    st_shape = jax.ShapeDtypeStruct((b, bw, bw), F32)
    if has_vres:
        out_specs, out_shape = [seq, st_spec], [seq_shape, st_shape]
    else:
        out_specs, out_shape = [seq, seq, st_spec], [seq_shape, seq_shape, st_shape]
    return pl.pallas_call(
        functools.partial(_rwkv_kernel, has_vres),
        grid=(b, t // tt),
        in_specs=in_specs, out_specs=out_specs, out_shape=out_shape,
        scratch_shapes=[pltpu.VMEM((bw, bw), F32), pltpu.VMEM((1, RWKV_COLS), F32)],
        compiler_params=_params(("arbitrary", "arbitrary"), 48),
        name="rwkv_mixer",
    )(*args)


def _rot_half(z):
    w = z.shape[1]
    half = HEAD_DIM // 2
    lane = lax.broadcasted_iota(jnp.int32, z.shape, 1)
    first = (lane % HEAD_DIM) < half
    return jnp.where(first, pltpu.roll(z, w - half, axis=1), pltpu.roll(z, half, axis=1))


def _ret_kernel(q_ref, k_ref, v_ref, g_ref, cos_ref, sin_ref, st0_ref, prm_ref, dec_ref, bdm_ref,
                o_ref, st_out_ref, st_sc):
    tb = pl.program_id(1)

    @pl.when(tb == 0)
    def _():
        st_sc[...] = st0_ref[0]

    cos = cos_ref[...]
    sin = sin_ref[...]
    q = q_ref[0]
    k = k_ref[0]
    v = v_ref[0]
    q = q * cos + _rot_half(q) * sin
    k = (k * cos + _rot_half(k) * sin) * (HEAD_DIM ** -0.5)
    bdm = bdm_ref[...]
    ones_bf = bdm.astype(BF16)
    q_dec = dec_ref[0:CHUNK, :]
    k_dec = dec_ref[CHUNK:2 * CHUNK, :]
    d_mat = dec_ref[2 * CHUNK:3 * CHUNK, :]
    s_dec = dec_ref[3 * CHUNK:3 * CHUNK + 1, :]
    st = st_sc[...]
    tt = q.shape[0]
    outs = []
    for c in range(tt // CHUNK):
        sl = slice(c * CHUNK, (c + 1) * CHUNK)
        qc, kc, vc = q[sl], k[sl], v[sl]
        att = _dot(qc, _stack_heads(kc, bdm), NT) * d_mat
        o = _dot(qc * q_dec, st, NT) + _dot(att, _stack_heads(vc, bdm))
        st = st * s_dec + bdm * _dot(vc, kc * k_dec, TN)
        outs.append(o)
    st_sc[...] = st
    o = outs[0] if len(outs) == 1 else jnp.concatenate(outs, axis=0)
    o = _head_layer_norm(o, ones_bf, prm_ref[0:1, :], prm_ref[1:2, :], LN_EPS)
    o_ref[0] = o * _silu(g_ref[0])

    @pl.when(tb == pl.num_programs(1) - 1)
    def _():
        st_out_ref[0] = st


def _ret_mixer(p3, cos_t, sin_t, st0, prm, dec, bdm, tt):
    b, t, _ = p3.shape
    bw = BR_WIDTH
    base = RWKV_COLS // bw
    tok = lambda j: pl.BlockSpec((1, tt, bw), lambda bi, ti, j=j: (bi, ti, base + j))
    tab = pl.BlockSpec((tt, bw), lambda bi, ti: (ti, 0))
    st_spec = pl.BlockSpec((1, bw, bw), lambda bi, ti: (bi, 0, 0))
    return pl.pallas_call(
        _ret_kernel,
        grid=(b, t // tt),
        in_specs=[tok(0), tok(1), tok(2), tok(3), tab, tab, st_spec,
                  _const_spec(prm.shape), _const_spec(dec.shape), _const_spec(bdm.shape)],
        out_specs=[pl.BlockSpec((1, tt, bw), lambda bi, ti: (bi, ti, 0)), st_spec],
        out_shape=[jax.ShapeDtypeStruct((b, t, bw), F32), jax.ShapeDtypeStruct((b, bw, bw), F32)],
        scratch_shapes=[pltpu.VMEM((bw, bw), F32)],
        compiler_params=_params(("arbitrary", "arbitrary"), 40),
        name="ret_mixer",
    )(p3, p3, p3, p3, cos_t, sin_t, st0, prm, dec, bdm)


def _gated_mixer_tail(q, k, v, glog, gate, norm_g, st_sc, st_out_ref, o_ref, tril_bf, pair_ones_bf,
                      st_mask, ones_v_bf):
    st = st_sc[...]
    tt = q.shape[0]
    outs = []
    for c in range(tt // CHUNK):
        sl = slice(c * CHUNK, (c + 1) * CHUNK)
        o, st = _gla_chunk(q[sl], k[sl], v[sl], glog[sl], st, tril_bf, pair_ones_bf, st_mask)
        outs.append(o)
    st_sc[...] = st
    o = outs[0] if len(outs) == 1 else jnp.concatenate(outs, axis=0)
    o_ref[0] = _head_rms_norm(o, ones_v_bf, norm_g) * _silu(gate)

    @pl.when(pl.program_id(1) == pl.num_programs(1) - 1)
    def _():
        st_out_ref[0] = st


def _hgrn_kernel(layer, q_ref, f_ref, i_ref, g_ref, st0_ref, lbl_ref, ng_ref, tril_ref, bdm_ref,
                 o_ref, st_out_ref, st_sc):
    @pl.when(pl.program_id(1) == 0)
    def _():
        st_sc[...] = st0_ref[0]

    logits = lbl_ref[...]
    ex = jnp.exp(logits - jnp.max(logits, axis=0, keepdims=True))
    sm = ex / jnp.sum(ex, axis=0, keepdims=True)
    lb = jnp.zeros((1, BR_WIDTH), F32)
    for d in range(1, layer + 1):
        lb = lb + sm[d:d + 1, :]
    fz = f_ref[0]
    f = lb + (1.0 - lb) * _sigmoid(fz)
    k = (1.0 - lb) * _sigmoid(-fz)
    bdm = bdm_ref[...]
    ones_bf = bdm.astype(BF16)
    _gated_mixer_tail(_silu(q_ref[0]), k, i_ref[0], jnp.log(f), g_ref[0], ng_ref[...], st_sc,
                      st_out_ref, o_ref, tril_ref[...], ones_bf, bdm, ones_bf)


def _hgrn_mixer(p3, st0, lb_logits, norm_g, tril, bdm, layer, tt):
    b, t, _ = p3.shape
    bw = BR_WIDTH
    base = (RWKV_COLS + 4 * bw) // bw
    tok = lambda j: pl.BlockSpec((1, tt, bw), lambda bi, ti, j=j: (bi, ti, base + j))
    st_spec = pl.BlockSpec((1, bw, bw), lambda bi, ti: (bi, 0, 0))
    return pl.pallas_call(
        functools.partial(_hgrn_kernel, layer),
        grid=(b, t // tt),
        in_specs=[tok(0), tok(1), tok(2), tok(3), st_spec, _const_spec(lb_logits.shape),
                  _const_spec(norm_g.shape), _const_spec(tril.shape), _const_spec(bdm.shape)],
        out_specs=[pl.BlockSpec((1, tt, bw), lambda bi, ti: (bi, ti, 0)), st_spec],
        out_shape=[jax.ShapeDtypeStruct((b, t, bw), F32), jax.ShapeDtypeStruct((b, bw, bw), F32)],
        scratch_shapes=[pltpu.VMEM((bw, bw), F32)],
        compiler_params=_params(("arbitrary", "arbitrary"), 48),
        name="hgrn_mixer",
    )(p3, p3, p3, p3, st0, lb_logits, norm_g, tril, bdm)


def _gla_kernel(q_ref, k_ref, v_ref, g_ref, la_ref, st0_ref, ng_ref, tril_ref, pair_ref, mask_ref,
                bdm_ref, o_ref, st_out_ref, st_sc):
    @pl.when(pl.program_id(1) == 0)
    def _():
        st_sc[...] = st0_ref[0]

    _gated_mixer_tail(q_ref[0] * (GLA_DK ** -0.5), k_ref[0], v_ref[0], la_ref[0], g_ref[0], ng_ref[...],
                      st_sc, st_out_ref, o_ref, tril_ref[...], pair_ref[...], mask_ref[...],
                      bdm_ref[...].astype(BF16))


def _gla_mixer(p3, aux3, st0, norm_g, tril, pair_ones, st_mask, bdm, tt):
    b, t, _ = p3.shape
    bw, kw = BR_WIDTH, GLA_KW
    gla0 = RWKV_COLS + 8 * bw
    st_spec = pl.BlockSpec((1, bw, kw), lambda bi, ti: (bi, 0, 0))
    return pl.pallas_call(
        _gla_kernel,
        grid=(b, t // tt),
        in_specs=[pl.BlockSpec((1, tt, kw), lambda bi, ti: (bi, ti, gla0 // kw)),
                  pl.BlockSpec((1, tt, kw), lambda bi, ti: (bi, ti, gla0 // kw + 1)),
                  pl.BlockSpec((1, tt, bw), lambda bi, ti: (bi, ti, (gla0 + 2 * kw) // bw)),
                  pl.BlockSpec((1, tt, bw), lambda bi, ti: (bi, ti, (gla0 + 2 * kw) // bw + 1)),
                  pl.BlockSpec((1, tt, kw), lambda bi, ti: (bi, ti, 4 * bw // kw)),
                  st_spec, _const_spec(norm_g.shape), _const_spec(tril.shape),
                  _const_spec(pair_ones.shape), _const_spec(st_mask.shape), _const_spec(bdm.shape)],
        out_specs=[pl.BlockSpec((1, tt, bw), lambda bi, ti: (bi, ti, 0)), st_spec],
        out_shape=[jax.ShapeDtypeStruct((b, t, bw), F32), jax.ShapeDtypeStruct((b, bw, kw), F32)],
        scratch_shapes=[pltpu.VMEM((bw, kw), F32)],
        compiler_params=_params(("arbitrary", "arbitrary"), 48),
        name="gla_mixer",
    )(p3, p3, p3, p3, aux3, st0, norm_g, tril, pair_ones, st_mask, bdm)


def _merge_kernel(emit_bf16, x_ref, o0_ref, o1_ref, o2_ref, o3_ref, wg_ref, bg_ref, wbr_ref, wo_ref,
                  ln_ref, *out_refs):
    x = x_ref[...]
    xb = x.astype(BF16)
    merged = None
    for m, o_ref in enumerate((o0_ref, o1_ref, o2_ref, o3_ref)):
        gate = _sigmoid(jnp.dot(xb, wg_ref[m], preferred_element_type=F32) + bg_ref[m:m + 1, :])
        term = gate * jnp.dot(o_ref[...].astype(BF16), wbr_ref[m], preferred_element_type=F32)
        merged = term if merged is None else merged + term
    y = ALPHA * x + jnp.dot(merged.astype(BF16), wo_ref[...], preferred_element_type=F32)
    y = _layer_norm_rows(y, ln_ref[0:1, :], ln_ref[1:2, :])
    out_refs[0][...] = y
    if emit_bf16:
        out_refs[1][...] = y.astype(BF16)


def _merge(x2, outs, wg, bg, wbr, wo, ln, tm, emit_bf16):
    n = x2.shape[0]
    row = pl.BlockSpec((tm, D_MODEL), lambda i: (i, 0))
    br = pl.BlockSpec((tm, BR_WIDTH), lambda i: (i, 0))
    out_specs = [row]
    out_shape = [jax.ShapeDtypeStruct((n, D_MODEL), F32)]
    if emit_bf16:
        out_specs.append(row)
        out_shape.append(jax.ShapeDtypeStruct((n, D_MODEL), BF16))
    return pl.pallas_call(
        functools.partial(_merge_kernel, emit_bf16),
        grid=(n // tm,),
        in_specs=[row, br, br, br, br, _const_spec(wg.shape), _const_spec(bg.shape),
                  _const_spec(wbr.shape), _const_spec(wo.shape), _const_spec(ln.shape)],
        out_specs=out_specs, out_shape=out_shape,
        compiler_params=_params(("arbitrary",), 56),
        name="merge",
    )(x2, *outs, wg, bg, wbr, wo, ln)


FF_SPLIT = 2
FF_PART = D_FF // FF_SPLIT


def _ffn_kernel(x_ref, wg_ref, wu_ref, wd_ref, ln_ref, o_ref):
    x = x_ref[...]
    xb = x.astype(BF16)
    acc = ALPHA * x
    for c in range(FF_SPLIT):
        cs = slice(c * FF_PART, (c + 1) * FF_PART)
        h = (_silu(jnp.dot(xb, wg_ref[:, cs], preferred_element_type=F32))
             * jnp.dot(xb, wu_ref[:, cs], preferred_element_type=F32))
        acc = acc + jnp.dot(h.astype(BF16), wd_ref[cs, :], preferred_element_type=F32)
    o_ref[...] = _layer_norm_rows(acc, ln_ref[0:1, :], ln_ref[1:2, :])


def _ffn(x2, wg, wu, wd, ln, tm):
    n = x2.shape[0]
    row = pl.BlockSpec((tm, D_MODEL), lambda i: (i, 0))
    return pl.pallas_call(
        _ffn_kernel,
        grid=(n // tm,),
        in_specs=[row, _const_spec(wg.shape), _const_spec(wu.shape), _const_spec(wd.shape),
                  _const_spec(ln.shape)],
        out_specs=row,
        out_shape=jax.ShapeDtypeStruct((n, D_MODEL), F32),
        compiler_params=_params(("arbitrary",), 56),
        name="ffn",
    )(x2, wg, wu, wd, ln)


def _router_kernel(x_ref, wr_ref, br_ref, tril_ref, rank_ref, wsel_ref, cnt_ref):
    logits = _dot3(x_ref[...], wr_ref[...]) + br_ref[...]
    lane = lax.broadcasted_iota(jnp.int32, logits.shape, 1)
    neg = jnp.float32(-jnp.inf)
    logits = jnp.where(lane < N_EXPERTS, logits, neg)
    m1 = jnp.max(logits, axis=1, keepdims=True)
    lane_f = lane.astype(F32)
    i1 = jnp.min(jnp.where(logits == m1, lane_f, float(LANE)), axis=1, keepdims=True)
    first = lane_f == i1
    rest = jnp.where(first, neg, logits)
    m2 = jnp.max(rest, axis=1, keepdims=True)
    i2 = jnp.min(jnp.where(rest == m2, lane_f, float(LANE)), axis=1, keepdims=True)
    second = lane_f == i2
    e = jnp.exp(m2 - m1)
    w1 = 1.0 / (1.0 + e)
    w2 = e / (1.0 + e)
    sel = jnp.where(first, 1.0, jnp.where(second, 1.0, 0.0))
    wsel_ref[...] = jnp.where(first, w1, jnp.where(second, w2, 0.0))
    sel_bf = sel.astype(BF16)
    rank = jnp.dot(tril_ref[...], sel_bf, preferred_element_type=F32)
    rank_ref[...] = jnp.where(sel > 0.5, rank, -1.0).astype(jnp.int32)
    ones = jnp.ones((SUBLANE, sel.shape[0]), BF16)
    cnt_ref[0] = jnp.dot(ones, sel_bf, preferred_element_type=F32).astype(jnp.int32)


def _router(x2, wr, br, tril, tm):
    n = x2.shape[0]
    nt = n // tm
    col = pl.BlockSpec((tm, LANE), lambda i: (i, 0))
    return pl.pallas_call(
        _router_kernel,
        grid=(nt,),
        in_specs=[pl.BlockSpec((tm, D_MODEL), lambda i: (i, 0)), _const_spec(wr.shape),
                  _const_spec(br.shape), _const_spec(tril.shape)],
        out_specs=[col, col, pl.BlockSpec((1, SUBLANE, LANE), lambda i: (i, 0, 0))],
        out_shape=[jax.ShapeDtypeStruct((n, LANE), jnp.int32), jax.ShapeDtypeStruct((n, LANE), F32),
                   jax.ShapeDtypeStruct((nt, SUBLANE, LANE), jnp.int32)],
        compiler_params=_params(("arbitrary",), 40),
        name="router",
    )(x2, wr, br, tril)


def _moe_kernel(rows, cnt_ref, x_ref, xb_ref, rrow_ref, rcol_ref, wcol_ref, wg_ref, wu_ref, wd_ref,
                ln_ref, o_ref, xg_sc, yb_sc):
    i = pl.program_id(0)
    e = pl.program_id(1)
    c = pl.program_id(2)
    n_e = pl.num_programs(1)
    n_c = pl.num_programs(2)
    cnt = cnt_ref[i * N_EXPERTS + e]
    n_blk = (cnt + rows - 1) // rows
    tm = xb_ref.shape[0]

    @pl.when((e == 0) & (c == 0))
    def _():
        o_ref[...] = ALPHA * x_ref[...]

    def gather(blk, carry):
        r0 = pl.multiple_of(blk * rows, SUBLANE)
        slot = lax.broadcasted_iota(jnp.int32, (rows, tm), 0) + r0
        onehot = jnp.where(rrow_ref[0] == slot, 1.0, 0.0).astype(BF16)
        xg_sc[pl.ds(r0, rows), :] = jnp.dot(onehot, xb_ref[...], preferred_element_type=F32).astype(BF16)
        return carry

    @pl.when(c == 0)
    def _():
        lax.fori_loop(0, n_blk, gather, 0)

    def expert(blk, carry):
        r0 = pl.multiple_of(blk * rows, SUBLANE)
        xg = xg_sc[pl.ds(r0, rows), :]
        h = (_silu(jnp.dot(xg, wg_ref[0], preferred_element_type=F32))
             * jnp.dot(xg, wu_ref[0], preferred_element_type=F32))
        yb = jnp.dot(h.astype(BF16), wd_ref[0], preferred_element_type=F32)

        @pl.when(c == 0)
        def _():
            yb_sc[pl.ds(r0, rows), :] = yb

        @pl.when(c > 0)
        def _():
            yb_sc[pl.ds(r0, rows), :] = yb_sc[pl.ds(r0, rows), :] + yb
        return carry

    lax.fori_loop(0, n_blk, expert, 0)

    def scatter(blk, carry):
        r0 = pl.multiple_of(blk * rows, SUBLANE)
        slot = lax.broadcasted_iota(jnp.int32, (tm, rows), 1) + r0
        onehot = jnp.where(rcol_ref[0] == slot, 1.0, 0.0).astype(BF16)
        o_ref[...] = o_ref[...] + wcol_ref[0] * _dot_exact_lhs(onehot, yb_sc[pl.ds(r0, rows), :])
        return carry

    @pl.when(c == n_c - 1)
    def _():
        lax.fori_loop(0, n_blk, scatter, 0)

    @pl.when((e == n_e - 1) & (c == n_c - 1))
    def _():
        o_ref[...] = _layer_norm_rows(o_ref[...], ln_ref[0:1, :], ln_ref[1:2, :])


def _moe(x2, xb2, counts, rank_row, rank_col, w_col, wg, wu, wd, ln, tm, rows):
    n = x2.shape[0]
    nt = n // tm
    cap = -(-tm // rows) * rows
    grid_spec = pltpu.PrefetchScalarGridSpec(
        num_scalar_prefetch=1,
        grid=(nt, N_EXPERTS, FF_SPLIT),
        in_specs=[
            pl.BlockSpec((tm, D_MODEL), lambda i, e, c, cnt: (i, 0)),
            pl.BlockSpec((tm, D_MODEL), lambda i, e, c, cnt: (i, 0)),
            pl.BlockSpec((1, 1, tm), lambda i, e, c, cnt: (e, 0, i)),
            pl.BlockSpec((1, tm, 1), lambda i, e, c, cnt: (e, i, 0)),
            pl.BlockSpec((1, tm, 1), lambda i, e, c, cnt: (e, i, 0)),
            pl.BlockSpec((1, D_MODEL, FF_PART), lambda i, e, c, cnt: (e, 0, c)),
            pl.BlockSpec((1, D_MODEL, FF_PART), lambda i, e, c, cnt: (e, 0, c)),
            pl.BlockSpec((1, FF_PART, D_MODEL), lambda i, e, c, cnt: (e, c, 0)),
            pl.BlockSpec((2, D_MODEL), lambda i, e, c, cnt: (0, 0)),
        ],
        out_specs=pl.BlockSpec((tm, D_MODEL), lambda i, e, c, cnt: (i, 0)),
        scratch_shapes=[pltpu.VMEM((cap, D_MODEL), BF16), pltpu.VMEM((cap, D_MODEL), F32)],
    )
    return pl.pallas_call(
        functools.partial(_moe_kernel, rows),
        grid_spec=grid_spec,
        out_shape=jax.ShapeDtypeStruct((n, D_MODEL), F32),
        compiler_params=_params(("arbitrary", "arbitrary", "arbitrary"), 56),
        name="moe",
    )(counts, x2, xb2, rank_row, rank_col, w_col, wg, wu, wd, ln)


def _tile_sizes(b, t):
    n = b * t
    tm = min(512, n)
    tm_proj = min(256, n)
    tt = min(256, t)
    tm_moe = min(1024, n)
    rows = 320 if tm_moe == 1024 else 160
    return tm, tm_proj, tt, tm_moe, rows


def _to_block_diag(s):
    b, h, r, c = s.shape
    eye = jnp.eye(h, dtype=s.dtype)
    return jnp.einsum("bhrc,hg->bhrgc", s, eye).reshape(b, h * r, h * c)


def _from_block_diag(s, r, c):
    b = s.shape[0]
    s5 = s.reshape(b, N_HEADS, r, N_HEADS, c)
    return jnp.stack([s5[:, h, :, h, :] for h in range(N_HEADS)], axis=1)


def _prep_layer(l, p):
    d = D_MODEL
    bw = BR_WIDTH
    w = {}
    w["win"] = p["w_in"][l].astype(BF16)
    wl1 = jnp.zeros((d, LORA_COLS), F32)
    wl1mu = jnp.zeros((d, LORA_COLS), F32)
    w2 = jnp.zeros((LORA_COLS, AUX_COLS), F32)
    bias = jnp.zeros((1, AUX_COLS), F32)
    mu = p["rwkv_mu_x"][l]
    c0 = 0
    segs = [(p["rwkv_w1"][l], mu[0], p["rwkv_w2"][l], 0),
            (p["rwkv_a1"][l], mu[1], p["rwkv_a2"][l], bw),
            (p["rwkv_g1"][l], mu[2], p["rwkv_g2"][l], 2 * bw)]
    if l >= 1:
        segs.append((p["rwkv_v1"][l - 1], p["rwkv_mu_v"][l - 1], p["rwkv_v2"][l - 1], 3 * bw))
    else:
        segs.append(None)
    segs.append((p["gla_w1"][l], None, p["gla_w2"][l], 4 * bw))
    widths = [RWKV_W_LORA, RWKV_A_LORA, RWKV_G_LORA, RWKV_V_LORA, GLA_GATE_RANK]
    for seg, width in zip(segs, widths):
        if seg is not None:
            w1, m, w2s, col = seg
            wl1 = wl1.at[:, c0:c0 + width].set(w1)
            if m is not None:
                wl1mu = wl1mu.at[:, c0:c0 + width].set(m[:, None] * w1)
            w2 = w2.at[c0:c0 + width, col:col + w2s.shape[1]].set(w2s)
        c0 += width
    bias = bias.at[0, 0:bw].set(p["rwkv_w0"][l])
    bias = bias.at[0, bw:2 * bw].set(p["rwkv_a0"][l])
    if l >= 1:
        bias = bias.at[0, 3 * bw:4 * bw].set(p["rwkv_v0"][l - 1])
    bias = bias.at[0, 4 * bw:].set(p["gla_b"][l])
    w["wl1"] = wl1.astype(BF16)
    w["wl1mu"] = wl1mu.astype(BF16)
    w["w2"] = w2.astype(BF16)
    w["bias"] = bias
    prm = jnp.zeros((SUBLANE, RWKV_COLS), F32)
    prm = prm.at[0].set(p["rwkv_mu_rkv"][l].reshape(RWKV_COLS))
    for row, name in enumerate(("rwkv_k_k", "rwkv_k_a", "rwkv_r_k", "rwkv_ln_g", "rwkv_ln_b"), start=1):
        prm = prm.at[row, 0:bw].set(p[name][l])
    w["rwkv_prm"] = prm
    w["ret_prm"] = jnp.stack([p["ret_gn_g"][l], p["ret_gn_b"][l]])
    w["hgrn_ng"] = p["hgrn_norm_g"][l][None]
    w["gla_ng"] = p["gla_norm_g"][l][None]
    w["wg"] = p["w_gate"][l].astype(BF16)
    w["bg"] = p["b_gate"][l]
    w["wbr"] = p["w_br"][l].astype(BF16)
    w["wo"] = p["w_o"][l].astype(BF16)
    w["ln1"] = jnp.stack([p["ln1_g"][l], p["ln1_b"][l]])
    w["ln2"] = jnp.stack([p["ln2_g"][l], p["ln2_b"][l]])
    j = l // 2
    if l % 2 == 0:
        w["ffn"] = (p["ffn_w_gate"][j].astype(BF16), p["ffn_w_up"][j].astype(BF16),
                    p["ffn_w_down"][j].astype(BF16))
    else:
        wr = jnp.zeros((d, LANE), F32).at[:, :N_EXPERTS].set(p["router_w"][j])
        br = jnp.zeros((1, LANE), F32).at[0, :N_EXPERTS].set(p["router_b"][j])
        w["moe"] = (wr, br, p["moe_w_gate"][j].astype(BF16), p["moe_w_up"][j].astype(BF16),
                    p["moe_w_down"][j].astype(BF16))
    return w


def _mixer_consts():
    bdm = _np_block_mask(HEAD_DIM, HEAD_DIM)
    lg = np.log1p(-np.exp2(-5.0 - np.arange(N_HEADS, dtype=np.float64)))
    lg_l = np.repeat(lg, HEAD_DIM)[None, :]
    t = np.arange(CHUNK, dtype=np.float64)[:, None]
    s_side = (np.arange(N_HEADS * CHUNK) % CHUNK)[None, :].astype(np.float64)
    lg_side = np.repeat(lg, CHUNK)[None, :]
    d_mat = np.where(s_side <= t, np.exp((t - s_side) * lg_side), 0.0)
    dec = np.zeros((3 * CHUNK + SUBLANE, BR_WIDTH), np.float64)
    dec[0:CHUNK] = np.exp((t + 1.0) * lg_l)
    dec[CHUNK:2 * CHUNK] = np.exp((CHUNK - 1.0 - t) * lg_l)
    dec[2 * CHUNK:3 * CHUNK] = d_mat
    dec[3 * CHUNK] = np.exp(CHUNK * lg_l[0])
    return {
        "tril": jnp.asarray(_np_tril(CHUNK), BF16),
        "bdm": jnp.asarray(bdm, F32),
        "strict": jnp.asarray(_np_causal_side_by_side(True), F32),
        "incl": jnp.asarray(_np_causal_side_by_side(False), F32),
        "eye": jnp.asarray(np.tile(np.eye(CHUNK, dtype=np.float32), (1, N_HEADS)), F32),
        "ret_dec": jnp.asarray(dec, F32),
        "gla_pair": jnp.asarray(_np_block_mask(GLA_DK, HEAD_DIM), BF16),
        "gla_mask": jnp.asarray(_np_block_mask(HEAD_DIM, GLA_DK), F32),
    }


def _rope_tables(pos0, t):
    half = HEAD_DIM // 2
    pos = pos0 + jnp.arange(t, dtype=F32)
    inv = ROPE_THETA ** (-jnp.arange(half, dtype=F32) / half)
    ang = pos[:, None] * inv[None]
    cos = jnp.cos(ang)
    sin = jnp.sin(ang)
    cos_t = jnp.tile(jnp.concatenate([cos, cos], axis=1), (1, N_HEADS))
    sin_t = jnp.tile(jnp.concatenate([-sin, sin], axis=1), (1, N_HEADS))
    return cos_t, sin_t


def _run_trunk(x, pos0, s_rwkv, c_shift, s_ret, s_hgrn, s_gla, prm, layers, consts):
    b, t, d = x.shape
    n = b * t
    tm, tm_proj, tt, tm_moe, rows = _tile_sizes(b, t)
    cos_t, sin_t = _rope_tables(pos0, t)
    v_first = None
    new_rwkv, new_shift, new_ret, new_hgrn, new_gla = [], [], [], [], []
    for l in range(DEPTH):
        w = layers[l]
        x_in = x
        x_last = c_shift[l]
        x_prev = jnp.concatenate([x_last[:, None], x[:, :-1]], axis=1)
        p2, aux2 = _in_proj(x.reshape(n, d), x_prev.reshape(n, d), w["win"], w["wl1"], w["wl1mu"],
                            w["w2"], w["bias"], tm_proj)
        p3 = p2.reshape(b, t, IN_COLS)
        aux3 = aux2.reshape(b, t, AUX_COLS)
        pad = (-b) % SUBLANE
        x_last_p = jnp.concatenate([x_last, jnp.zeros((pad, d), F32)], axis=0) if pad else x_last
        rkv_last = _rows_matmul(x_last_p, w["win"][:, :RWKV_COLS])[:b, None, :]

        st0 = _to_block_diag(s_rwkv[l])
        res = _rwkv_mixer(p3, aux3, v_first, rkv_last, st0, w["rwkv_prm"], consts, tt)
        if v_first is None:
            o_rwkv, v_first, st_rwkv = res
        else:
            o_rwkv, st_rwkv = res
        o_ret, st_ret = _ret_mixer(p3, cos_t, sin_t, _to_block_diag(jnp.swapaxes(s_ret[l], -1, -2)),
                                   w["ret_prm"], consts["ret_dec"], consts["bdm"], tt)
        o_hgrn, st_hgrn = _hgrn_mixer(p3, _to_block_diag(jnp.swapaxes(s_hgrn[l], -1, -2)),
                                      prm["hgrn_lb_logits"], w["hgrn_ng"], consts["tril"], consts["bdm"],
                                      l, tt)
        o_gla, st_gla = _gla_mixer(p3, aux3, _to_block_diag(jnp.swapaxes(s_gla[l], -1, -2)), w["gla_ng"],
                                   consts["tril"], consts["gla_pair"], consts["gla_mask"], consts["bdm"], tt)

        outs = [o.reshape(n, BR_WIDTH) for o in (o_rwkv, o_ret, o_hgrn, o_gla)]
        is_moe = l % 2 == 1
        merged = _merge(x.reshape(n, d), outs, w["wg"], w["bg"], w["wbr"], w["wo"], w["ln1"], tm, is_moe)
        if not is_moe:
            x1 = merged[0]
            x2 = _ffn(x1, *w["ffn"], w["ln2"], tm)
        else:
            x1, x1b = merged
            wr, br, mg, mu_, md = w["moe"]
            tril_m = jnp.asarray(np.tril(np.ones((tm_moe, tm_moe), np.float32), -1), BF16)
            rank, wsel, cnt = _router(x1, wr, br, tril_m, tm_moe)
            rank_row = rank[:, :N_EXPERTS].T.reshape(N_EXPERTS, 1, n)
            rank_col = rank[:, :N_EXPERTS].T.reshape(N_EXPERTS, n, 1)
            w_col = wsel[:, :N_EXPERTS].T.reshape(N_EXPERTS, n, 1)
            counts = cnt[:, 0, :N_EXPERTS].reshape(-1)
            x2 = _moe(x1, x1b, counts, rank_row, rank_col, w_col, mg, mu_, md, w["ln2"], tm_moe, rows)
        x = x2.reshape(b, t, d)

        new_rwkv.append(_from_block_diag(st_rwkv, HEAD_DIM, HEAD_DIM))
        new_shift.append(x_in[:, -1])
        new_ret.append(jnp.swapaxes(_from_block_diag(st_ret, HEAD_DIM, HEAD_DIM), -1, -2))
        new_hgrn.append(jnp.swapaxes(_from_block_diag(st_hgrn, HEAD_DIM, HEAD_DIM), -1, -2))
        new_gla.append(jnp.swapaxes(_from_block_diag(st_gla, HEAD_DIM, GLA_DK), -1, -2))
    return (x, jnp.stack(new_rwkv), jnp.stack(new_shift), jnp.stack(new_ret), jnp.stack(new_hgrn),
            jnp.stack(new_gla))


def kernel(x_prompt, x_sample, state_rwkv, cache_shift, state_ret, state_hgrn, state_gla, w_in, rwkv_mu_rkv, rwkv_mu_x, rwkv_mu_v, rwkv_w0, rwkv_w1, rwkv_w2, rwkv_a0, rwkv_a1, rwkv_a2, rwkv_v0, rwkv_v1, rwkv_v2, rwkv_g1, rwkv_g2, rwkv_k_k, rwkv_k_a, rwkv_r_k, rwkv_ln_g, rwkv_ln_b, ret_gn_g, ret_gn_b, hgrn_lb_logits, hgrn_norm_g, gla_w1, gla_w2, gla_b, gla_norm_g, w_br, w_gate, b_gate, w_o, ln1_g, ln1_b, ln2_g, ln2_b, ffn_w_gate, ffn_w_up, ffn_w_down, router_w, router_b, moe_w_gate, moe_w_up, moe_w_down):
    prm = {
        'w_in': w_in, 'rwkv_mu_rkv': rwkv_mu_rkv, 'rwkv_mu_x': rwkv_mu_x, 'rwkv_mu_v': rwkv_mu_v,
        'rwkv_w0': rwkv_w0, 'rwkv_w1': rwkv_w1, 'rwkv_w2': rwkv_w2,
        'rwkv_a0': rwkv_a0, 'rwkv_a1': rwkv_a1, 'rwkv_a2': rwkv_a2,
        'rwkv_v0': rwkv_v0, 'rwkv_v1': rwkv_v1, 'rwkv_v2': rwkv_v2,
        'rwkv_g1': rwkv_g1, 'rwkv_g2': rwkv_g2, 'rwkv_k_k': rwkv_k_k, 'rwkv_k_a': rwkv_k_a,
        'rwkv_r_k': rwkv_r_k, 'rwkv_ln_g': rwkv_ln_g, 'rwkv_ln_b': rwkv_ln_b,
        'ret_gn_g': ret_gn_g, 'ret_gn_b': ret_gn_b, 'hgrn_lb_logits': hgrn_lb_logits,
        'hgrn_norm_g': hgrn_norm_g, 'gla_w1': gla_w1, 'gla_w2': gla_w2, 'gla_b': gla_b,
        'gla_norm_g': gla_norm_g, 'w_br': w_br, 'w_gate': w_gate, 'b_gate': b_gate, 'w_o': w_o,
        'ln1_g': ln1_g, 'ln1_b': ln1_b, 'ln2_g': ln2_g, 'ln2_b': ln2_b,
        'ffn_w_gate': ffn_w_gate, 'ffn_w_up': ffn_w_up, 'ffn_w_down': ffn_w_down,
        'router_w': router_w, 'router_b': router_b,
        'moe_w_gate': moe_w_gate, 'moe_w_up': moe_w_up, 'moe_w_down': moe_w_down,
    }
    layers = [_prep_layer(l, prm) for l in range(DEPTH)]
    consts = _mixer_consts()
    bp = x_prompt.shape[0]
    zero_hd = jnp.zeros((DEPTH, bp, N_HEADS, HEAD_DIM, HEAD_DIM), F32)
    zero_shift = jnp.zeros((DEPTH, bp, D_MODEL), F32)
    zero_gla = jnp.zeros((DEPTH, bp, N_HEADS, GLA_DK, HEAD_DIM), F32)
    prompt = _run_trunk(x_prompt, 0.0, zero_hd, zero_shift, zero_hd, zero_hd, zero_gla, prm, layers, consts)
    sample = _run_trunk(x_sample, float(PAST_LEN), state_rwkv, cache_shift, state_ret, state_hgrn,
                        state_gla, prm, layers, consts)
    y_p, p_rwkv, p_shift, p_ret, p_hgrn, p_gla = prompt
    y_s, s_rwkv, s_shift, s_ret, s_hgrn, s_gla = sample
    return (y_p, y_s, p_rwkv, p_shift, p_ret, p_hgrn, p_gla, s_rwkv, s_shift, s_ret, s_hgrn, s_gla)
```

```python
import functools
import math

import numpy as np
import jax
import jax.numpy as jnp
import jax.scipy.linalg
from jax import lax
from jax.experimental import pallas as pl
from jax.experimental.pallas import tpu as pltpu

F32 = jnp.float32
BF16 = jnp.bfloat16

D_MODEL = 1024
DEPTH = 2
PAST_LEN = 4096
CHUNK = 64
SUB = 16
N_BRANCH = 4
BR_WIDTH = D_MODEL // N_BRANCH
HEAD_DIM = 64
N_HEADS = BR_WIDTH // HEAD_DIM
GLA_DK = HEAD_DIM // 2
GLA_KW = N_HEADS * GLA_DK
GLA_GATE_RANK = 16
GLA_TAU = 16.0
RWKV_W_LORA = 32
RWKV_A_LORA = 32
RWKV_V_LORA = 16
RWKV_G_LORA = 64
RWKV_GN_EPS = 64e-5
ROPE_THETA = 10000.0
LN_EPS = 1e-5
D_FF = 2816
N_EXPERTS = 8
ALPHA = (2.0 * DEPTH) ** 0.25
RWKV_COLS = 3 * BR_WIDTH
IN_COLS = 3584
LORA_COLS = 256
AUX_COLS = 4 * BR_WIDTH + GLA_KW
SEQ_PAIR = 2

LANE = 128
SUBLANE = 8

NN = (((1,), (0,)), ((), ()))
NT = (((1,), (1,)), ((), ()))
TN = (((0,), (0,)), ((), ()))


def _params(sem, vmem_mib):
    return pltpu.CompilerParams(dimension_semantics=sem, vmem_limit_bytes=vmem_mib * 1024 * 1024)


def _const_spec(shape):
    nd = len(shape)
    return pl.BlockSpec(shape, lambda *_: (0,) * nd, pipeline_mode=pl.Buffered(1))


def _dot(a, b, dims=NN):
    return lax.dot_general(a.astype(BF16), b.astype(BF16), dims, preferred_element_type=F32)


def _split(x):
    hi = x.astype(BF16)
    lo = (x - hi.astype(F32)).astype(BF16)
    return hi, lo


def _dot_exact_lhs(a_bf, x):
    hi, lo = _split(x)
    return (jnp.dot(a_bf, hi, preferred_element_type=F32)
            + jnp.dot(a_bf, lo, preferred_element_type=F32))


def _dot_exact_rhs(x, b_bf):
    hi, lo = _split(x)
    return (jnp.dot(hi, b_bf, preferred_element_type=F32)
            + jnp.dot(lo, b_bf, preferred_element_type=F32))


def _dot3(a, b, dims=NN):
    ah, al = _split(a)
    bh, bl = _split(b)
    d = functools.partial(lax.dot_general, dimension_numbers=dims, preferred_element_type=F32)
    return d(ah, bh) + (d(ah, bl) + d(al, bh))


def _sigmoid(x):
    return 1.0 / (1.0 + jnp.exp(-x))


def _softplus(x):
    return jnp.maximum(x, 0.0) + jnp.log(1.0 + jnp.exp(-jnp.abs(x)))


def _silu(x):
    return x * _sigmoid(x)


def _layer_norm_rows(y, g, b):
    mu = jnp.mean(y, axis=-1, keepdims=True)
    yc = y - mu
    var = jnp.mean(yc * yc, axis=-1, keepdims=True)
    return yc * lax.rsqrt(var + LN_EPS) * g + b


def _np_block_mask(rows_per_head, cols_per_head):
    r = np.arange(N_HEADS * rows_per_head)[:, None] // rows_per_head
    c = np.arange(N_HEADS * cols_per_head)[None, :] // cols_per_head
    return (r == c).astype(np.float32)


def _np_head_lanes(cols_per_head):
    m = np.zeros((SUBLANE, N_HEADS * cols_per_head), np.float32)
    for h in range(N_HEADS):
        m[h, h * cols_per_head:(h + 1) * cols_per_head] = 1.0
    return m


def _np_causal_side_by_side(strict):
    t = np.arange(CHUNK)[:, None]
    s = np.arange(N_HEADS * CHUNK)[None, :] % CHUNK
    return ((s < t) if strict else (s <= t)).astype(np.float32)


def _in_proj_kernel(x_ref, prev_ref, win_ref, wl1_ref, wl1mu_ref, w2_ref, bias_ref, p_ref, aux_ref):
    x = x_ref[...]
    row = lax.broadcasted_iota(jnp.int32, x.shape, 0)
    xx = jnp.where(row == 0, prev_ref[0], pltpu.roll(x, 1, axis=0)) - x
    xb = x.astype(BF16)
    p_ref[...] = jnp.dot(xb, win_ref[...], preferred_element_type=F32)
    h = (jnp.dot(xb, wl1_ref[...], preferred_element_type=F32)
         + jnp.dot(xx.astype(BF16), wl1mu_ref[...], preferred_element_type=F32))
    lane = lax.broadcasted_iota(jnp.int32, h.shape, 1)
    act = jnp.where(lane < RWKV_W_LORA, jnp.tanh(h), h)
    g_lo = RWKV_W_LORA + RWKV_A_LORA
    in_g = jnp.where(lane >= g_lo, jnp.where(lane < g_lo + RWKV_G_LORA, 1.0, 0.0), 0.0)
    act = jnp.where(in_g > 0.5, _sigmoid(h), act)
    z = jnp.dot(act.astype(BF16), w2_ref[...], preferred_element_type=F32) + bias_ref[...]
    bw = BR_WIDTH
    w_log = -_softplus(-z[:, 0:bw]) - 0.5
    aux_ref[:, 0:bw] = -jnp.exp(w_log)
    aux_ref[:, bw:2 * bw] = _sigmoid(z[:, bw:2 * bw])
    aux_ref[:, 2 * bw:3 * bw] = z[:, 2 * bw:3 * bw]
    aux_ref[:, 3 * bw:4 * bw] = _sigmoid(z[:, 3 * bw:4 * bw])
    zg = z[:, 4 * bw:]
    aux_ref[:, 4 * bw:] = (jnp.minimum(zg, 0.0) - jnp.log(1.0 + jnp.exp(-jnp.abs(zg)))) * (1.0 / GLA_TAU)


def _in_proj(x2, prev_rows, win, wl1, wl1mu, w2, bias, tm):
    n = x2.shape[0]
    return pl.pallas_call(
        _in_proj_kernel,
        grid=(n // tm,),
        in_specs=[
            pl.BlockSpec((tm, D_MODEL), lambda i: (i, 0)),
            pl.BlockSpec((1, 1, D_MODEL), lambda i: (i, 0, 0)),
            _const_spec(win.shape), _const_spec(wl1.shape), _const_spec(wl1mu.shape),
            _const_spec(w2.shape), _const_spec(bias.shape),
        ],
        out_specs=[pl.BlockSpec((tm, IN_COLS), lambda i: (i, 0)),
                   pl.BlockSpec((tm, AUX_COLS), lambda i: (i, 0))],
        out_shape=[jax.ShapeDtypeStruct((n, IN_COLS), F32), jax.ShapeDtypeStruct((n, AUX_COLS), F32)],
        compiler_params=_params(("arbitrary",), 56),
        name="in_proj",
    )(x2, prev_rows, win, wl1, wl1mu, w2, bias)


def _rows_matmul_kernel(x_ref, w_ref, o_ref):
    o_ref[...] = jnp.dot(x_ref[...].astype(BF16), w_ref[...], preferred_element_type=F32)


def _rows_matmul(x, w):
    return pl.pallas_call(
        _rows_matmul_kernel,
        out_shape=jax.ShapeDtypeStruct((x.shape[0], w.shape[1]), F32),
        name="rows_matmul",
    )(x, w)


def _stack_heads(x, head_lanes):
    return jnp.concatenate([x * head_lanes[h:h + 1, :] for h in range(N_HEADS)], axis=0)


def _head_sum(x, ones_bf):
    return _dot_exact_rhs(x, ones_bf)


def _head_layer_norm(y, ones_bf, g, b, eps):
    inv = 1.0 / HEAD_DIM
    mu = _head_sum(y, ones_bf) * inv
    yc = y - mu
    var = _head_sum(yc * yc, ones_bf) * inv
    return yc * lax.rsqrt(var + eps) * g + b


def _head_rms_norm(y, ones_bf, g):
    ms = _head_sum(y * y, ones_bf) * (1.0 / HEAD_DIM)
    return y * lax.rsqrt(ms + 1e-6) * g


def _pairwise_block(q, k, v, b, pair_ones_bf):
    parts = []
    for j in range(SUB):
        lo = (j // SUBLANE) * SUBLANE
        p = q[lo:] * jnp.exp(jnp.minimum(b[lo:] - b[j:j + 1], 0.0)) * k[j:j + 1]
        if j % SUBLANE:
            rid = lax.broadcasted_iota(jnp.int32, p.shape, 0) + lo
            p = jnp.where(rid >= j, p, 0.0)
        parts.append(p)
    att = jnp.dot(jnp.concatenate(parts, axis=0).astype(BF16), pair_ones_bf, preferred_element_type=F32)
    outs = []
    off = 0
    for g in range(SUB // SUBLANE):
        rows = SUB - g * SUBLANE
        acc = None
        for j in range(g * SUBLANE, (g + 1) * SUBLANE):
            term = att[off:off + rows] * v[j:j + 1]
            acc = term if acc is None else acc + term
            off += rows
        if g:
            acc = jnp.concatenate([jnp.zeros((g * SUBLANE, v.shape[1]), F32), acc], axis=0)
        outs.append(acc)
    total = outs[0]
    for extra in outs[1:]:
        total = total + extra
    return total


def _gla_state_free(tiles, tril_bf, pair_ones_bf, st_mask, lanes_k, lanes_v):
    n = range(len(tiles))
    q, k, v, glog = ([t[i] for t in tiles] for i in range(4))
    b = [_dot_exact_lhs(tril_bf, glog[i]) for i in n]
    blocks = [[] for _ in n]
    for blk in range(CHUNK // SUB):
        r0 = blk * SUB
        sl = slice(r0, r0 + SUB)
        o_blk = [_pairwise_block(q[i][sl], k[i][sl], v[i][sl], b[i][sl], pair_ones_bf) for i in n]
        if blk:
            c0 = [b[i][r0 - 1:r0] for i in n]
            q_t = [q[i][sl] * jnp.exp(b[i][sl] - c0[i]) for i in n]
            k_t = [k[i][:r0] * jnp.exp(c0[i] - b[i][:r0]) for i in n]
            att = [_dot(q_t[i], _stack_heads(k_t[i], lanes_k), NT) for i in n]
            o_blk = [o_blk[i] + _dot(att[i], _stack_heads(v[i][:r0], lanes_v)) for i in n]
        for i in n:
            blocks[i].append(o_blk[i])
    b_last = [b[i][CHUNK - 1:CHUNK, :] for i in n]
    upd = [st_mask * _dot(v[i], k[i] * jnp.exp(b_last[i] - b[i]), TN) for i in n]
    return [(jnp.concatenate(blocks[i], axis=0), q[i] * jnp.exp(b[i]), upd[i], jnp.exp(b_last[i]))
            for i in n]


def _rwkv_state_free(tiles, tril_bf, hl, strict, incl, eye):
    n = range(len(tiles))
    r, k, v, kk, bv, lw = ([t[i] for t in tiles] for i in range(6))
    l = [_dot_exact_lhs(tril_bf, lw[i]) for i in n]
    l_last = [l[i][CHUNK - 1:CHUNK, :] for i in n]
    e_neg = [jnp.exp(-l[i]) for i in n]
    lhs = [jnp.concatenate([kk[i] * jnp.exp(l[i] - lw[i]), r[i] * jnp.exp(l[i])], axis=0) for i in n]
    rhs = [jnp.concatenate([_stack_heads(k[i] * e_neg[i], hl), _stack_heads(bv[i] * e_neg[i], hl)], axis=0)
           for i in n]
    amat = [_dot(lhs[i], rhs[i], NT) for i in n]
    w = N_HEADS * CHUNK
    a_ab = [amat[i][:CHUNK, w:] * strict for i in n]
    a_vk = [jnp.concatenate([amat[i][:CHUNK, :w] * strict, amat[i][CHUNK:, :w] * incl], axis=0) for i in n]
    a_rb = [amat[i][CHUNK:, w:] * incl for i in n]
    x = [eye + a_ab[i] for i in n]
    m = a_ab
    for _ in range(int(math.log2(CHUNK)) - 1):
        m_st = [_stack_heads(m[i], hl) for i in n]
        m = [_dot(m[i], m_st[i]) for i in n]
        m_st = [_stack_heads(m[i], hl) for i in n]
        x = [x[i] + _dot(x[i], m_st[i]) for i in n]
    from_v = [_dot(a_vk[i], _stack_heads(v[i], hl)) for i in n]
    e_end = [jnp.exp(l_last[i] - l[i]) for i in n]
    upd_v = [_dot(v[i], k[i] * e_end[i], TN) for i in n]
    b_end = [bv[i] * e_end[i] for i in n]
    st_dec = [jnp.exp(l_last[i]) for i in n]
    return [(lhs[i], x[i], a_rb[i], from_v[i], upd_v[i], b_end[i], st_dec[i]) for i in n]


def _rwkv_state_step(parts, sts, bdm, hl):
    n = range(len(parts))
    lhs, x, a_rb, from_v, upd_v, b_end, st_dec = ([p[i] for p in parts] for i in range(7))
    from_state = [_dot(lhs[i], sts[i], NT) for i in n]
    u = [_dot(x[i], _stack_heads(from_state[i][:CHUNK] + from_v[i][:CHUNK], hl)) for i in n]
    upd = [upd_v[i] + _dot(u[i], b_end[i], TN) for i in n]
    new = [sts[i] * st_dec[i] + upd[i] * bdm for i in n]
    y = [from_state[i][CHUNK:] + from_v[i][CHUNK:] + _dot(a_rb[i], _stack_heads(u[i], hl)) for i in n]
    return y, new


def _rwkv_kernel(has_vres, *refs):
    if has_vres:
        (rkv_ref, lw_ref, a_ref, g_ref, vg_ref, vf_ref, last_ref, st0_ref, prm_ref, tril_ref, bdm_ref,
         hl_ref, strict_ref, incl_ref, eye_ref, o_ref, st_out_ref, st_sc, prev_sc) = refs
    else:
        (rkv_ref, lw_ref, a_ref, g_ref, last_ref, st0_ref, prm_ref, tril_ref, bdm_ref,
         hl_ref, strict_ref, incl_ref, eye_ref, o_ref, v_out_ref, st_out_ref, st_sc, prev_sc) = refs
    tb = pl.program_id(1)

    @pl.when(tb == 0)
    def _():
        st_sc[...] = st0_ref[...]
        prev_sc[...] = last_ref[...]

    bw = BR_WIDTH
    k_k = prm_ref[1:2, 0:bw]
    k_a = prm_ref[2:3, 0:bw]
    r_k = prm_ref[3:4, 0:bw]
    ln_g = prm_ref[4:5, 0:bw]
    ln_b = prm_ref[5:6, 0:bw]
    bdm = bdm_ref[...]
    ones_bf = bdm.astype(BF16)
    hl = hl_ref[...]
    tril_bf = tril_ref[...]
    strict = strict_ref[...]
    incl = incl_ref[...]
    eye = eye_ref[...]
    n_seq = rkv_ref.shape[0]
    tt = rkv_ref.shape[1]
    seqs = []
    for s in range(n_seq):
        rkv = rkv_ref[s]
        row = lax.broadcasted_iota(jnp.int32, rkv.shape, 0)
        prev = jnp.where(row == 0, prev_sc[s], pltpu.roll(rkv, 1, axis=0))
        prev_sc[s] = rkv[tt - 1:tt, :]
        mixed = rkv + (prev - rkv) * prm_ref[0:1, :]
        r = mixed[:, 0:bw]
        k = mixed[:, bw:2 * bw]
        v = mixed[:, 2 * bw:]
        a = a_ref[s]
        if has_vres:
            v = v + (vf_ref[s] - v) * vg_ref[s]
        else:
            v_out_ref[s] = v
        kk = k * k_k
        kk = kk * lax.rsqrt(jnp.maximum(_head_sum(kk * kk, ones_bf), 1e-24))
        k = k * (1.0 + (a - 1.0) * k_a)
        seqs.append((r, k, v, kk, -(kk * a), lw_ref[s]))
    n_chunks = tt // CHUNK
    tiles = [tuple(z[c * CHUNK:(c + 1) * CHUNK] for z in seqs[s])
             for c in range(n_chunks) for s in range(n_seq)]
    parts = _rwkv_state_free(tiles, tril_bf, hl, strict, incl, eye)
    sts = [st_sc[s] for s in range(n_seq)]
    ys = [[] for _ in range(n_seq)]
    for c in range(n_chunks):
        y_c, sts = _rwkv_state_step(parts[c * n_seq:(c + 1) * n_seq], sts, bdm, hl)
        for s in range(n_seq):
            ys[s].append(y_c[s])
    for s in range(n_seq):
        r, k, v = seqs[s][0:3]
        st_sc[s] = sts[s]
        y = ys[s][0] if len(ys[s]) == 1 else jnp.concatenate(ys[s], axis=0)
        y = _head_layer_norm(y, ones_bf, ln_g, ln_b, RWKV_GN_EPS)
        bonus = _head_sum(r * k * r_k, ones_bf) * v
        o_ref[s] = (y + bonus) * g_ref[s]

    @pl.when(tb == pl.num_programs(1) - 1)
    def _():
        for s in range(n_seq):
            st_out_ref[s] = sts[s]


def _rwkv_mixer(p3, aux3, v_first, rkv_last, st0, prm, consts, tt):
    b, t, _ = p3.shape
    bw = BR_WIDTH
    pb = SEQ_PAIR
    has_vres = v_first is not None
    tok = lambda j: pl.BlockSpec((pb, tt, bw), lambda bi, ti, j=j: (bi, ti, j))
    in_specs = [pl.BlockSpec((pb, tt, RWKV_COLS), lambda bi, ti: (bi, ti, 0)),
                tok(0), tok(1), tok(2)]
    args = [p3, aux3, aux3, aux3]
    if has_vres:
        in_specs += [tok(3), pl.BlockSpec((pb, tt, bw), lambda bi, ti: (bi, ti, 0))]
        args += [aux3, v_first]
    in_specs += [pl.BlockSpec((pb, 1, RWKV_COLS), lambda bi, ti: (bi, 0, 0)),
                 pl.BlockSpec((pb, bw, bw), lambda bi, ti: (bi, 0, 0)),
                 _const_spec(prm.shape)]
    args += [rkv_last, st0, prm]
    for name in ("tril", "bdm", "head_lanes", "strict", "incl", "eye"):
        in_specs.append(_const_spec(consts[name].shape))
        args.append(consts[name])
    seq = pl.BlockSpec((pb, tt, bw), lambda bi, ti: (bi, ti, 0))
    st_spec = pl.BlockSpec((pb, bw, bw), lambda bi, ti: (bi, 0, 0))
    seq_shape = jax.ShapeDtypeStruct((b, t, bw), F32)
    st_shape = jax.ShapeDtypeStruct((b, bw, bw), F32)
    if has_vres:
        out_specs, out_shape = [seq, st_spec], [seq_shape, st_shape]
    else:
        out_specs, out_shape = [seq, seq, st_spec], [seq_shape, seq_shape, st_shape]
    return pl.pallas_call(
        functools.partial(_rwkv_kernel, has_vres),
        grid=(b // pb, t // tt),
        in_specs=in_specs, out_specs=out_specs, out_shape=out_shape,
        scratch_shapes=[pltpu.VMEM((pb, bw, bw), F32), pltpu.VMEM((pb, 1, RWKV_COLS), F32)],
        compiler_params=_params(("arbitrary", "arbitrary"), 48),
        name="rwkv_mixer",
    )(*args)


def _rot_half(z):
    w = z.shape[1]
    half = HEAD_DIM // 2
    lane = lax.broadcasted_iota(jnp.int32, z.shape, 1)
    first = (lane % HEAD_DIM) < half
    return jnp.where(first, pltpu.roll(z, w - half, axis=1), pltpu.roll(z, half, axis=1))


def _ret_kernel(q_ref, k_ref, v_ref, g_ref, cos_ref, sin_ref, st0_ref, prm_ref, dec_ref, bdm_ref, hl_ref,
                o_ref, st_out_ref, st_sc):
    tb = pl.program_id(1)

    @pl.when(tb == 0)
    def _():
        st_sc[...] = st0_ref[...]

    cos = cos_ref[...]
    sin = sin_ref[...]
    bdm = bdm_ref[...]
    hl = hl_ref[...]
    ones_bf = bdm.astype(BF16)
    q_dec = dec_ref[0:CHUNK, :]
    k_dec = dec_ref[CHUNK:2 * CHUNK, :]
    d_mat = dec_ref[2 * CHUNK:3 * CHUNK, :]
    s_dec = dec_ref[3 * CHUNK:3 * CHUNK + 1, :]
    n_seq = q_ref.shape[0]
    tt = q_ref.shape[1]
    qs, ks, vs = [], [], []
    for s in range(n_seq):
        q = q_ref[s]
        k = k_ref[s]
        qs.append(q * cos + _rot_half(q) * sin)
        ks.append((k * cos + _rot_half(k) * sin) * (HEAD_DIM ** -0.5))
        vs.append(v_ref[s])
    sts = [st_sc[s] for s in range(n_seq)]
    outs = [[] for _ in range(n_seq)]
    for c in range(tt // CHUNK):
        sl = slice(c * CHUNK, (c + 1) * CHUNK)
        for s in range(n_seq):
            qc, kc, vc = qs[s][sl], ks[s][sl], vs[s][sl]
            att = _dot(qc, _stack_heads(kc, hl), NT) * d_mat
            outs[s].append(_dot(qc * q_dec, sts[s], NT) + _dot(att, _stack_heads(vc, hl)))
            sts[s] = sts[s] * s_dec + bdm * _dot(vc, kc * k_dec, TN)
    for s in range(n_seq):
        st_sc[s] = sts[s]
        o = outs[s][0] if len(outs[s]) == 1 else jnp.concatenate(outs[s], axis=0)
        o = _head_layer_norm(o, ones_bf, prm_ref[0:1, :], prm_ref[1:2, :], LN_EPS)
        o_ref[s] = o * _silu(g_ref[s])

    @pl.when(tb == pl.num_programs(1) - 1)
    def _():
        for s in range(n_seq):
            st_out_ref[s] = sts[s]


def _ret_mixer(p3, cos_t, sin_t, st0, prm, consts, tt):
    b, t, _ = p3.shape
    bw = BR_WIDTH
    pb = SEQ_PAIR
    base = RWKV_COLS // bw
    tok = lambda j: pl.BlockSpec((pb, tt, bw), lambda bi, ti, j=j: (bi, ti, base + j))
    tab = pl.BlockSpec((tt, bw), lambda bi, ti: (ti, 0))
    st_spec = pl.BlockSpec((pb, bw, bw), lambda bi, ti: (bi, 0, 0))
    dec, bdm, hl = consts["ret_dec"], consts["bdm"], consts["head_lanes"]
    return pl.pallas_call(
        _ret_kernel,
        grid=(b // pb, t // tt),
        in_specs=[tok(0), tok(1), tok(2), tok(3), tab, tab, st_spec, _const_spec(prm.shape),
                  _const_spec(dec.shape), _const_spec(bdm.shape), _const_spec(hl.shape)],
        out_specs=[pl.BlockSpec((pb, tt, bw), lambda bi, ti: (bi, ti, 0)), st_spec],
        out_shape=[jax.ShapeDtypeStruct((b, t, bw), F32), jax.ShapeDtypeStruct((b, bw, bw), F32)],
        scratch_shapes=[pltpu.VMEM((pb, bw, bw), F32)],
        compiler_params=_params(("arbitrary", "arbitrary"), 40),
        name="ret_mixer",
    )(p3, p3, p3, p3, cos_t, sin_t, st0, prm, dec, bdm, hl)


def _gated_mixer_tail(seqs, norm_g, st_sc, st_out_ref, o_ref, tril_bf, pair_ones_bf, st_mask,
                      lanes_k, lanes_v, ones_v_bf):
    n_seq = len(seqs)
    tt = seqs[0][0].shape[0]
    n_chunks = tt // CHUNK
    tiles = [tuple(z[c * CHUNK:(c + 1) * CHUNK] for z in seqs[s][0:4])
             for c in range(n_chunks) for s in range(n_seq)]
    parts = _gla_state_free(tiles, tril_bf, pair_ones_bf, st_mask, lanes_k, lanes_v)
    sts = [st_sc[s] for s in range(n_seq)]
    outs = [[] for _ in range(n_seq)]
    for c in range(n_chunks):
        for s in range(n_seq):
            o_intra, q_dec, upd, dec = parts[c * n_seq + s]
            outs[s].append(o_intra + _dot(q_dec, sts[s], NT))
            sts[s] = sts[s] * dec + upd
    for s in range(n_seq):
        st_sc[s] = sts[s]
        o = outs[s][0] if len(outs[s]) == 1 else jnp.concatenate(outs[s], axis=0)
        o_ref[s] = _head_rms_norm(o, ones_v_bf, norm_g) * _silu(seqs[s][4])

    @pl.when(pl.program_id(1) == pl.num_programs(1) - 1)
    def _():
        for s in range(n_seq):
            st_out_ref[s] = sts[s]


def _hgrn_kernel(layer, q_ref, f_ref, i_ref, g_ref, st0_ref, lbl_ref, ng_ref, tril_ref, bdm_ref, hl_ref,
                 o_ref, st_out_ref, st_sc):
    @pl.when(pl.program_id(1) == 0)
    def _():
        st_sc[...] = st0_ref[...]

    logits = lbl_ref[...]
    ex = jnp.exp(logits - jnp.max(logits, axis=0, keepdims=True))
    sm = ex / jnp.sum(ex, axis=0, keepdims=True)
    lb = jnp.zeros((1, BR_WIDTH), F32)
    for d in range(1, layer + 1):
        lb = lb + sm[d:d + 1, :]
    seqs = []
    for s in range(q_ref.shape[0]):
        fz = f_ref[s]
        f = lb + (1.0 - lb) * _sigmoid(fz)
        k = (1.0 - lb) * _sigmoid(-fz)
        seqs.append((_silu(q_ref[s]), k, i_ref[s], jnp.log(f), g_ref[s]))
    bdm = bdm_ref[...]
    ones_bf = bdm.astype(BF16)
    hl = hl_ref[...]
    _gated_mixer_tail(seqs, ng_ref[...], st_sc, st_out_ref, o_ref, tril_ref[...], ones_bf, bdm, hl, hl,
                      ones_bf)


def _hgrn_mixer(p3, st0, lb_logits, norm_g, consts, layer, tt):
    b, t, _ = p3.shape
    bw = BR_WIDTH
    pb = SEQ_PAIR
    base = (RWKV_COLS + 4 * bw) // bw
    tok = lambda j: pl.BlockSpec((pb, tt, bw), lambda bi, ti, j=j: (bi, ti, base + j))
    st_spec = pl.BlockSpec((pb, bw, bw), lambda bi, ti: (bi, 0, 0))
    tril, bdm, hl = consts["tril"], consts["bdm"], consts["head_lanes"]
    return pl.pallas_call(
        functools.partial(_hgrn_kernel, layer),
        grid=(b // pb, t // tt),
        in_specs=[tok(0), tok(1), tok(2), tok(3), st_spec, _const_spec(lb_logits.shape),
                  _const_spec(norm_g.shape), _const_spec(tril.shape), _const_spec(bdm.shape),
                  _const_spec(hl.shape)],
        out_specs=[pl.BlockSpec((pb, tt, bw), lambda bi, ti: (bi, ti, 0)), st_spec],
        out_shape=[jax.ShapeDtypeStruct((b, t, bw), F32), jax.ShapeDtypeStruct((b, bw, bw), F32)],
        scratch_shapes=[pltpu.VMEM((pb, bw, bw), F32)],
        compiler_params=_params(("arbitrary", "arbitrary"), 48),
        name="hgrn_mixer",
    )(p3, p3, p3, p3, st0, lb_logits, norm_g, tril, bdm, hl)


def _gla_kernel(q_ref, k_ref, v_ref, g_ref, la_ref, st0_ref, ng_ref, tril_ref, pair_ref, mask_ref,
                bdm_ref, hlk_ref, hlv_ref, o_ref, st_out_ref, st_sc):
    @pl.when(pl.program_id(1) == 0)
    def _():
        st_sc[...] = st0_ref[...]

    seqs = [(q_ref[s] * (GLA_DK ** -0.5), k_ref[s], v_ref[s], la_ref[s], g_ref[s])
            for s in range(q_ref.shape[0])]
    _gated_mixer_tail(seqs, ng_ref[...], st_sc, st_out_ref, o_ref, tril_ref[...], pair_ref[...],
                      mask_ref[...], hlk_ref[...], hlv_ref[...], bdm_ref[...].astype(BF16))


def _gla_mixer(p3, aux3, st0, norm_g, consts, tt):
    b, t, _ = p3.shape
    bw, kw = BR_WIDTH, GLA_KW
    pb = SEQ_PAIR
    gla0 = RWKV_COLS + 8 * bw
    st_spec = pl.BlockSpec((pb, bw, kw), lambda bi, ti: (bi, 0, 0))
    names = ("tril", "gla_pair", "gla_mask", "bdm", "gla_head_lanes", "head_lanes")
    return pl.pallas_call(
        _gla_kernel,
        grid=(b // pb, t // tt),
        in_specs=[pl.BlockSpec((pb, tt, kw), lambda bi, ti: (bi, ti, gla0 // kw)),
                  pl.BlockSpec((pb, tt, kw), lambda bi, ti: (bi, ti, gla0 // kw + 1)),
                  pl.BlockSpec((pb, tt, bw), lambda bi, ti: (bi, ti, (gla0 + 2 * kw) // bw)),
                  pl.BlockSpec((pb, tt, bw), lambda bi, ti: (bi, ti, (gla0 + 2 * kw) // bw + 1)),
                  pl.BlockSpec((pb, tt, kw), lambda bi, ti: (bi, ti, 4 * bw // kw)),
                  st_spec, _const_spec(norm_g.shape)] + [_const_spec(consts[nm].shape) for nm in names],
        out_specs=[pl.BlockSpec((pb, tt, bw), lambda bi, ti: (bi, ti, 0)), st_spec],
        out_shape=[jax.ShapeDtypeStruct((b, t, bw), F32), jax.ShapeDtypeStruct((b, bw, kw), F32)],
        scratch_shapes=[pltpu.VMEM((pb, bw, kw), F32)],
        compiler_params=_params(("arbitrary", "arbitrary"), 48),
        name="gla_mixer",
    )(p3, p3, p3, p3, aux3, st0, norm_g, *[consts[nm] for nm in names])


def _merge_kernel(emit_bf16, x_ref, o0_ref, o1_ref, o2_ref, o3_ref, wg_ref, bg_ref, wbr_ref, wo_ref,
                  ln_ref, *out_refs):
    x = x_ref[...]
    xb = x.astype(BF16)
    merged = None
    for m, o_ref in enumerate((o0_ref, o1_ref, o2_ref, o3_ref)):
        gate = _sigmoid(jnp.dot(xb, wg_ref[m], preferred_element_type=F32) + bg_ref[m:m + 1, :])
        term = gate * jnp.dot(o_ref[...].astype(BF16), wbr_ref[m], preferred_element_type=F32)
        merged = term if merged is None else merged + term
    y = ALPHA * x + jnp.dot(merged.astype(BF16), wo_ref[...], preferred_element_type=F32)
    y = _layer_norm_rows(y, ln_ref[0:1, :], ln_ref[1:2, :])
    out_refs[0][...] = y
    if emit_bf16:
        out_refs[1][...] = y.astype(BF16)


def _merge(x2, outs, wg, bg, wbr, wo, ln, tm, emit_bf16):
    n = x2.shape[0]
    row = pl.BlockSpec((tm, D_MODEL), lambda i: (i, 0))
    br = pl.BlockSpec((tm, BR_WIDTH), lambda i: (i, 0))
    out_specs = [row]
    out_shape = [jax.ShapeDtypeStruct((n, D_MODEL), F32)]
    if emit_bf16:
        out_specs.append(row)
        out_shape.append(jax.ShapeDtypeStruct((n, D_MODEL), BF16))
    return pl.pallas_call(
        functools.partial(_merge_kernel, emit_bf16),
        grid=(n // tm,),
        in_specs=[row, br, br, br, br, _const_spec(wg.shape), _const_spec(bg.shape),
                  _const_spec(wbr.shape), _const_spec(wo.shape), _const_spec(ln.shape)],
        out_specs=out_specs, out_shape=out_shape,
        compiler_params=_params(("arbitrary",), 56),
        name="merge",
    )(x2, *outs, wg, bg, wbr, wo, ln)


FF_SPLIT = 2
FF_PART = D_FF // FF_SPLIT


def _ffn_kernel(x_ref, wg_ref, wu_ref, wd_ref, ln_ref, o_ref):
    x = x_ref[...]
    xb = x.astype(BF16)
    acc = ALPHA * x
    for c in range(FF_SPLIT):
        cs = slice(c * FF_PART, (c + 1) * FF_PART)
        h = (_silu(jnp.dot(xb, wg_ref[:, cs], preferred_element_type=F32))
             * jnp.dot(xb, wu_ref[:, cs], preferred_element_type=F32))
        acc = acc + jnp.dot(h.astype(BF16), wd_ref[cs, :], preferred_element_type=F32)
    o_ref[...] = _layer_norm_rows(acc, ln_ref[0:1, :], ln_ref[1:2, :])


def _ffn(x2, wg, wu, wd, ln, tm):
    n = x2.shape[0]
    row = pl.BlockSpec((tm, D_MODEL), lambda i: (i, 0))
    return pl.pallas_call(
        _ffn_kernel,
        grid=(n // tm,),
        in_specs=[row, _const_spec(wg.shape), _const_spec(wu.shape), _const_spec(wd.shape),
                  _const_spec(ln.shape)],
        out_specs=row,
        out_shape=jax.ShapeDtypeStruct((n, D_MODEL), F32),
        compiler_params=_params(("arbitrary",), 56),
        name="ffn",
    )(x2, wg, wu, wd, ln)


def _router_kernel(x_ref, wr_ref, br_ref, tril_ref, rank_ref, wsel_ref, cnt_ref):
    logits = _dot3(x_ref[...], wr_ref[...]) + br_ref[...]
    lane = lax.broadcasted_iota(jnp.int32, logits.shape, 1)
    neg = jnp.float32(-jnp.inf)
    logits = jnp.where(lane < N_EXPERTS, logits, neg)
    m1 = jnp.max(logits, axis=1, keepdims=True)
    lane_f = lane.astype(F32)
    i1 = jnp.min(jnp.where(logits == m1, lane_f, float(LANE)), axis=1, keepdims=True)
    first = lane_f == i1
    rest = jnp.where(first, neg, logits)
    m2 = jnp.max(rest, axis=1, keepdims=True)
    i2 = jnp.min(jnp.where(rest == m2, lane_f, float(LANE)), axis=1, keepdims=True)
    second = lane_f == i2
    e = jnp.exp(m2 - m1)
    w1 = 1.0 / (1.0 + e)
    w2 = e / (1.0 + e)
    sel = jnp.where(first, 1.0, jnp.where(second, 1.0, 0.0))
    wsel_ref[...] = jnp.where(first, w1, jnp.where(second, w2, 0.0))
    sel_bf = sel.astype(BF16)
    rank = jnp.dot(tril_ref[...], sel_bf, preferred_element_type=F32)
    rank_ref[...] = jnp.where(sel > 0.5, rank, -1.0)
    ones = jnp.ones((SUBLANE, sel.shape[0]), BF16)
    cnt_ref[0] = jnp.dot(ones, sel_bf, preferred_element_type=F32).astype(jnp.int32)


def _router(x2, wr, br, tril, tm):
    n = x2.shape[0]
    nt = n // tm
    col = pl.BlockSpec((tm, LANE), lambda i: (i, 0))
    return pl.pallas_call(
        _router_kernel,
        grid=(nt,),
        in_specs=[pl.BlockSpec((tm, D_MODEL), lambda i: (i, 0)), _const_spec(wr.shape),
                  _const_spec(br.shape), _const_spec(tril.shape)],
        out_specs=[col, col, pl.BlockSpec((1, SUBLANE, LANE), lambda i: (i, 0, 0))],
        out_shape=[jax.ShapeDtypeStruct((n, LANE), F32), jax.ShapeDtypeStruct((n, LANE), F32),
                   jax.ShapeDtypeStruct((nt, SUBLANE, LANE), jnp.int32)],
        compiler_params=_params(("arbitrary",), 40),
        name="router",
    )(x2, wr, br, tril)


def _moe_kernel(rows, cnt_ref, x_ref, xb_ref, rrow_ref, rank_ref, wsel_ref, wg_ref, wu_ref, wd_ref,
                ln_ref, o_ref, xg_sc, yb_sc):
    i = pl.program_id(0)
    e = pl.program_id(1)
    c = pl.program_id(2)
    n_e = pl.num_programs(1)
    n_c = pl.num_programs(2)
    cnt = cnt_ref[i * N_EXPERTS + e]
    n_blk = (cnt + rows - 1) // rows
    tm = xb_ref.shape[0]

    @pl.when((e == 0) & (c == 0))
    def _():
        o_ref[...] = ALPHA * x_ref[...]

    def gather(blk, carry):
        r0 = pl.multiple_of(blk * rows, SUBLANE)
        slot = (lax.broadcasted_iota(jnp.int32, (rows, tm), 0) + r0).astype(F32)
        onehot = jnp.where(rrow_ref[0] == slot, 1.0, 0.0).astype(BF16)
        xg_sc[pl.ds(r0, rows), :] = jnp.dot(onehot, xb_ref[...], preferred_element_type=F32).astype(BF16)
        return carry

    @pl.when(c == 0)
    def _():
        lax.fori_loop(0, n_blk, gather, 0)

    def expert(blk, carry):
        r0 = pl.multiple_of(blk * rows, SUBLANE)
        xg = xg_sc[pl.ds(r0, rows), :]
        h = (_silu(jnp.dot(xg, wg_ref[0], preferred_element_type=F32))
             * jnp.dot(xg, wu_ref[0], preferred_element_type=F32))
        yb = jnp.dot(h.astype(BF16), wd_ref[0], preferred_element_type=F32)

        @pl.when(c == 0)
        def _():
            yb_sc[pl.ds(r0, rows), :] = yb

        @pl.when(c > 0)
        def _():
            yb_sc[pl.ds(r0, rows), :] = yb_sc[pl.ds(r0, rows), :] + yb
        return carry

    lax.fori_loop(0, n_blk, expert, 0)

    @pl.when(c == n_c - 1)
    def _():
        mine = lax.broadcasted_iota(jnp.int32, (tm, LANE), 1) == e
        rank_col = jnp.sum(jnp.where(mine, rank_ref[...], 0.0), axis=1, keepdims=True)
        w_col = jnp.sum(jnp.where(mine, wsel_ref[...], 0.0), axis=1, keepdims=True)

        def scatter(blk, carry):
            r0 = pl.multiple_of(blk * rows, SUBLANE)
            slot = (lax.broadcasted_iota(jnp.int32, (tm, rows), 1) + r0).astype(F32)
            onehot = jnp.where(rank_col == slot, 1.0, 0.0).astype(BF16)
            back = jnp.dot(onehot, yb_sc[pl.ds(r0, rows), :].astype(BF16), preferred_element_type=F32)
            o_ref[...] = o_ref[...] + w_col * back
            return carry

        lax.fori_loop(0, n_blk, scatter, 0)

    @pl.when((e == n_e - 1) & (c == n_c - 1))
    def _():
        o_ref[...] = _layer_norm_rows(o_ref[...], ln_ref[0:1, :], ln_ref[1:2, :])


def _moe(x2, xb2, counts, rank_row, rank, wsel, wg, wu, wd, ln, tm, rows):
    n = x2.shape[0]
    nt = n // tm
    cap = -(-tm // rows) * rows
    grid_spec = pltpu.PrefetchScalarGridSpec(
        num_scalar_prefetch=1,
        grid=(nt, N_EXPERTS, FF_SPLIT),
        in_specs=[
            pl.BlockSpec((tm, D_MODEL), lambda i, e, c, cnt: (i, 0)),
            pl.BlockSpec((tm, D_MODEL), lambda i, e, c, cnt: (i, 0)),
            pl.BlockSpec((1, 1, tm), lambda i, e, c, cnt: (e, 0, i)),
            pl.BlockSpec((tm, LANE), lambda i, e, c, cnt: (i, 0)),
            pl.BlockSpec((tm, LANE), lambda i, e, c, cnt: (i, 0)),
            pl.BlockSpec((1, D_MODEL, FF_PART), lambda i, e, c, cnt: (e, 0, c)),
            pl.BlockSpec((1, D_MODEL, FF_PART), lambda i, e, c, cnt: (e, 0, c)),
            pl.BlockSpec((1, FF_PART, D_MODEL), lambda i, e, c, cnt: (e, c, 0)),
            pl.BlockSpec((2, D_MODEL), lambda i, e, c, cnt: (0, 0)),
        ],
        out_specs=pl.BlockSpec((tm, D_MODEL), lambda i, e, c, cnt: (i, 0)),
        scratch_shapes=[pltpu.VMEM((cap, D_MODEL), BF16), pltpu.VMEM((cap, D_MODEL), F32)],
    )
    return pl.pallas_call(
        functools.partial(_moe_kernel, rows),
        grid_spec=grid_spec,
        out_shape=jax.ShapeDtypeStruct((n, D_MODEL), F32),
        compiler_params=_params(("arbitrary", "arbitrary", "arbitrary"), 56),
        name="moe",
    )(counts, x2, xb2, rank_row, rank, wsel, wg, wu, wd, ln)


def _tile_sizes(b, t):
    n = b * t
    tm = min(512, n)
    tm_proj = min(256, t)
    tt = min(256, t)
    tm_moe = min(1024, n)
    rows = 288 if tm_moe == 1024 else 160
    return tm, tm_proj, tt, tm_moe, rows


def _to_block_diag(s):
    b, h, r, c = s.shape
    eye = jnp.eye(h, dtype=s.dtype)
    return jnp.einsum("bhrc,hg->bhrgc", s, eye).reshape(b, h * r, h * c)


def _from_block_diag(s, r, c):
    b = s.shape[0]
    s5 = s.reshape(b, N_HEADS, r, N_HEADS, c)
    return jnp.stack([s5[:, h, :, h, :] for h in range(N_HEADS)], axis=1)


def _pad_cols(a, width):
    return jnp.pad(a, ((0, 0), (0, width - a.shape[1])))


def _prep_layer(l, p):
    d = D_MODEL
    bw = BR_WIDTH
    w = {}
    w["win"] = p["w_in"][l].astype(BF16)
    mu = p["rwkv_mu_x"][l]
    if l >= 1:
        v1, v2, v0, mu_v = p["rwkv_v1"][l - 1], p["rwkv_v2"][l - 1], p["rwkv_v0"][l - 1], p["rwkv_mu_v"][l - 1]
    else:
        v1, v2 = jnp.zeros((d, RWKV_V_LORA), F32), jnp.zeros((RWKV_V_LORA, bw), F32)
        v0, mu_v = jnp.zeros((bw,), F32), jnp.zeros((d,), F32)
    first = [p["rwkv_w1"][l], p["rwkv_a1"][l], p["rwkv_g1"][l], v1, p["gla_w1"][l]]
    shift_mu = [mu[0], mu[1], mu[2], mu_v, jnp.zeros((d,), F32)]
    w["wl1"] = _pad_cols(jnp.concatenate(first, axis=1), LORA_COLS).astype(BF16)
    w["wl1mu"] = _pad_cols(jnp.concatenate([m[:, None] * a for m, a in zip(shift_mu, first)], axis=1),
                           LORA_COLS).astype(BF16)
    second = jax.scipy.linalg.block_diag(p["rwkv_w2"][l], p["rwkv_a2"][l], p["rwkv_g2"][l], v2, p["gla_w2"][l])
    w["w2"] = jnp.pad(second, ((0, LORA_COLS - second.shape[0]), (0, 0))).astype(BF16)
    w["bias"] = jnp.concatenate([p["rwkv_w0"][l], p["rwkv_a0"][l], jnp.zeros((bw,), F32), v0,
                                 p["gla_b"][l]])[None]
    rows = [p["rwkv_mu_rkv"][l].reshape(RWKV_COLS)]
    rows += [jnp.pad(p[name][l], (0, RWKV_COLS - bw))
             for name in ("rwkv_k_k", "rwkv_k_a", "rwkv_r_k", "rwkv_ln_g", "rwkv_ln_b")]
    rows += [jnp.zeros((RWKV_COLS,), F32)] * (SUBLANE - len(rows))
    w["rwkv_prm"] = jnp.stack(rows)
    w["ret_prm"] = jnp.stack([p["ret_gn_g"][l], p["ret_gn_b"][l]])
    w["hgrn_ng"] = p["hgrn_norm_g"][l][None]
    w["gla_ng"] = p["gla_norm_g"][l][None]
    w["wg"] = p["w_gate"][l].astype(BF16)
    w["bg"] = p["b_gate"][l]
    w["wbr"] = p["w_br"][l].astype(BF16)
    w["wo"] = p["w_o"][l].astype(BF16)
    w["ln1"] = jnp.stack([p["ln1_g"][l], p["ln1_b"][l]])
    w["ln2"] = jnp.stack([p["ln2_g"][l], p["ln2_b"][l]])
    j = l // 2
    if l % 2 == 0:
        w["ffn"] = (p["ffn_w_gate"][j].astype(BF16), p["ffn_w_up"][j].astype(BF16),
                    p["ffn_w_down"][j].astype(BF16))
    else:
        wr = _pad_cols(p["router_w"][j], LANE)
        br = _pad_cols(p["router_b"][j][None], LANE)
        w["moe"] = (wr, br, p["moe_w_gate"][j].astype(BF16), p["moe_w_up"][j].astype(BF16),
                    p["moe_w_down"][j].astype(BF16))
    return w


def _mixer_consts():
    bdm = _np_block_mask(HEAD_DIM, HEAD_DIM)
    lg = np.log1p(-np.exp2(-5.0 - np.arange(N_HEADS, dtype=np.float64)))
    lg_l = np.repeat(lg, HEAD_DIM)[None, :]
    t = np.arange(CHUNK, dtype=np.float64)[:, None]
    s_side = (np.arange(N_HEADS * CHUNK) % CHUNK)[None, :].astype(np.float64)
    lg_side = np.repeat(lg, CHUNK)[None, :]
    d_mat = np.where(s_side <= t, np.exp((t - s_side) * lg_side), 0.0)
    dec = np.zeros((3 * CHUNK + SUBLANE, BR_WIDTH), np.float64)
    dec[0:CHUNK] = np.exp((t + 1.0) * lg_l)
    dec[CHUNK:2 * CHUNK] = np.exp((CHUNK - 1.0 - t) * lg_l)
    dec[2 * CHUNK:3 * CHUNK] = d_mat
    dec[3 * CHUNK] = np.exp(CHUNK * lg_l[0])
    return {
        "tril": jnp.asarray(np.tril(np.ones((CHUNK, CHUNK), np.float32)), BF16),
        "bdm": jnp.asarray(bdm, F32),
        "head_lanes": jnp.asarray(_np_head_lanes(HEAD_DIM), F32),
        "strict": jnp.asarray(_np_causal_side_by_side(True), F32),
        "incl": jnp.asarray(_np_causal_side_by_side(False), F32),
        "eye": jnp.asarray(np.tile(np.eye(CHUNK, dtype=np.float32), (1, N_HEADS)), F32),
        "ret_dec": jnp.asarray(dec, F32),
        "gla_pair": jnp.asarray(_np_block_mask(GLA_DK, HEAD_DIM), BF16),
        "gla_mask": jnp.asarray(_np_block_mask(HEAD_DIM, GLA_DK), F32),
        "gla_head_lanes": jnp.asarray(_np_head_lanes(GLA_DK), F32),
    }


def _rope_tables(pos0, t):
    half = HEAD_DIM // 2
    pos = pos0 + jnp.arange(t, dtype=F32)
    inv = ROPE_THETA ** (-jnp.arange(half, dtype=F32) / half)
    ang = pos[:, None] * inv[None]
    cos = jnp.cos(ang)
    sin = jnp.sin(ang)
    cos_t = jnp.tile(jnp.concatenate([cos, cos], axis=1), (1, N_HEADS))
    sin_t = jnp.tile(jnp.concatenate([-sin, sin], axis=1), (1, N_HEADS))
    return cos_t, sin_t


def _previous_rows(x, x_last, tm):
    b, t, d = x.shape
    per_seq = t // tm
    tails = x.reshape(b, per_seq, tm, d)[:, :, tm - 1, :]
    prev = jnp.concatenate([x_last[:, None, :], tails[:, :per_seq - 1, :]], axis=1)
    return prev.reshape(b * per_seq, 1, d)


def _run_trunk(x, pos0, s_rwkv, c_shift, s_ret, s_hgrn, s_gla, prm, layers, consts):
    b, t, d = x.shape
    n = b * t
    tm, tm_proj, tt, tm_moe, rows = _tile_sizes(b, t)
    cos_t, sin_t = _rope_tables(pos0, t)
    v_first = None
    new_rwkv, new_shift, new_ret, new_hgrn, new_gla = [], [], [], [], []
    for l in range(DEPTH):
        w = layers[l]
        x_in = x
        x_last = c_shift[l]
        p2, aux2 = _in_proj(x.reshape(n, d), _previous_rows(x, x_last, tm_proj), w["win"], w["wl1"],
                            w["wl1mu"], w["w2"], w["bias"], tm_proj)
        p3 = p2.reshape(b, t, IN_COLS)
        aux3 = aux2.reshape(b, t, AUX_COLS)
        pad = (-b) % SUBLANE
        x_last_p = jnp.concatenate([x_last, jnp.zeros((pad, d), F32)], axis=0) if pad else x_last
        rkv_last = _rows_matmul(x_last_p, w["win"][:, :RWKV_COLS])[:b, None, :]

        res = _rwkv_mixer(p3, aux3, v_first, rkv_last, _to_block_diag(s_rwkv[l]), w["rwkv_prm"], consts, tt)
        if v_first is None:
            o_rwkv, v_first, st_rwkv = res
        else:
            o_rwkv, st_rwkv = res
        o_ret, st_ret = _ret_mixer(p3, cos_t, sin_t, _to_block_diag(jnp.swapaxes(s_ret[l], -1, -2)),
                                   w["ret_prm"], consts, tt)
        o_hgrn, st_hgrn = _hgrn_mixer(p3, _to_block_diag(jnp.swapaxes(s_hgrn[l], -1, -2)),
                                      prm["hgrn_lb_logits"], w["hgrn_ng"], consts, l, tt)
        o_gla, st_gla = _gla_mixer(p3, aux3, _to_block_diag(jnp.swapaxes(s_gla[l], -1, -2)), w["gla_ng"],
                                   consts, tt)

        outs = [o.reshape(n, BR_WIDTH) for o in (o_rwkv, o_ret, o_hgrn, o_gla)]
        is_moe = l % 2 == 1
        merged = _merge(x.reshape(n, d), outs, w["wg"], w["bg"], w["wbr"], w["wo"], w["ln1"], tm, is_moe)
        if not is_moe:
            x1 = merged[0]
            x2 = _ffn(x1, *w["ffn"], w["ln2"], tm)
        else:
            x1, x1b = merged
            wr, br, mg, mu_, md = w["moe"]
            tril_m = jnp.asarray(np.tril(np.ones((tm_moe, tm_moe), np.float32), -1), BF16)
            rank, wsel, cnt = _router(x1, wr, br, tril_m, tm_moe)
            rank_row = rank[:, :N_EXPERTS].T.reshape(N_EXPERTS, 1, n)
            counts = cnt[:, 0, :N_EXPERTS].reshape(-1)
            x2 = _moe(x1, x1b, counts, rank_row, rank, wsel, mg, mu_, md, w["ln2"], tm_moe, rows)
        x = x2.reshape(b, t, d)

        new_rwkv.append(_from_block_diag(st_rwkv, HEAD_DIM, HEAD_DIM))
        new_shift.append(x_in[:, -1])
        new_ret.append(jnp.swapaxes(_from_block_diag(st_ret, HEAD_DIM, HEAD_DIM), -1, -2))
        new_hgrn.append(jnp.swapaxes(_from_block_diag(st_hgrn, HEAD_DIM, HEAD_DIM), -1, -2))
        new_gla.append(jnp.swapaxes(_from_block_diag(st_gla, HEAD_DIM, GLA_DK), -1, -2))
    return (x, jnp.stack(new_rwkv), jnp.stack(new_shift), jnp.stack(new_ret), jnp.stack(new_hgrn),
            jnp.stack(new_gla))


def kernel(x_prompt, x_sample, state_rwkv, cache_shift, state_ret, state_hgrn, state_gla, w_in, rwkv_mu_rkv, rwkv_mu_x, rwkv_mu_v, rwkv_w0, rwkv_w1, rwkv_w2, rwkv_a0, rwkv_a1, rwkv_a2, rwkv_v0, rwkv_v1, rwkv_v2, rwkv_g1, rwkv_g2, rwkv_k_k, rwkv_k_a, rwkv_r_k, rwkv_ln_g, rwkv_ln_b, ret_gn_g, ret_gn_b, hgrn_lb_logits, hgrn_norm_g, gla_w1, gla_w2, gla_b, gla_norm_g, w_br, w_gate, b_gate, w_o, ln1_g, ln1_b, ln2_g, ln2_b, ffn_w_gate, ffn_w_up, ffn_w_down, router_w, router_b, moe_w_gate, moe_w_up, moe_w_down):
    prm = {
        'w_in': w_in, 'rwkv_mu_rkv': rwkv_mu_rkv, 'rwkv_mu_x': rwkv_mu_x, 'rwkv_mu_v': rwkv_mu_v,
        'rwkv_w0': rwkv_w0, 'rwkv_w1': rwkv_w1, 'rwkv_w2': rwkv_w2,
        'rwkv_a0': rwkv_a0, 'rwkv_a1': rwkv_a1, 'rwkv_a2': rwkv_a2,
        'rwkv_v0': rwkv_v0, 'rwkv_v1': rwkv_v1, 'rwkv_v2': rwkv_v2,
        'rwkv_g1': rwkv_g1, 'rwkv_g2': rwkv_g2, 'rwkv_k_k': rwkv_k_k, 'rwkv_k_a': rwkv_k_a,
        'rwkv_r_k': rwkv_r_k, 'rwkv_ln_g': rwkv_ln_g, 'rwkv_ln_b': rwkv_ln_b,
        'ret_gn_g': ret_gn_g, 'ret_gn_b': ret_gn_b, 'hgrn_lb_logits': hgrn_lb_logits,
        'hgrn_norm_g': hgrn_norm_g, 'gla_w1': gla_w1, 'gla_w2': gla_w2, 'gla_b': gla_b,
        'gla_norm_g': gla_norm_g, 'w_br': w_br, 'w_gate': w_gate, 'b_gate': b_gate, 'w_o': w_o,
        'ln1_g': ln1_g, 'ln1_b': ln1_b, 'ln2_g': ln2_g, 'ln2_b': ln2_b,
        'ffn_w_gate': ffn_w_gate, 'ffn_w_up': ffn_w_up, 'ffn_w_down': ffn_w_down,
        'router_w': router_w, 'router_b': router_b,
        'moe_w_gate': moe_w_gate, 'moe_w_up': moe_w_up, 'moe_w_down': moe_w_down,
    }
    layers = [_prep_layer(l, prm) for l in range(DEPTH)]
    consts = _mixer_consts()
    bp = x_prompt.shape[0]
    zero_hd = jnp.zeros((DEPTH, bp, N_HEADS, HEAD_DIM, HEAD_DIM), F32)
    zero_shift = jnp.zeros((DEPTH, bp, D_MODEL), F32)
    zero_gla = jnp.zeros((DEPTH, bp, N_HEADS, GLA_DK, HEAD_DIM), F32)
    prompt = _run_trunk(x_prompt, 0.0, zero_hd, zero_shift, zero_hd, zero_hd, zero_gla, prm, layers, consts)
    sample = _run_trunk(x_sample, float(PAST_LEN), state_rwkv, cache_shift, state_ret, state_hgrn,
                        state_gla, prm, layers, consts)
    y_p, p_rwkv, p_shift, p_ret, p_hgrn, p_gla = prompt
    y_s, s_rwkv, s_shift, s_ret, s_hgrn, s_gla = sample
    return (y_p, y_s, p_rwkv, p_shift, p_ret, p_hgrn, p_gla, s_rwkv, s_shift, s_ret, s_hgrn, s_gla)
```

```python
import functools
import math

import numpy as np
import jax
import jax.numpy as jnp
import jax.scipy.linalg
from jax import lax
from jax.experimental import pallas as pl
from jax.experimental.pallas import tpu as pltpu

F32 = jnp.float32
BF16 = jnp.bfloat16

D_MODEL = 1024
DEPTH = 2
PAST_LEN = 4096
CHUNK = 64
SUB = 16
N_BRANCH = 4
BR_WIDTH = D_MODEL // N_BRANCH
HEAD_DIM = 64
N_HEADS = BR_WIDTH // HEAD_DIM
GLA_DK = HEAD_DIM // 2
GLA_KW = N_HEADS * GLA_DK
GLA_GATE_RANK = 16
GLA_TAU = 16.0
RWKV_W_LORA = 32
RWKV_A_LORA = 32
RWKV_V_LORA = 16
RWKV_G_LORA = 64
RWKV_GN_EPS = 64e-5
ROPE_THETA = 10000.0
LN_EPS = 1e-5
D_FF = 2816
N_EXPERTS = 8
ALPHA = (2.0 * DEPTH) ** 0.25
RWKV_COLS = 3 * BR_WIDTH
IN_COLS = 3584
LORA_COLS = 256
AUX_COLS = 4 * BR_WIDTH + GLA_KW
SEQ_PAIR = 2

LANE = 128
SUBLANE = 8
LOG2_E = 1.4426950408889634

NN = (((1,), (0,)), ((), ()))
NT = (((1,), (1,)), ((), ()))
TN = (((0,), (0,)), ((), ()))


def _params(sem, vmem_mib):
    return pltpu.CompilerParams(dimension_semantics=sem, vmem_limit_bytes=vmem_mib * 1024 * 1024)


def _const_spec(shape):
    nd = len(shape)
    return pl.BlockSpec(shape, lambda *_: (0,) * nd, pipeline_mode=pl.Buffered(1))


def _dot(a, b, dims=NN):
    return lax.dot_general(a.astype(BF16), b.astype(BF16), dims, preferred_element_type=F32)


def _split(x):
    hi = x.astype(BF16)
    lo = (x - hi.astype(F32)).astype(BF16)
    return hi, lo


def _dot_exact_lhs(a_bf, x):
    hi, lo = _split(x)
    return (jnp.dot(a_bf, hi, preferred_element_type=F32)
            + jnp.dot(a_bf, lo, preferred_element_type=F32))


def _dot_exact_rhs(x, b_bf):
    hi, lo = _split(x)
    return (jnp.dot(hi, b_bf, preferred_element_type=F32)
            + jnp.dot(lo, b_bf, preferred_element_type=F32))


def _dot3(a, b, dims=NN):
    ah, al = _split(a)
    bh, bl = _split(b)
    d = functools.partial(lax.dot_general, dimension_numbers=dims, preferred_element_type=F32)
    return d(ah, bh) + (d(ah, bl) + d(al, bh))


def _sigmoid(x):
    return 1.0 / (1.0 + jnp.exp(-x))


def _softplus(x):
    return jnp.maximum(x, 0.0) + jnp.log(1.0 + jnp.exp(-jnp.abs(x)))


def _silu(x):
    return x * _sigmoid(x)


def _layer_norm_rows(y, g, b):
    mu = jnp.mean(y, axis=-1, keepdims=True)
    yc = y - mu
    var = jnp.mean(yc * yc, axis=-1, keepdims=True)
    return yc * lax.rsqrt(var + LN_EPS) * g + b


def _np_block_mask(rows_per_head, cols_per_head):
    r = np.arange(N_HEADS * rows_per_head)[:, None] // rows_per_head
    c = np.arange(N_HEADS * cols_per_head)[None, :] // cols_per_head
    return (r == c).astype(np.float32)


def _np_head_lanes(cols_per_head):
    m = np.zeros((SUBLANE, N_HEADS * cols_per_head), np.float32)
    for h in range(N_HEADS):
        m[h, h * cols_per_head:(h + 1) * cols_per_head] = 1.0
    return m


def _np_causal_side_by_side(strict):
    t = np.arange(CHUNK)[:, None]
    s = np.arange(N_HEADS * CHUNK)[None, :] % CHUNK
    return ((s < t) if strict else (s <= t)).astype(np.float32)


def _in_proj_kernel(x_ref, prev_ref, win_ref, wl1_ref, wl1mu_ref, w2_ref, bias_ref, p_ref, aux_ref):
    x = x_ref[...]
    row = lax.broadcasted_iota(jnp.int32, x.shape, 0)
    xx = jnp.where(row == 0, prev_ref[0], pltpu.roll(x, 1, axis=0)) - x
    xb = x.astype(BF16)
    p_ref[...] = jnp.dot(xb, win_ref[...], preferred_element_type=F32)
    h = (jnp.dot(xb, wl1_ref[...], preferred_element_type=F32)
         + jnp.dot(xx.astype(BF16), wl1mu_ref[...], preferred_element_type=F32))
    lane = lax.broadcasted_iota(jnp.int32, h.shape, 1)
    act = jnp.where(lane < RWKV_W_LORA, jnp.tanh(h), h)
    g_lo = RWKV_W_LORA + RWKV_A_LORA
    in_g = jnp.where(lane >= g_lo, jnp.where(lane < g_lo + RWKV_G_LORA, 1.0, 0.0), 0.0)
    act = jnp.where(in_g > 0.5, _sigmoid(h), act)
    z = jnp.dot(act.astype(BF16), w2_ref[...], preferred_element_type=F32) + bias_ref[...]
    bw = BR_WIDTH
    w_log = -_softplus(-z[:, 0:bw]) - 0.5
    aux_ref[:, 0:bw] = -jnp.exp(w_log)
    aux_ref[:, bw:2 * bw] = _sigmoid(z[:, bw:2 * bw])
    aux_ref[:, 2 * bw:3 * bw] = z[:, 2 * bw:3 * bw]
    aux_ref[:, 3 * bw:4 * bw] = _sigmoid(z[:, 3 * bw:4 * bw])
    zg = z[:, 4 * bw:]
    aux_ref[:, 4 * bw:] = (jnp.minimum(zg, 0.0) - jnp.log(1.0 + jnp.exp(-jnp.abs(zg)))) * (1.0 / GLA_TAU)


def _in_proj(x2, prev_rows, win, wl1, wl1mu, w2, bias, tm):
    n = x2.shape[0]
    return pl.pallas_call(
        _in_proj_kernel,
        grid=(n // tm,),
        in_specs=[
            pl.BlockSpec((tm, D_MODEL), lambda i: (i, 0)),
            pl.BlockSpec((1, 1, D_MODEL), lambda i: (i, 0, 0)),
            _const_spec(win.shape), _const_spec(wl1.shape), _const_spec(wl1mu.shape),
            _const_spec(w2.shape), _const_spec(bias.shape),
        ],
        out_specs=[pl.BlockSpec((tm, IN_COLS), lambda i: (i, 0)),
                   pl.BlockSpec((tm, AUX_COLS), lambda i: (i, 0))],
        out_shape=[jax.ShapeDtypeStruct((n, IN_COLS), F32), jax.ShapeDtypeStruct((n, AUX_COLS), F32)],
        compiler_params=_params(("arbitrary",), 56),
        name="in_proj",
    )(x2, prev_rows, win, wl1, wl1mu, w2, bias)


def _rows_matmul_kernel(x_ref, w_ref, o_ref):
    o_ref[...] = jnp.dot(x_ref[...].astype(BF16), w_ref[...], preferred_element_type=F32)


def _rows_matmul(x, w):
    return pl.pallas_call(
        _rows_matmul_kernel,
        out_shape=jax.ShapeDtypeStruct((x.shape[0], w.shape[1]), F32),
        name="rows_matmul",
    )(x, w)


def _stack_heads(x, head_lanes):
    xb = x.astype(BF16)
    return jnp.concatenate([xb * head_lanes[h:h + 1, :].astype(BF16) for h in range(N_HEADS)], axis=0)


def _head_sum(x, ones_bf):
    return _dot_exact_rhs(x, ones_bf)


def _head_layer_norm(y, ones_bf, g, b, eps):
    inv = 1.0 / HEAD_DIM
    mu = _head_sum(y, ones_bf) * inv
    yc = y - mu
    var = _head_sum(yc * yc, ones_bf) * inv
    return yc * lax.rsqrt(var + eps) * g + b


def _head_rms_norm(y, ones_bf, g):
    ms = _head_sum(y * y, ones_bf) * (1.0 / HEAD_DIM)
    return y * lax.rsqrt(ms + 1e-6) * g


def _pairwise_block(q, k, v, b2, pair_ones_bf):
    parts = []
    for j in range(SUB):
        lo = (j // SUBLANE) * SUBLANE
        p = q[lo:] * jnp.exp2(b2[lo:] - b2[j:j + 1]) * k[j:j + 1]
        if j % SUBLANE:
            rid = lax.broadcasted_iota(jnp.int32, p.shape, 0) + lo
            p = jnp.where(rid >= j, p, 0.0)
        parts.append(p)
    att = jnp.dot(jnp.concatenate(parts, axis=0).astype(BF16), pair_ones_bf, preferred_element_type=F32)
    outs = []
    off = 0
    for g in range(SUB // SUBLANE):
        rows = SUB - g * SUBLANE
        acc = None
        for j in range(g * SUBLANE, (g + 1) * SUBLANE):
            term = att[off:off + rows] * v[j:j + 1]
            acc = term if acc is None else acc + term
            off += rows
        if g:
            acc = jnp.concatenate([jnp.zeros((g * SUBLANE, v.shape[1]), F32), acc], axis=0)
        outs.append(acc)
    total = outs[0]
    for extra in outs[1:]:
        total = total + extra
    return total


def _gla_state_free(tiles, tril_bf, pair_ones_bf, st_mask, lanes_k, lanes_v):
    n = range(len(tiles))
    q, k, v, glog = ([t[i] for t in tiles] for i in range(4))
    b = [_dot_exact_lhs(tril_bf, glog[i]) for i in n]
    b2 = [b[i] * LOG2_E for i in n]
    blocks = [[] for _ in n]
    for blk in range(CHUNK // SUB):
        r0 = blk * SUB
        sl = slice(r0, r0 + SUB)
        o_blk = [_pairwise_block(q[i][sl], k[i][sl], v[i][sl], b2[i][sl], pair_ones_bf) for i in n]
        if blk:
            c0 = [b[i][r0 - 1:r0] for i in n]
            q_t = [q[i][sl] * jnp.exp(b[i][sl] - c0[i]) for i in n]
            k_t = [k[i][:r0] * jnp.exp(c0[i] - b[i][:r0]) for i in n]
            att = [_dot(q_t[i], _stack_heads(k_t[i], lanes_k), NT) for i in n]
            o_blk = [o_blk[i] + _dot(att[i], _stack_heads(v[i][:r0], lanes_v)) for i in n]
        for i in n:
            blocks[i].append(o_blk[i])
    b_last = [b[i][CHUNK - 1:CHUNK, :] for i in n]
    upd = [st_mask * _dot(v[i], k[i] * jnp.exp(b_last[i] - b[i]), TN) for i in n]
    return [(jnp.concatenate(blocks[i], axis=0), q[i] * jnp.exp(b[i]), upd[i], jnp.exp(b_last[i]))
            for i in n]


def _rwkv_state_free(tiles, tril_bf, hl, strict, incl, eye):
    n = range(len(tiles))
    r, k, v, kk, bv, lw = ([t[i] for t in tiles] for i in range(6))
    l = [_dot_exact_lhs(tril_bf, lw[i]) for i in n]
    l_last = [l[i][CHUNK - 1:CHUNK, :] for i in n]
    e_neg = [jnp.exp(-l[i]) for i in n]
    lhs = [jnp.concatenate([kk[i] * jnp.exp(l[i] - lw[i]), r[i] * jnp.exp(l[i])], axis=0) for i in n]
    rhs = [jnp.concatenate([_stack_heads(k[i] * e_neg[i], hl), _stack_heads(bv[i] * e_neg[i], hl)], axis=0)
           for i in n]
    amat = [_dot(lhs[i], rhs[i], NT) for i in n]
    w = N_HEADS * CHUNK
    a_ab = [amat[i][:CHUNK, w:] * strict for i in n]
    a_vk = [jnp.concatenate([amat[i][:CHUNK, :w] * strict, amat[i][CHUNK:, :w] * incl], axis=0) for i in n]
    a_rb = [amat[i][CHUNK:, w:] * incl for i in n]
    x = [eye + a_ab[i] for i in n]
    m = a_ab
    for _ in range(int(math.log2(CHUNK)) - 1):
        m_st = [_stack_heads(m[i], hl) for i in n]
        m = [_dot(m[i], m_st[i]) for i in n]
        m_st = [_stack_heads(m[i], hl) for i in n]
        x = [x[i] + _dot(x[i], m_st[i]) for i in n]
    from_v = [_dot(a_vk[i], _stack_heads(v[i], hl)) for i in n]
    e_end = [jnp.exp(l_last[i] - l[i]) for i in n]
    upd_v = [_dot(v[i], k[i] * e_end[i], TN) for i in n]
    b_end = [bv[i] * e_end[i] for i in n]
    st_dec = [jnp.exp(l_last[i]) for i in n]
    return [(lhs[i], x[i], a_rb[i], from_v[i], upd_v[i], b_end[i], st_dec[i]) for i in n]


def _rwkv_state_step(parts, sts, bdm, hl):
    n = range(len(parts))
    lhs, x, a_rb, from_v, upd_v, b_end, st_dec = ([p[i] for p in parts] for i in range(7))
    from_state = [_dot(lhs[i], sts[i], NT) for i in n]
    u = [_dot(x[i], _stack_heads(from_state[i][:CHUNK] + from_v[i][:CHUNK], hl)) for i in n]
    upd = [upd_v[i] + _dot(u[i], b_end[i], TN) for i in n]
    new = [sts[i] * st_dec[i] + upd[i] * bdm for i in n]
    y = [from_state[i][CHUNK:] + from_v[i][CHUNK:] + _dot(a_rb[i], _stack_heads(u[i], hl)) for i in n]
    return y, new


def _rwkv_kernel(has_vres, *refs):
    if has_vres:
        (rkv_ref, lw_ref, a_ref, g_ref, vg_ref, vf_ref, last_ref, st0_ref, prm_ref, tril_ref, bdm_ref,
         hl_ref, strict_ref, incl_ref, eye_ref, o_ref, st_out_ref, st_sc, prev_sc) = refs
    else:
        (rkv_ref, lw_ref, a_ref, g_ref, last_ref, st0_ref, prm_ref, tril_ref, bdm_ref,
         hl_ref, strict_ref, incl_ref, eye_ref, o_ref, v_out_ref, st_out_ref, st_sc, prev_sc) = refs
    tb = pl.program_id(1)

    @pl.when(tb == 0)
    def _():
        st_sc[...] = st0_ref[...]
        prev_sc[...] = last_ref[...]

    bw = BR_WIDTH
    k_k = prm_ref[1:2, 0:bw]
    k_a = prm_ref[2:3, 0:bw]
    r_k = prm_ref[3:4, 0:bw]
    ln_g = prm_ref[4:5, 0:bw]
    ln_b = prm_ref[5:6, 0:bw]
    bdm = bdm_ref[...]
    ones_bf = bdm.astype(BF16)
    hl = hl_ref[...]
    tril_bf = tril_ref[...]
    strict = strict_ref[...]
    incl = incl_ref[...]
    eye = eye_ref[...]
    n_seq = rkv_ref.shape[0]
    tt = rkv_ref.shape[1]
    seqs = []
    for s in range(n_seq):
        rkv = rkv_ref[s]
        row = lax.broadcasted_iota(jnp.int32, rkv.shape, 0)
        prev = jnp.where(row == 0, prev_sc[s], pltpu.roll(rkv, 1, axis=0))
        prev_sc[s] = rkv[tt - 1:tt, :]
        mixed = rkv + (prev - rkv) * prm_ref[0:1, :]
        r = mixed[:, 0:bw]
        k = mixed[:, bw:2 * bw]
        v = mixed[:, 2 * bw:]
        a = a_ref[s]
        if has_vres:
            v = v + (vf_ref[s] - v) * vg_ref[s]
        else:
            v_out_ref[s] = v
        kk = k * k_k
        kk = kk * lax.rsqrt(jnp.maximum(_head_sum(kk * kk, ones_bf), 1e-24))
        k = k * (1.0 + (a - 1.0) * k_a)
        seqs.append((r, k, v, kk, -(kk * a), lw_ref[s]))
    n_chunks = tt // CHUNK
    tiles = [tuple(z[c * CHUNK:(c + 1) * CHUNK] for z in seqs[s])
             for c in range(n_chunks) for s in range(n_seq)]
    parts = _rwkv_state_free(tiles, tril_bf, hl, strict, incl, eye)
    sts = [st_sc[s] for s in range(n_seq)]
    ys = [[] for _ in range(n_seq)]
    for c in range(n_chunks):
        y_c, sts = _rwkv_state_step(parts[c * n_seq:(c + 1) * n_seq], sts, bdm, hl)
        for s in range(n_seq):
            ys[s].append(y_c[s])
    for s in range(n_seq):
        r, k, v = seqs[s][0:3]
        st_sc[s] = sts[s]
        y = ys[s][0] if len(ys[s]) == 1 else jnp.concatenate(ys[s], axis=0)
        y = _head_layer_norm(y, ones_bf, ln_g, ln_b, RWKV_GN_EPS)
        bonus = _head_sum(r * k * r_k, ones_bf) * v
        o_ref[s] = (y + bonus) * g_ref[s]

    @pl.when(tb == pl.num_programs(1) - 1)
    def _():
        for s in range(n_seq):
            st_out_ref[s] = sts[s]


def _rwkv_mixer(p3, aux3, v_first, rkv_last, st0, prm, consts, tt):
    b, t, _ = p3.shape
    bw = BR_WIDTH
    pb = SEQ_PAIR
    has_vres = v_first is not None
    tok = lambda j: pl.BlockSpec((pb, tt, bw), lambda bi, ti, j=j: (bi, ti, j))
    in_specs = [pl.BlockSpec((pb, tt, RWKV_COLS), lambda bi, ti: (bi, ti, 0)),
                tok(0), tok(1), tok(2)]
    args = [p3, aux3, aux3, aux3]
    if has_vres:
        in_specs += [tok(3), pl.BlockSpec((pb, tt, bw), lambda bi, ti: (bi, ti, 0))]
        args += [aux3, v_first]
    in_specs += [pl.BlockSpec((pb, 1, RWKV_COLS), lambda bi, ti: (bi, 0, 0)),
                 pl.BlockSpec((pb, bw, bw), lambda bi, ti: (bi, 0, 0)),
                 _const_spec(prm.shape)]
    args += [rkv_last, st0, prm]
    for name in ("tril", "bdm", "head_lanes", "strict", "incl", "eye"):
        in_specs.append(_const_spec(consts[name].shape))
        args.append(consts[name])
    seq = pl.BlockSpec((pb, tt, bw), lambda bi, ti: (bi, ti, 0))
    st_spec = pl.BlockSpec((pb, bw, bw), lambda bi, ti: (bi, 0, 0))
    seq_shape = jax.ShapeDtypeStruct((b, t, bw), F32)
    st_shape = jax.ShapeDtypeStruct((b, bw, bw), F32)
    if has_vres:
        out_specs, out_shape = [seq, st_spec], [seq_shape, st_shape]
    else:
        out_specs, out_shape = [seq, seq, st_spec], [seq_shape, seq_shape, st_shape]
    return pl.pallas_call(
        functools.partial(_rwkv_kernel, has_vres),
        grid=(b // pb, t // tt),
        in_specs=in_specs, out_specs=out_specs, out_shape=out_shape,
        scratch_shapes=[pltpu.VMEM((pb, bw, bw), F32), pltpu.VMEM((pb, 1, RWKV_COLS), F32)],
        compiler_params=_params(("arbitrary", "arbitrary"), 48),
        name="rwkv_mixer",
    )(*args)


def _rot_half(z):
    w = z.shape[1]
    half = HEAD_DIM // 2
    lane = lax.broadcasted_iota(jnp.int32, z.shape, 1)
    first = (lane % HEAD_DIM) < half
    return jnp.where(first, pltpu.roll(z, w - half, axis=1), pltpu.roll(z, half, axis=1))


def _ret_kernel(q_ref, k_ref, v_ref, g_ref, cos_ref, sin_ref, st0_ref, prm_ref, dec_ref, bdm_ref, hl_ref,
                o_ref, st_out_ref, st_sc):
    tb = pl.program_id(1)

    @pl.when(tb == 0)
    def _():
        st_sc[...] = st0_ref[...]

    cos = cos_ref[...]
    sin = sin_ref[...]
    bdm = bdm_ref[...]
    hl = hl_ref[...]
    ones_bf = bdm.astype(BF16)
    q_dec = dec_ref[0:CHUNK, :]
    k_dec = dec_ref[CHUNK:2 * CHUNK, :]
    d_mat = dec_ref[2 * CHUNK:3 * CHUNK, :]
    s_dec = dec_ref[3 * CHUNK:3 * CHUNK + 1, :]
    n_seq = q_ref.shape[0]
    tt = q_ref.shape[1]
    qs, ks, vs = [], [], []
    for s in range(n_seq):
        q = q_ref[s]
        k = k_ref[s]
        qs.append(q * cos + _rot_half(q) * sin)
        ks.append((k * cos + _rot_half(k) * sin) * (HEAD_DIM ** -0.5))
        vs.append(v_ref[s])
    sts = [st_sc[s] for s in range(n_seq)]
    outs = [[] for _ in range(n_seq)]
    for c in range(tt // CHUNK):
        sl = slice(c * CHUNK, (c + 1) * CHUNK)
        for s in range(n_seq):
            qc, kc, vc = qs[s][sl], ks[s][sl], vs[s][sl]
            att = _dot(qc, _stack_heads(kc, hl), NT) * d_mat
            outs[s].append(_dot(qc * q_dec, sts[s], NT) + _dot(att, _stack_heads(vc, hl)))
            sts[s] = sts[s] * s_dec + bdm * _dot(vc, kc * k_dec, TN)
    for s in range(n_seq):
        st_sc[s] = sts[s]
        o = outs[s][0] if len(outs[s]) == 1 else jnp.concatenate(outs[s], axis=0)
        o = _head_layer_norm(o, ones_bf, prm_ref[0:1, :], prm_ref[1:2, :], LN_EPS)
        o_ref[s] = o * _silu(g_ref[s])

    @pl.when(tb == pl.num_programs(1) - 1)
    def _():
        for s in range(n_seq):
            st_out_ref[s] = sts[s]


def _ret_mixer(p3, cos_t, sin_t, st0, prm, consts, tt):
    b, t, _ = p3.shape
    bw = BR_WIDTH
    pb = SEQ_PAIR
    base = RWKV_COLS // bw
    tok = lambda j: pl.BlockSpec((pb, tt, bw), lambda bi, ti, j=j: (bi, ti, base + j))
    tab = pl.BlockSpec((tt, bw), lambda bi, ti: (ti, 0))
    st_spec = pl.BlockSpec((pb, bw, bw), lambda bi, ti: (bi, 0, 0))
    dec, bdm, hl = consts["ret_dec"], consts["bdm"], consts["head_lanes"]
    return pl.pallas_call(
        _ret_kernel,
        grid=(b // pb, t // tt),
        in_specs=[tok(0), tok(1), tok(2), tok(3), tab, tab, st_spec, _const_spec(prm.shape),
                  _const_spec(dec.shape), _const_spec(bdm.shape), _const_spec(hl.shape)],
        out_specs=[pl.BlockSpec((pb, tt, bw), lambda bi, ti: (bi, ti, 0)), st_spec],
        out_shape=[jax.ShapeDtypeStruct((b, t, bw), F32), jax.ShapeDtypeStruct((b, bw, bw), F32)],
        scratch_shapes=[pltpu.VMEM((pb, bw, bw), F32)],
        compiler_params=_params(("arbitrary", "arbitrary"), 40),
        name="ret_mixer",
    )(p3, p3, p3, p3, cos_t, sin_t, st0, prm, dec, bdm, hl)


def _gated_mixer_tail(seqs, norm_g, st_sc, st_out_ref, o_ref, tril_bf, pair_ones_bf, st_mask,
                      lanes_k, lanes_v, ones_v_bf):
    n_seq = len(seqs)
    tt = seqs[0][0].shape[0]
    n_chunks = tt // CHUNK
    tiles = [tuple(z[c * CHUNK:(c + 1) * CHUNK] for z in seqs[s][0:4])
             for c in range(n_chunks) for s in range(n_seq)]
    parts = _gla_state_free(tiles, tril_bf, pair_ones_bf, st_mask, lanes_k, lanes_v)
    sts = [st_sc[s] for s in range(n_seq)]
    outs = [[] for _ in range(n_seq)]
    for c in range(n_chunks):
        for s in range(n_seq):
            o_intra, q_dec, upd, dec = parts[c * n_seq + s]
            outs[s].append(o_intra + _dot(q_dec, sts[s], NT))
            sts[s] = sts[s] * dec + upd
    for s in range(n_seq):
        st_sc[s] = sts[s]
        o = outs[s][0] if len(outs[s]) == 1 else jnp.concatenate(outs[s], axis=0)
        o_ref[s] = _head_rms_norm(o, ones_v_bf, norm_g) * _silu(seqs[s][4])

    @pl.when(pl.program_id(1) == pl.num_programs(1) - 1)
    def _():
        for s in range(n_seq):
            st_out_ref[s] = sts[s]


def _hgrn_kernel(layer, q_ref, f_ref, i_ref, g_ref, st0_ref, lbl_ref, ng_ref, tril_ref, bdm_ref, hl_ref,
                 o_ref, st_out_ref, st_sc):
    @pl.when(pl.program_id(1) == 0)
    def _():
        st_sc[...] = st0_ref[...]

    logits = lbl_ref[...]
    ex = jnp.exp(logits - jnp.max(logits, axis=0, keepdims=True))
    sm = ex / jnp.sum(ex, axis=0, keepdims=True)
    lb = jnp.zeros((1, BR_WIDTH), F32)
    for d in range(1, layer + 1):
        lb = lb + sm[d:d + 1, :]
    seqs = []
    for s in range(q_ref.shape[0]):
        fz = f_ref[s]
        f = lb + (1.0 - lb) * _sigmoid(fz)
        k = (1.0 - lb) * _sigmoid(-fz)
        seqs.append((_silu(q_ref[s]), k, i_ref[s], jnp.log(f), g_ref[s]))
    bdm = bdm_ref[...]
    ones_bf = bdm.astype(BF16)
    hl = hl_ref[...]
    _gated_mixer_tail(seqs, ng_ref[...], st_sc, st_out_ref, o_ref, tril_ref[...], ones_bf, bdm, hl, hl,
                      ones_bf)


def _hgrn_mixer(p3, st0, lb_logits, norm_g, consts, layer, tt):
    b, t, _ = p3.shape
    bw = BR_WIDTH
    pb = SEQ_PAIR
    base = (RWKV_COLS + 4 * bw) // bw
    tok = lambda j: pl.BlockSpec((pb, tt, bw), lambda bi, ti, j=j: (bi, ti, base + j))
    st_spec = pl.BlockSpec((pb, bw, bw), lambda bi, ti: (bi, 0, 0))
    tril, bdm, hl = consts["tril"], consts["bdm"], consts["head_lanes"]
    return pl.pallas_call(
        functools.partial(_hgrn_kernel, layer),
        grid=(b // pb, t // tt),
        in_specs=[tok(0), tok(1), tok(2), tok(3), st_spec, _const_spec(lb_logits.shape),
                  _const_spec(norm_g.shape), _const_spec(tril.shape), _const_spec(bdm.shape),
                  _const_spec(hl.shape)],
        out_specs=[pl.BlockSpec((pb, tt, bw), lambda bi, ti: (bi, ti, 0)), st_spec],
        out_shape=[jax.ShapeDtypeStruct((b, t, bw), F32), jax.ShapeDtypeStruct((b, bw, bw), F32)],
        scratch_shapes=[pltpu.VMEM((pb, bw, bw), F32)],
        compiler_params=_params(("arbitrary", "arbitrary"), 48),
        name="hgrn_mixer",
    )(p3, p3, p3, p3, st0, lb_logits, norm_g, tril, bdm, hl)


def _gla_kernel(q_ref, k_ref, v_ref, g_ref, la_ref, st0_ref, ng_ref, tril_ref, pair_ref, mask_ref,
                bdm_ref, hlk_ref, hlv_ref, o_ref, st_out_ref, st_sc):
    @pl.when(pl.program_id(1) == 0)
    def _():
        st_sc[...] = st0_ref[...]

    seqs = [(q_ref[s] * (GLA_DK ** -0.5), k_ref[s], v_ref[s], la_ref[s], g_ref[s])
            for s in range(q_ref.shape[0])]
    _gated_mixer_tail(seqs, ng_ref[...], st_sc, st_out_ref, o_ref, tril_ref[...], pair_ref[...],
                      mask_ref[...], hlk_ref[...], hlv_ref[...], bdm_ref[...].astype(BF16))


def _gla_mixer(p3, aux3, st0, norm_g, consts, tt):
    b, t, _ = p3.shape
    bw, kw = BR_WIDTH, GLA_KW
    pb = SEQ_PAIR
    gla0 = RWKV_COLS + 8 * bw
    st_spec = pl.BlockSpec((pb, bw, kw), lambda bi, ti: (bi, 0, 0))
    names = ("tril", "gla_pair", "gla_mask", "bdm", "gla_head_lanes", "head_lanes")
    return pl.pallas_call(
        _gla_kernel,
        grid=(b // pb, t // tt),
        in_specs=[pl.BlockSpec((pb, tt, kw), lambda bi, ti: (bi, ti, gla0 // kw)),
                  pl.BlockSpec((pb, tt, kw), lambda bi, ti: (bi, ti, gla0 // kw + 1)),
                  pl.BlockSpec((pb, tt, bw), lambda bi, ti: (bi, ti, (gla0 + 2 * kw) // bw)),
                  pl.BlockSpec((pb, tt, bw), lambda bi, ti: (bi, ti, (gla0 + 2 * kw) // bw + 1)),
                  pl.BlockSpec((pb, tt, kw), lambda bi, ti: (bi, ti, 4 * bw // kw)),
                  st_spec, _const_spec(norm_g.shape)] + [_const_spec(consts[nm].shape) for nm in names],
        out_specs=[pl.BlockSpec((pb, tt, bw), lambda bi, ti: (bi, ti, 0)), st_spec],
        out_shape=[jax.ShapeDtypeStruct((b, t, bw), F32), jax.ShapeDtypeStruct((b, bw, kw), F32)],
        scratch_shapes=[pltpu.VMEM((pb, bw, kw), F32)],
        compiler_params=_params(("arbitrary", "arbitrary"), 48),
        name="gla_mixer",
    )(p3, p3, p3, p3, aux3, st0, norm_g, *[consts[nm] for nm in names])


def _merge_kernel(emit_bf16, x_ref, o0_ref, o1_ref, o2_ref, o3_ref, wg_ref, bg_ref, wbr_ref, wo_ref,
                  ln_ref, *out_refs):
    branches = (o0_ref, o1_ref, o2_ref, o3_ref)
    half = x_ref.shape[0] // 2
    rows = [slice(0, half), slice(half, 2 * half)]
    x = [x_ref[r, :] for r in rows]
    xb = [v.astype(BF16) for v in x]
    merged = [None, None]
    for m in range(N_BRANCH):
        pre = [jnp.dot(xb[h], wg_ref[m], preferred_element_type=F32) for h in range(2)]
        proj = [jnp.dot(branches[m][rows[h], :].astype(BF16), wbr_ref[m], preferred_element_type=F32)
                for h in range(2)]
        for h in range(2):
            term = _sigmoid(pre[h] + bg_ref[m:m + 1, :]) * proj[h]
            merged[h] = term if merged[h] is None else merged[h] + term
    y = [ALPHA * x[h] + jnp.dot(merged[h].astype(BF16), wo_ref[...], preferred_element_type=F32)
         for h in range(2)]
    for h in range(2):
        yn = _layer_norm_rows(y[h], ln_ref[0:1, :], ln_ref[1:2, :])
        out_refs[0][rows[h], :] = yn
        if emit_bf16:
            out_refs[1][rows[h], :] = yn.astype(BF16)


def _merge(x2, outs, wg, bg, wbr, wo, ln, tm, emit_bf16):
    n = x2.shape[0]
    row = pl.BlockSpec((tm, D_MODEL), lambda i: (i, 0))
    br = pl.BlockSpec((tm, BR_WIDTH), lambda i: (i, 0))
    out_specs = [row]
    out_shape = [jax.ShapeDtypeStruct((n, D_MODEL), F32)]
    if emit_bf16:
        out_specs.append(row)
        out_shape.append(jax.ShapeDtypeStruct((n, D_MODEL), BF16))
    return pl.pallas_call(
        functools.partial(_merge_kernel, emit_bf16),
        grid=(n // tm,),
        in_specs=[row, br, br, br, br, _const_spec(wg.shape), _const_spec(bg.shape),
                  _const_spec(wbr.shape), _const_spec(wo.shape), _const_spec(ln.shape)],
        out_specs=out_specs, out_shape=out_shape,
        compiler_params=_params(("arbitrary",), 56),
        name="merge",
    )(x2, *outs, wg, bg, wbr, wo, ln)


FF_SPLIT = 2
FF_PART = D_FF // FF_SPLIT
FFN_PARTS = (1024, 1024, 768)


def _ffn_kernel(x_ref, wg_ref, wu_ref, wd_ref, ln_ref, o_ref):
    x = x_ref[...]
    xb = x.astype(BF16)
    acc = ALPHA * x
    lo = 0
    for width in FFN_PARTS:
        cs = slice(lo, lo + width)
        lo += width
        h = (_silu(jnp.dot(xb, wg_ref[:, cs], preferred_element_type=F32))
             * jnp.dot(xb, wu_ref[:, cs], preferred_element_type=F32))
        acc = acc + jnp.dot(h.astype(BF16), wd_ref[cs, :], preferred_element_type=F32)
    o_ref[...] = _layer_norm_rows(acc, ln_ref[0:1, :], ln_ref[1:2, :])


def _ffn(x2, wg, wu, wd, ln, tm):
    n = x2.shape[0]
    row = pl.BlockSpec((tm, D_MODEL), lambda i: (i, 0))
    return pl.pallas_call(
        _ffn_kernel,
        grid=(n // tm,),
        in_specs=[row, _const_spec(wg.shape), _const_spec(wu.shape), _const_spec(wd.shape),
                  _const_spec(ln.shape)],
        out_specs=row,
        out_shape=jax.ShapeDtypeStruct((n, D_MODEL), F32),
        compiler_params=_params(("arbitrary",), 60),
        name="ffn",
    )(x2, wg, wu, wd, ln)


def _router_kernel(x_ref, wr_ref, br_ref, tril_ref, rank_ref, wsel_ref, cnt_ref):
    logits = _dot3(x_ref[...], wr_ref[...]) + br_ref[...]
    lane = lax.broadcasted_iota(jnp.int32, logits.shape, 1)
    neg = jnp.float32(-jnp.inf)
    logits = jnp.where(lane < N_EXPERTS, logits, neg)
    m1 = jnp.max(logits, axis=1, keepdims=True)
    lane_f = lane.astype(F32)
    i1 = jnp.min(jnp.where(logits == m1, lane_f, float(LANE)), axis=1, keepdims=True)
    first = lane_f == i1
    rest = jnp.where(first, neg, logits)
    m2 = jnp.max(rest, axis=1, keepdims=True)
    i2 = jnp.min(jnp.where(rest == m2, lane_f, float(LANE)), axis=1, keepdims=True)
    second = lane_f == i2
    e = jnp.exp(m2 - m1)
    w1 = 1.0 / (1.0 + e)
    w2 = e / (1.0 + e)
    sel = jnp.where(first, 1.0, jnp.where(second, 1.0, 0.0))
    wsel_ref[...] = jnp.where(first, w1, jnp.where(second, w2, 0.0))
    sel_bf = sel.astype(BF16)
    rank = jnp.dot(tril_ref[...], sel_bf, preferred_element_type=F32)
    rank_ref[...] = jnp.where(sel > 0.5, rank, -1.0)
    ones = jnp.ones((SUBLANE, sel.shape[0]), BF16)
    cnt_ref[0] = jnp.dot(ones, sel_bf, preferred_element_type=F32).astype(jnp.int32)


def _router(x2, wr, br, tril, tm):
    n = x2.shape[0]
    nt = n // tm
    col = pl.BlockSpec((tm, LANE), lambda i: (i, 0))
    return pl.pallas_call(
        _router_kernel,
        grid=(nt,),
        in_specs=[pl.BlockSpec((tm, D_MODEL), lambda i: (i, 0)), _const_spec(wr.shape),
                  _const_spec(br.shape), _const_spec(tril.shape)],
        out_specs=[col, col, pl.BlockSpec((1, SUBLANE, LANE), lambda i: (i, 0, 0))],
        out_shape=[jax.ShapeDtypeStruct((n, LANE), F32), jax.ShapeDtypeStruct((n, LANE), F32),
                   jax.ShapeDtypeStruct((nt, SUBLANE, LANE), jnp.int32)],
        compiler_params=_params(("arbitrary",), 40),
        name="router",
    )(x2, wr, br, tril)


def _moe_kernel(rows, cnt_ref, x_ref, xb_ref, rrow_ref, rank_ref, wsel_ref, wg_ref, wu_ref, wd_ref,
                ln_ref, o_ref, xg_sc, yb_sc):
    i = pl.program_id(0)
    e = pl.program_id(1)
    c = pl.program_id(2)
    n_e = pl.num_programs(1)
    n_c = pl.num_programs(2)
    cnt = cnt_ref[i * N_EXPERTS + e]
    n_blk = (cnt + rows - 1) // rows
    tm = xb_ref.shape[0]

    @pl.when((e == 0) & (c == 0))
    def _():
        o_ref[...] = ALPHA * x_ref[...]

    def gather(blk, carry):
        r0 = pl.multiple_of(blk * rows, SUBLANE)
        slot = (lax.broadcasted_iota(jnp.int32, (rows, tm), 0) + r0).astype(F32)
        onehot = jnp.where(rrow_ref[0] == slot, 1.0, 0.0).astype(BF16)
        xg_sc[pl.ds(r0, rows), :] = jnp.dot(onehot, xb_ref[...], preferred_element_type=F32).astype(BF16)
        return carry

    @pl.when(c == 0)
    def _():
        lax.fori_loop(0, n_blk, gather, 0)

    def expert(blk, carry):
        r0 = pl.multiple_of(blk * rows, SUBLANE)
        xg = xg_sc[pl.ds(r0, rows), :]
        h = (_silu(jnp.dot(xg, wg_ref[0], preferred_element_type=F32))
             * jnp.dot(xg, wu_ref[0], preferred_element_type=F32))
        yb = jnp.dot(h.astype(BF16), wd_ref[0], preferred_element_type=F32)

        @pl.when(c == 0)
        def _():
            yb_sc[pl.ds(r0, rows), :] = yb

        @pl.when(c > 0)
        def _():
            yb_sc[pl.ds(r0, rows), :] = yb_sc[pl.ds(r0, rows), :] + yb
        return carry

    lax.fori_loop(0, n_blk, expert, 0)

    @pl.when(c == n_c - 1)
    def _():
        mine = lax.broadcasted_iota(jnp.int32, (tm, LANE), 1) == e
        rank_col = jnp.sum(jnp.where(mine, rank_ref[...], 0.0), axis=1, keepdims=True)
        w_col = jnp.sum(jnp.where(mine, wsel_ref[...], 0.0), axis=1, keepdims=True)

        def scatter(blk, carry):
            r0 = pl.multiple_of(blk * rows, SUBLANE)
            slot = (lax.broadcasted_iota(jnp.int32, (tm, rows), 1) + r0).astype(F32)
            onehot = jnp.where(rank_col == slot, 1.0, 0.0).astype(BF16)
            back = jnp.dot(onehot, yb_sc[pl.ds(r0, rows), :].astype(BF16), preferred_element_type=F32)
            o_ref[...] = o_ref[...] + w_col * back
            return carry

        lax.fori_loop(0, n_blk, scatter, 0)

    @pl.when((e == n_e - 1) & (c == n_c - 1))
    def _():
        o_ref[...] = _layer_norm_rows(o_ref[...], ln_ref[0:1, :], ln_ref[1:2, :])


def _moe(x2, xb2, counts, rank_row, rank, wsel, wg, wu, wd, ln, tm, rows):
    n = x2.shape[0]
    nt = n // tm
    cap = -(-tm // rows) * rows
    grid_spec = pltpu.PrefetchScalarGridSpec(
        num_scalar_prefetch=1,
        grid=(nt, N_EXPERTS, FF_SPLIT),
        in_specs=[
            pl.BlockSpec((tm, D_MODEL), lambda i, e, c, cnt: (i, 0)),
            pl.BlockSpec((tm, D_MODEL), lambda i, e, c, cnt: (i, 0)),
            pl.BlockSpec((1, 1, tm), lambda i, e, c, cnt: (e, 0, i)),
            pl.BlockSpec((tm, LANE), lambda i, e, c, cnt: (i, 0)),
            pl.BlockSpec((tm, LANE), lambda i, e, c, cnt: (i, 0)),
            pl.BlockSpec((1, D_MODEL, FF_PART), lambda i, e, c, cnt: (e, 0, c)),
            pl.BlockSpec((1, D_MODEL, FF_PART), lambda i, e, c, cnt: (e, 0, c)),
            pl.BlockSpec((1, FF_PART, D_MODEL), lambda i, e, c, cnt: (e, c, 0)),
            pl.BlockSpec((2, D_MODEL), lambda i, e, c, cnt: (0, 0)),
        ],
        out_specs=pl.BlockSpec((tm, D_MODEL), lambda i, e, c, cnt: (i, 0)),
        scratch_shapes=[pltpu.VMEM((cap, D_MODEL), BF16), pltpu.VMEM((cap, D_MODEL), F32)],
    )
    return pl.pallas_call(
        functools.partial(_moe_kernel, rows),
        grid_spec=grid_spec,
        out_shape=jax.ShapeDtypeStruct((n, D_MODEL), F32),
        compiler_params=_params(("arbitrary", "arbitrary", "arbitrary"), 56),
        name="moe",
    )(counts, x2, xb2, rank_row, rank, wsel, wg, wu, wd, ln)


def _tile_sizes(b, t):
    n = b * t
    tm = min(512, n)
    tm_proj = min(512, t)
    tt = min(512, t)
    tm_moe = min(1024, n)
    rows = 288 if tm_moe == 1024 else 160
    return tm, tm_proj, tt, tm_moe, rows


def _to_block_diag(s):
    b, h, r, c = s.shape
    eye = jnp.eye(h, dtype=s.dtype)
    return jnp.einsum("bhrc,hg->bhrgc", s, eye).reshape(b, h * r, h * c)


def _from_block_diag(s, r, c):
    b = s.shape[0]
    s5 = s.reshape(b, N_HEADS, r, N_HEADS, c)
    return jnp.stack([s5[:, h, :, h, :] for h in range(N_HEADS)], axis=1)


def _pad_cols(a, width):
    return jnp.pad(a, ((0, 0), (0, width - a.shape[1])))


def _prep_layer(l, p):
    d = D_MODEL
    bw = BR_WIDTH
    w = {}
    w["win"] = p["w_in"][l].astype(BF16)
    mu = p["rwkv_mu_x"][l]
    if l >= 1:
        v1, v2, v0, mu_v = p["rwkv_v1"][l - 1], p["rwkv_v2"][l - 1], p["rwkv_v0"][l - 1], p["rwkv_mu_v"][l - 1]
    else:
        v1, v2 = jnp.zeros((d, RWKV_V_LORA), F32), jnp.zeros((RWKV_V_LORA, bw), F32)
        v0, mu_v = jnp.zeros((bw,), F32), jnp.zeros((d,), F32)
    first = [p["rwkv_w1"][l], p["rwkv_a1"][l], p["rwkv_g1"][l], v1, p["gla_w1"][l]]
    shift_mu = [mu[0], mu[1], mu[2], mu_v, jnp.zeros((d,), F32)]
    w["wl1"] = _pad_cols(jnp.concatenate(first, axis=1), LORA_COLS).astype(BF16)
    w["wl1mu"] = _pad_cols(jnp.concatenate([m[:, None] * a for m, a in zip(shift_mu, first)], axis=1),
                           LORA_COLS).astype(BF16)
    second = jax.scipy.linalg.block_diag(p["rwkv_w2"][l], p["rwkv_a2"][l], p["rwkv_g2"][l], v2, p["gla_w2"][l])
    w["w2"] = jnp.pad(second, ((0, LORA_COLS - second.shape[0]), (0, 0))).astype(BF16)
    w["bias"] = jnp.concatenate([p["rwkv_w0"][l], p["rwkv_a0"][l], jnp.zeros((bw,), F32), v0,
                                 p["gla_b"][l]])[None]
    rows = [p["rwkv_mu_rkv"][l].reshape(RWKV_COLS)]
    rows += [jnp.pad(p[name][l], (0, RWKV_COLS - bw))
             for name in ("rwkv_k_k", "rwkv_k_a", "rwkv_r_k", "rwkv_ln_g", "rwkv_ln_b")]
    rows += [jnp.zeros((RWKV_COLS,), F32)] * (SUBLANE - len(rows))
    w["rwkv_prm"] = jnp.stack(rows)
    w["ret_prm"] = jnp.stack([p["ret_gn_g"][l], p["ret_gn_b"][l]])
    w["hgrn_ng"] = p["hgrn_norm_g"][l][None]
    w["gla_ng"] = p["gla_norm_g"][l][None]
    w["wg"] = p["w_gate"][l].astype(BF16)
    w["bg"] = p["b_gate"][l]
    w["wbr"] = p["w_br"][l].astype(BF16)
    w["wo"] = p["w_o"][l].astype(BF16)
    w["ln1"] = jnp.stack([p["ln1_g"][l], p["ln1_b"][l]])
    w["ln2"] = jnp.stack([p["ln2_g"][l], p["ln2_b"][l]])
    j = l // 2
    if l % 2 == 0:
        w["ffn"] = (p["ffn_w_gate"][j].astype(BF16), p["ffn_w_up"][j].astype(BF16),
                    p["ffn_w_down"][j].astype(BF16))
    else:
        wr = _pad_cols(p["router_w"][j], LANE)
        br = _pad_cols(p["router_b"][j][None], LANE)
        w["moe"] = (wr, br, p["moe_w_gate"][j].astype(BF16), p["moe_w_up"][j].astype(BF16),
                    p["moe_w_down"][j].astype(BF16))
    return w


def _mixer_consts():
    bdm = _np_block_mask(HEAD_DIM, HEAD_DIM)
    lg = np.log1p(-np.exp2(-5.0 - np.arange(N_HEADS, dtype=np.float64)))
    lg_l = np.repeat(lg, HEAD_DIM)[None, :]
    t = np.arange(CHUNK, dtype=np.float64)[:, None]
    s_side = (np.arange(N_HEADS * CHUNK) % CHUNK)[None, :].astype(np.float64)
    lg_side = np.repeat(lg, CHUNK)[None, :]
    d_mat = np.where(s_side <= t, np.exp((t - s_side) * lg_side), 0.0)
    dec = np.zeros((3 * CHUNK + SUBLANE, BR_WIDTH), np.float64)
    dec[0:CHUNK] = np.exp((t + 1.0) * lg_l)
    dec[CHUNK:2 * CHUNK] = np.exp((CHUNK - 1.0 - t) * lg_l)
    dec[2 * CHUNK:3 * CHUNK] = d_mat
    dec[3 * CHUNK] = np.exp(CHUNK * lg_l[0])
    return {
        "tril": jnp.asarray(np.tril(np.ones((CHUNK, CHUNK), np.float32)), BF16),
        "bdm": jnp.asarray(bdm, F32),
        "head_lanes": jnp.asarray(_np_head_lanes(HEAD_DIM), F32),
        "strict": jnp.asarray(_np_causal_side_by_side(True), F32),
        "incl": jnp.asarray(_np_causal_side_by_side(False), F32),
        "eye": jnp.asarray(np.tile(np.eye(CHUNK, dtype=np.float32), (1, N_HEADS)), F32),
        "ret_dec": jnp.asarray(dec, F32),
        "gla_pair": jnp.asarray(_np_block_mask(GLA_DK, HEAD_DIM), BF16),
        "gla_mask": jnp.asarray(_np_block_mask(HEAD_DIM, GLA_DK), F32),
        "gla_head_lanes": jnp.asarray(_np_head_lanes(GLA_DK), F32),
    }


def _rope_tables(pos0, t):
    half = HEAD_DIM // 2
    pos = pos0 + jnp.arange(t, dtype=F32)
    inv = ROPE_THETA ** (-jnp.arange(half, dtype=F32) / half)
    ang = pos[:, None] * inv[None]
    cos = jnp.cos(ang)
    sin = jnp.sin(ang)
    cos_t = jnp.tile(jnp.concatenate([cos, cos], axis=1), (1, N_HEADS))
    sin_t = jnp.tile(jnp.concatenate([-sin, sin], axis=1), (1, N_HEADS))
    return cos_t, sin_t


def _previous_rows(x, x_last, tm):
    b, t, d = x.shape
    per_seq = t // tm
    tails = x.reshape(b, per_seq, tm, d)[:, :, tm - 1, :]
    prev = jnp.concatenate([x_last[:, None, :], tails[:, :per_seq - 1, :]], axis=1)
    return prev.reshape(b * per_seq, 1, d)


def _run_trunk(x, pos0, s_rwkv, c_shift, s_ret, s_hgrn, s_gla, prm, layers, consts):
    b, t, d = x.shape
    n = b * t
    tm, tm_proj, tt, tm_moe, rows = _tile_sizes(b, t)
    cos_t, sin_t = _rope_tables(pos0, t)
    v_first = None
    new_rwkv, new_shift, new_ret, new_hgrn, new_gla = [], [], [], [], []
    for l in range(DEPTH):
        w = layers[l]
        x_in = x
        x_last = c_shift[l]
        p2, aux2 = _in_proj(x.reshape(n, d), _previous_rows(x, x_last, tm_proj), w["win"], w["wl1"],
                            w["wl1mu"], w["w2"], w["bias"], tm_proj)
        p3 = p2.reshape(b, t, IN_COLS)
        aux3 = aux2.reshape(b, t, AUX_COLS)
        pad = (-b) % SUBLANE
        x_last_p = jnp.concatenate([x_last, jnp.zeros((pad, d), F32)], axis=0) if pad else x_last
        rkv_last = _rows_matmul(x_last_p, w["win"][:, :RWKV_COLS])[:b, None, :]

        res = _rwkv_mixer(p3, aux3, v_first, rkv_last, _to_block_diag(s_rwkv[l]), w["rwkv_prm"], consts, tt)
        if v_first is None:
            o_rwkv, v_first, st_rwkv = res
        else:
            o_rwkv, st_rwkv = res
        o_ret, st_ret = _ret_mixer(p3, cos_t, sin_t, _to_block_diag(jnp.swapaxes(s_ret[l], -1, -2)),
                                   w["ret_prm"], consts, tt)
        o_hgrn, st_hgrn = _hgrn_mixer(p3, _to_block_diag(jnp.swapaxes(s_hgrn[l], -1, -2)),
                                      prm["hgrn_lb_logits"], w["hgrn_ng"], consts, l, tt)
        o_gla, st_gla = _gla_mixer(p3, aux3, _to_block_diag(jnp.swapaxes(s_gla[l], -1, -2)), w["gla_ng"],
                                   consts, tt)

        outs = [o.reshape(n, BR_WIDTH) for o in (o_rwkv, o_ret, o_hgrn, o_gla)]
        is_moe = l % 2 == 1
        merged = _merge(x.reshape(n, d), outs, w["wg"], w["bg"], w["wbr"], w["wo"], w["ln1"], tm, is_moe)
        if not is_moe:
            x1 = merged[0]
            x2 = _ffn(x1, *w["ffn"], w["ln2"], tm_moe)
        else:
            x1, x1b = merged
            wr, br, mg, mu_, md = w["moe"]
            tril_m = jnp.asarray(np.tril(np.ones((tm_moe, tm_moe), np.float32), -1), BF16)
            rank, wsel, cnt = _router(x1, wr, br, tril_m, tm_moe)
            rank_row = rank[:, :N_EXPERTS].T.reshape(N_EXPERTS, 1, n)
            counts = cnt[:, 0, :N_EXPERTS].reshape(-1)
            x2 = _moe(x1, x1b, counts, rank_row, rank, wsel, mg, mu_, md, w["ln2"], tm_moe, rows)
        x = x2.reshape(b, t, d)

        new_rwkv.append(_from_block_diag(st_rwkv, HEAD_DIM, HEAD_DIM))
        new_shift.append(x_in[:, -1])
        new_ret.append(jnp.swapaxes(_from_block_diag(st_ret, HEAD_DIM, HEAD_DIM), -1, -2))
        new_hgrn.append(jnp.swapaxes(_from_block_diag(st_hgrn, HEAD_DIM, HEAD_DIM), -1, -2))
        new_gla.append(jnp.swapaxes(_from_block_diag(st_gla, HEAD_DIM, GLA_DK), -1, -2))
    return (x, jnp.stack(new_rwkv), jnp.stack(new_shift), jnp.stack(new_ret), jnp.stack(new_hgrn),
            jnp.stack(new_gla))


def kernel(x_prompt, x_sample, state_rwkv, cache_shift, state_ret, state_hgrn, state_gla, w_in, rwkv_mu_rkv, rwkv_mu_x, rwkv_mu_v, rwkv_w0, rwkv_w1, rwkv_w2, rwkv_a0, rwkv_a1, rwkv_a2, rwkv_v0, rwkv_v1, rwkv_v2, rwkv_g1, rwkv_g2, rwkv_k_k, rwkv_k_a, rwkv_r_k, rwkv_ln_g, rwkv_ln_b, ret_gn_g, ret_gn_b, hgrn_lb_logits, hgrn_norm_g, gla_w1, gla_w2, gla_b, gla_norm_g, w_br, w_gate, b_gate, w_o, ln1_g, ln1_b, ln2_g, ln2_b, ffn_w_gate, ffn_w_up, ffn_w_down, router_w, router_b, moe_w_gate, moe_w_up, moe_w_down):
    prm = {
        'w_in': w_in, 'rwkv_mu_rkv': rwkv_mu_rkv, 'rwkv_mu_x': rwkv_mu_x, 'rwkv_mu_v': rwkv_mu_v,
        'rwkv_w0': rwkv_w0, 'rwkv_w1': rwkv_w1, 'rwkv_w2': rwkv_w2,
        'rwkv_a0': rwkv_a0, 'rwkv_a1': rwkv_a1, 'rwkv_a2': rwkv_a2,
        'rwkv_v0': rwkv_v0, 'rwkv_v1': rwkv_v1, 'rwkv_v2': rwkv_v2,
        'rwkv_g1': rwkv_g1, 'rwkv_g2': rwkv_g2, 'rwkv_k_k': rwkv_k_k, 'rwkv_k_a': rwkv_k_a,
        'rwkv_r_k': rwkv_r_k, 'rwkv_ln_g': rwkv_ln_g, 'rwkv_ln_b': rwkv_ln_b,
        'ret_gn_g': ret_gn_g, 'ret_gn_b': ret_gn_b, 'hgrn_lb_logits': hgrn_lb_logits,
        'hgrn_norm_g': hgrn_norm_g, 'gla_w1': gla_w1, 'gla_w2': gla_w2, 'gla_b': gla_b,
        'gla_norm_g': gla_norm_g, 'w_br': w_br, 'w_gate': w_gate, 'b_gate': b_gate, 'w_o': w_o,
        'ln1_g': ln1_g, 'ln1_b': ln1_b, 'ln2_g': ln2_g, 'ln2_b': ln2_b,
        'ffn_w_gate': ffn_w_gate, 'ffn_w_up': ffn_w_up, 'ffn_w_down': ffn_w_down,
        'router_w': router_w, 'router_b': router_b,
        'moe_w_gate': moe_w_gate, 'moe_w_up': moe_w_up, 'moe_w_down': moe_w_down,
    }
    layers = [_prep_layer(l, prm) for l in range(DEPTH)]
    consts = _mixer_consts()
    bp = x_prompt.shape[0]
    zero_hd = jnp.zeros((DEPTH, bp, N_HEADS, HEAD_DIM, HEAD_DIM), F32)
    zero_shift = jnp.zeros((DEPTH, bp, D_MODEL), F32)
    zero_gla = jnp.zeros((DEPTH, bp, N_HEADS, GLA_DK, HEAD_DIM), F32)
    prompt = _run_trunk(x_prompt, 0.0, zero_hd, zero_shift, zero_hd, zero_hd, zero_gla, prm, layers, consts)
    sample = _run_trunk(x_sample, float(PAST_LEN), state_rwkv, cache_shift, state_ret, state_hgrn,
                        state_gla, prm, layers, consts)
    y_p, p_rwkv, p_shift, p_ret, p_hgrn, p_gla = prompt
    y_s, s_rwkv, s_shift, s_ret, s_hgrn, s_gla = sample
    return (y_p, y_s, p_rwkv, p_shift, p_ret, p_hgrn, p_gla, s_rwkv, s_shift, s_ret, s_hgrn, s_gla)
```

```python
import functools
import math

import numpy as np
import jax
import jax.numpy as jnp
import jax.scipy.linalg
from jax import lax
from jax.experimental import pallas as pl
from jax.experimental.pallas import tpu as pltpu

F32 = jnp.float32
BF16 = jnp.bfloat16

D_MODEL = 1024
DEPTH = 2
PAST_LEN = 4096
CHUNK = 64
SUB = 16
N_BRANCH = 4
BR_WIDTH = D_MODEL // N_BRANCH
HEAD_DIM = 64
N_HEADS = BR_WIDTH // HEAD_DIM
GLA_DK = HEAD_DIM // 2
GLA_KW = N_HEADS * GLA_DK
GLA_GATE_RANK = 16
GLA_TAU = 16.0
RWKV_W_LORA = 32
RWKV_A_LORA = 32
RWKV_V_LORA = 16
RWKV_G_LORA = 64
RWKV_GN_EPS = 64e-5
ROPE_THETA = 10000.0
LN_EPS = 1e-5
D_FF = 2816
N_EXPERTS = 8
ALPHA = (2.0 * DEPTH) ** 0.25
RWKV_COLS = 3 * BR_WIDTH
IN_COLS = 3584
LORA_COLS = 256
AUX_COLS = 4 * BR_WIDTH + GLA_KW
SEQ_PAIR = 2

MOE_SUB = 256
MOE_WIN = 112
MOE_TAIL = 128

LANE = 128
SUBLANE = 8
BF16_ROWS = 16
LOG2_E = 1.4426950408889634

NN = (((1,), (0,)), ((), ()))
NT = (((1,), (1,)), ((), ()))
TN = (((0,), (0,)), ((), ()))


def _params(sem, vmem_mib):
    return pltpu.CompilerParams(dimension_semantics=sem, vmem_limit_bytes=vmem_mib * 1024 * 1024)


def _const_spec(shape):
    nd = len(shape)
    return pl.BlockSpec(shape, lambda *_: (0,) * nd, pipeline_mode=pl.Buffered(1))


def _dot(a, b, dims=NN):
    return lax.dot_general(a.astype(BF16), b.astype(BF16), dims, preferred_element_type=F32)


def _split(x):
    hi = x.astype(BF16)
    lo = (x - hi.astype(F32)).astype(BF16)
    return hi, lo


def _dot_exact_lhs(a_bf, x):
    hi, lo = _split(x)
    return (jnp.dot(a_bf, hi, preferred_element_type=F32)
            + jnp.dot(a_bf, lo, preferred_element_type=F32))


def _dot_exact_rhs(x, b_bf):
    hi, lo = _split(x)
    return (jnp.dot(hi, b_bf, preferred_element_type=F32)
            + jnp.dot(lo, b_bf, preferred_element_type=F32))


def _dot3(a, b, dims=NN):
    ah, al = _split(a)
    bh, bl = _split(b)
    d = functools.partial(lax.dot_general, dimension_numbers=dims, preferred_element_type=F32)
    return d(ah, bh) + (d(ah, bl) + d(al, bh))


def _sigmoid(x):
    return 1.0 / (1.0 + jnp.exp(-x))


def _softplus(x):
    return jnp.maximum(x, 0.0) + jnp.log(1.0 + jnp.exp(-jnp.abs(x)))


def _silu(x):
    return x * _sigmoid(x)


def _layer_norm_rows(y, g, b):
    mu = jnp.mean(y, axis=-1, keepdims=True)
    yc = y - mu
    var = jnp.mean(yc * yc, axis=-1, keepdims=True)
    return yc * lax.rsqrt(var + LN_EPS) * g + b


def _np_block_mask(rows_per_head, cols_per_head):
    r = np.arange(N_HEADS * rows_per_head)[:, None] // rows_per_head
    c = np.arange(N_HEADS * cols_per_head)[None, :] // cols_per_head
    return (r == c).astype(np.float32)


def _np_head_lanes(cols_per_head):
    m = np.zeros((SUBLANE, N_HEADS * cols_per_head), np.float32)
    for h in range(N_HEADS):
        m[h, h * cols_per_head:(h + 1) * cols_per_head] = 1.0
    return m


def _np_causal_side_by_side(strict):
    t = np.arange(CHUNK)[:, None]
    s = np.arange(N_HEADS * CHUNK)[None, :] % CHUNK
    return ((s < t) if strict else (s <= t)).astype(np.float32)


def _in_proj_kernel(x_ref, prev_ref, win_ref, wl1_ref, wl1mu_ref, w2_ref, bias_ref, p_ref, aux_ref):
    x = x_ref[...]
    row = lax.broadcasted_iota(jnp.int32, x.shape, 0)
    xx = jnp.where(row == 0, prev_ref[0], pltpu.roll(x, 1, axis=0)) - x
    xb = x.astype(BF16)
    h = (jnp.dot(xb, wl1_ref[...], preferred_element_type=F32)
         + jnp.dot(xx.astype(BF16), wl1mu_ref[...], preferred_element_type=F32))
    lane = lax.broadcasted_iota(jnp.int32, h.shape, 1)
    act = jnp.where(lane < RWKV_W_LORA, jnp.tanh(h), h)
    g_lo = RWKV_W_LORA + RWKV_A_LORA
    in_g = jnp.where(lane >= g_lo, jnp.where(lane < g_lo + RWKV_G_LORA, 1.0, 0.0), 0.0)
    act = jnp.where(in_g > 0.5, _sigmoid(h), act)
    z = jnp.dot(act.astype(BF16), w2_ref[...], preferred_element_type=F32) + bias_ref[...]
    p_ref[...] = jnp.dot(xb, win_ref[...], preferred_element_type=F32)
    bw = BR_WIDTH
    w_log = -_softplus(-z[:, 0:bw]) - 0.5
    aux_ref[:, 0:bw] = -jnp.exp(w_log)
    aux_ref[:, bw:2 * bw] = _sigmoid(z[:, bw:2 * bw])
    aux_ref[:, 2 * bw:3 * bw] = z[:, 2 * bw:3 * bw]
    aux_ref[:, 3 * bw:4 * bw] = _sigmoid(z[:, 3 * bw:4 * bw])
    zg = z[:, 4 * bw:]
    aux_ref[:, 4 * bw:] = (jnp.minimum(zg, 0.0) - jnp.log(1.0 + jnp.exp(-jnp.abs(zg)))) * (1.0 / GLA_TAU)


def _in_proj(x2, prev_rows, win, wl1, wl1mu, w2, bias, tm):
    n = x2.shape[0]
    return pl.pallas_call(
        _in_proj_kernel,
        grid=(n // tm,),
        in_specs=[
            pl.BlockSpec((tm, D_MODEL), lambda i: (i, 0)),
            pl.BlockSpec((1, 1, D_MODEL), lambda i: (i, 0, 0)),
            _const_spec(win.shape), _const_spec(wl1.shape), _const_spec(wl1mu.shape),
            _const_spec(w2.shape), _const_spec(bias.shape),
        ],
        out_specs=[pl.BlockSpec((tm, IN_COLS), lambda i: (i, 0)),
                   pl.BlockSpec((tm, AUX_COLS), lambda i: (i, 0))],
        out_shape=[jax.ShapeDtypeStruct((n, IN_COLS), F32), jax.ShapeDtypeStruct((n, AUX_COLS), F32)],
        compiler_params=_params(("arbitrary",), 56),
        name="in_proj",
    )(x2, prev_rows, win, wl1, wl1mu, w2, bias)


def _rows_matmul_kernel(x_ref, w_ref, o_ref):
    o_ref[...] = jnp.dot(x_ref[...].astype(BF16), w_ref[...], preferred_element_type=F32)


def _rows_matmul(x, w):
    return pl.pallas_call(
        _rows_matmul_kernel,
        out_shape=jax.ShapeDtypeStruct((x.shape[0], w.shape[1]), F32),
        name="rows_matmul",
    )(x, w)


def _stack_heads(x, head_lanes):
    xb = x.astype(BF16)
    return jnp.concatenate([xb * head_lanes[h:h + 1, :].astype(BF16) for h in range(N_HEADS)], axis=0)


def _head_sum(x, ones_bf):
    return _dot_exact_rhs(x, ones_bf)


def _head_layer_norm(y, ones_bf, g, b, eps):
    inv = 1.0 / HEAD_DIM
    mu = _head_sum(y, ones_bf) * inv
    yc = y - mu
    var = _head_sum(yc * yc, ones_bf) * inv
    return yc * lax.rsqrt(var + eps) * g + b


def _head_rms_norm(y, ones_bf, g):
    ms = _head_sum(y * y, ones_bf) * (1.0 / HEAD_DIM)
    return y * lax.rsqrt(ms + 1e-6) * g


def _pairwise_block(q, k, v, b2, pair_ones_bf):
    parts = []
    for j in range(SUB):
        lo = (j // SUBLANE) * SUBLANE
        p = q[lo:] * jnp.exp2(b2[lo:] - b2[j:j + 1]) * k[j:j + 1]
        if j % SUBLANE:
            rid = lax.broadcasted_iota(jnp.int32, p.shape, 0) + lo
            p = jnp.where(rid >= j, p, 0.0)
        parts.append(p)
    att = jnp.dot(jnp.concatenate(parts, axis=0).astype(BF16), pair_ones_bf, preferred_element_type=F32)
    outs = []
    off = 0
    for g in range(SUB // SUBLANE):
        rows = SUB - g * SUBLANE
        acc = None
        for j in range(g * SUBLANE, (g + 1) * SUBLANE):
            term = att[off:off + rows] * v[j:j + 1]
            acc = term if acc is None else acc + term
            off += rows
        if g:
            acc = jnp.concatenate([jnp.zeros((g * SUBLANE, v.shape[1]), F32), acc], axis=0)
        outs.append(acc)
    total = outs[0]
    for extra in outs[1:]:
        total = total + extra
    return total


def _gla_state_free(tiles, tril_bf, pair_ones_bf, st_mask, lanes_k, lanes_v):
    n = range(len(tiles))
    q, k, v, glog = ([t[i] for t in tiles] for i in range(4))
    b = [_dot_exact_lhs(tril_bf, glog[i]) for i in n]
    b2 = [b[i] * LOG2_E for i in n]
    blocks = [[] for _ in n]
    for blk in range(CHUNK // SUB):
        r0 = blk * SUB
        sl = slice(r0, r0 + SUB)
        o_blk = [_pairwise_block(q[i][sl], k[i][sl], v[i][sl], b2[i][sl], pair_ones_bf) for i in n]
        if blk:
            c0 = [b[i][r0 - 1:r0] for i in n]
            q_t = [q[i][sl] * jnp.exp(b[i][sl] - c0[i]) for i in n]
            k_t = [k[i][:r0] * jnp.exp(c0[i] - b[i][:r0]) for i in n]
            att = [_dot(q_t[i], _stack_heads(k_t[i], lanes_k), NT) for i in n]
            o_blk = [o_blk[i] + _dot(att[i], _stack_heads(v[i][:r0], lanes_v)) for i in n]
        for i in n:
            blocks[i].append(o_blk[i])
    b_last = [b[i][CHUNK - 1:CHUNK, :] for i in n]
    upd = [st_mask * _dot(v[i], k[i] * jnp.exp(b_last[i] - b[i]), TN) for i in n]
    return [(jnp.concatenate(blocks[i], axis=0), q[i] * jnp.exp(b[i]), upd[i], jnp.exp(b_last[i]))
            for i in n]


def _rwkv_state_free(tiles, tril_bf, hl, strict, incl, eye):
    n = range(len(tiles))
    r, k, v, kk, bv, lw = ([t[i] for t in tiles] for i in range(6))
    l = [_dot_exact_lhs(tril_bf, lw[i]) for i in n]
    l_last = [l[i][CHUNK - 1:CHUNK, :] for i in n]
    e_neg = [jnp.exp(-l[i]) for i in n]
    lhs = [jnp.concatenate([kk[i] * jnp.exp(l[i] - lw[i]), r[i] * jnp.exp(l[i])], axis=0) for i in n]
    rhs = [jnp.concatenate([_stack_heads(k[i] * e_neg[i], hl), _stack_heads(bv[i] * e_neg[i], hl)], axis=0)
           for i in n]
    amat = [_dot(lhs[i], rhs[i], NT) for i in n]
    w = N_HEADS * CHUNK
    a_ab = [amat[i][:CHUNK, w:] * strict for i in n]
    a_vk = [jnp.concatenate([amat[i][:CHUNK, :w] * strict, amat[i][CHUNK:, :w] * incl], axis=0) for i in n]
    a_rb = [amat[i][CHUNK:, w:] * incl for i in n]
    x = [eye + a_ab[i] for i in n]
    m = a_ab
    for _ in range(int(math.log2(CHUNK)) - 1):
        m_st = [_stack_heads(m[i], hl) for i in n]
        m = [_dot(m[i], m_st[i]) for i in n]
        m_st = [_stack_heads(m[i], hl) for i in n]
        x = [x[i] + _dot(x[i], m_st[i]) for i in n]
    from_v = [_dot(a_vk[i], _stack_heads(v[i], hl)) for i in n]
    e_end = [jnp.exp(l_last[i] - l[i]) for i in n]
    upd_v = [_dot(v[i], k[i] * e_end[i], TN) for i in n]
    b_end = [bv[i] * e_end[i] for i in n]
    st_dec = [jnp.exp(l_last[i]) for i in n]
    return [(lhs[i], x[i], a_rb[i], from_v[i], upd_v[i], b_end[i], st_dec[i]) for i in n]


def _rwkv_state_step(parts, sts, bdm, hl):
    n = range(len(parts))
    lhs, x, a_rb, from_v, upd_v, b_end, st_dec = ([p[i] for p in parts] for i in range(7))
    from_state = [_dot(lhs[i], sts[i], NT) for i in n]
    u = [_dot(x[i], _stack_heads(from_state[i][:CHUNK] + from_v[i][:CHUNK], hl)) for i in n]
    upd = [upd_v[i] + _dot(u[i], b_end[i], TN) for i in n]
    new = [sts[i] * st_dec[i] + upd[i] * bdm for i in n]
    y = [from_state[i][CHUNK:] + from_v[i][CHUNK:] + _dot(a_rb[i], _stack_heads(u[i], hl)) for i in n]
    return y, new


def _rwkv_kernel(has_vres, *refs):
    if has_vres:
        (rkv_ref, lw_ref, a_ref, g_ref, vg_ref, vf_ref, last_ref, st0_ref, prm_ref, tril_ref, bdm_ref,
         hl_ref, strict_ref, incl_ref, eye_ref, o_ref, st_out_ref, st_sc, prev_sc) = refs
    else:
        (rkv_ref, lw_ref, a_ref, g_ref, last_ref, st0_ref, prm_ref, tril_ref, bdm_ref,
         hl_ref, strict_ref, incl_ref, eye_ref, o_ref, v_out_ref, st_out_ref, st_sc, prev_sc) = refs
    tb = pl.program_id(1)

    @pl.when(tb == 0)
    def _():
        st_sc[...] = st0_ref[...]
        prev_sc[...] = last_ref[...]

    bw = BR_WIDTH
    k_k = prm_ref[1:2, 0:bw]
    k_a = prm_ref[2:3, 0:bw]
    r_k = prm_ref[3:4, 0:bw]
    ln_g = prm_ref[4:5, 0:bw]
    ln_b = prm_ref[5:6, 0:bw]
    bdm = bdm_ref[...]
    ones_bf = bdm.astype(BF16)
    hl = hl_ref[...]
    tril_bf = tril_ref[...]
    strict = strict_ref[...]
    incl = incl_ref[...]
    eye = eye_ref[...]
    n_seq = rkv_ref.shape[0]
    tt = rkv_ref.shape[1]
    seqs = []
    for s in range(n_seq):
        rkv = rkv_ref[s]
        row = lax.broadcasted_iota(jnp.int32, rkv.shape, 0)
        prev = jnp.where(row == 0, prev_sc[s], pltpu.roll(rkv, 1, axis=0))
        prev_sc[s] = rkv[tt - 1:tt, :]
        mixed = rkv + (prev - rkv) * prm_ref[0:1, :]
        r = mixed[:, 0:bw]
        k = mixed[:, bw:2 * bw]
        v = mixed[:, 2 * bw:]
        a = a_ref[s]
        if has_vres:
            v = v + (vf_ref[s] - v) * vg_ref[s]
        else:
            v_out_ref[s] = v
        kk = k * k_k
        kk = kk * lax.rsqrt(jnp.maximum(_head_sum(kk * kk, ones_bf), 1e-24))
        k = k * (1.0 + (a - 1.0) * k_a)
        seqs.append((r, k, v, kk, -(kk * a), lw_ref[s]))
    n_chunks = tt // CHUNK
    tiles = [tuple(z[c * CHUNK:(c + 1) * CHUNK] for z in seqs[s])
             for c in range(n_chunks) for s in range(n_seq)]
    parts = _rwkv_state_free(tiles, tril_bf, hl, strict, incl, eye)
    sts = [st_sc[s] for s in range(n_seq)]
    ys = [[] for _ in range(n_seq)]
    for c in range(n_chunks):
        y_c, sts = _rwkv_state_step(parts[c * n_seq:(c + 1) * n_seq], sts, bdm, hl)
        for s in range(n_seq):
            ys[s].append(y_c[s])
    for s in range(n_seq):
        r, k, v = seqs[s][0:3]
        st_sc[s] = sts[s]
        y = ys[s][0] if len(ys[s]) == 1 else jnp.concatenate(ys[s], axis=0)
        y = _head_layer_norm(y, ones_bf, ln_g, ln_b, RWKV_GN_EPS)
        bonus = _head_sum(r * k * r_k, ones_bf) * v
        o_ref[s] = (y + bonus) * g_ref[s]

    @pl.when(tb == pl.num_programs(1) - 1)
    def _():
        for s in range(n_seq):
            st_out_ref[s] = sts[s]


def _rwkv_mixer(p3, aux3, v_first, rkv_last, st0, prm, consts, tt):
    b, t, _ = p3.shape
    bw = BR_WIDTH
    pb = SEQ_PAIR
    has_vres = v_first is not None
    tok = lambda j: pl.BlockSpec((pb, tt, bw), lambda bi, ti, j=j: (bi, ti, j))
    in_specs = [pl.BlockSpec((pb, tt, RWKV_COLS), lambda bi, ti: (bi, ti, 0)),
                tok(0), tok(1), tok(2)]
    args = [p3, aux3, aux3, aux3]
    if has_vres:
        in_specs += [tok(3), pl.BlockSpec((pb, tt, bw), lambda bi, ti: (bi, ti, 0))]
        args += [aux3, v_first]
    in_specs += [pl.BlockSpec((pb, 1, RWKV_COLS), lambda bi, ti: (bi, 0, 0)),
                 pl.BlockSpec((pb, bw, bw), lambda bi, ti: (bi, 0, 0)),
                 _const_spec(prm.shape)]
    args += [rkv_last, st0, prm]
    for name in ("tril", "bdm", "head_lanes", "strict", "incl", "eye"):
        in_specs.append(_const_spec(consts[name].shape))
        args.append(consts[name])
    seq = pl.BlockSpec((pb, tt, bw), lambda bi, ti: (bi, ti, 0))
    st_spec = pl.BlockSpec((pb, bw, bw), lambda bi, ti: (bi, 0, 0))
    seq_shape = jax.ShapeDtypeStruct((b, t, bw), F32)
    st_shape = jax.ShapeDtypeStruct((b, bw, bw), F32)
    if has_vres:
        out_specs, out_shape = [seq, st_spec], [seq_shape, st_shape]
    else:
        out_specs, out_shape = [seq, seq, st_spec], [seq_shape, seq_shape, st_shape]
    return pl.pallas_call(
        functools.partial(_rwkv_kernel, has_vres),
        grid=(b // pb, t // tt),
        in_specs=in_specs, out_specs=out_specs, out_shape=out_shape,
        scratch_shapes=[pltpu.VMEM((pb, bw, bw), F32), pltpu.VMEM((pb, 1, RWKV_COLS), F32)],
        compiler_params=_params(("arbitrary", "arbitrary"), 48),
        name="rwkv_mixer",
    )(*args)


def _rot_half(z):
    w = z.shape[1]
    half = HEAD_DIM // 2
    lane = lax.broadcasted_iota(jnp.int32, z.shape, 1)
    first = (lane % HEAD_DIM) < half
    return jnp.where(first, pltpu.roll(z, w - half, axis=1), pltpu.roll(z, half, axis=1))


def _ret_kernel(q_ref, k_ref, v_ref, g_ref, cos_ref, sin_ref, st0_ref, prm_ref, dec_ref, bdm_ref, hl_ref,
                o_ref, st_out_ref, st_sc):
    tb = pl.program_id(1)

    @pl.when(tb == 0)
    def _():
        st_sc[...] = st0_ref[...]

    cos = cos_ref[...]
    sin = sin_ref[...]
    bdm = bdm_ref[...]
    hl = hl_ref[...]
    ones_bf = bdm.astype(BF16)
    q_dec = dec_ref[0:CHUNK, :]
    k_dec = dec_ref[CHUNK:2 * CHUNK, :]
    d_mat = dec_ref[2 * CHUNK:3 * CHUNK, :]
    s_dec = dec_ref[3 * CHUNK:3 * CHUNK + 1, :]
    n_seq = q_ref.shape[0]
    tt = q_ref.shape[1]
    qs, ks, vs = [], [], []
    for s in range(n_seq):
        q = q_ref[s]
        k = k_ref[s]
        qs.append(q * cos + _rot_half(q) * sin)
        ks.append((k * cos + _rot_half(k) * sin) * (HEAD_DIM ** -0.5))
        vs.append(v_ref[s])
    sts = [st_sc[s] for s in range(n_seq)]
    outs = [[] for _ in range(n_seq)]
    for c in range(tt // CHUNK):
        sl = slice(c * CHUNK, (c + 1) * CHUNK)
        for s in range(n_seq):
            qc, kc, vc = qs[s][sl], ks[s][sl], vs[s][sl]
            att = _dot(qc, _stack_heads(kc, hl), NT) * d_mat
            outs[s].append(_dot(qc * q_dec, sts[s], NT) + _dot(att, _stack_heads(vc, hl)))
            sts[s] = sts[s] * s_dec + bdm * _dot(vc, kc * k_dec, TN)
    for s in range(n_seq):
        st_sc[s] = sts[s]
        o = outs[s][0] if len(outs[s]) == 1 else jnp.concatenate(outs[s], axis=0)
        o = _head_layer_norm(o, ones_bf, prm_ref[0:1, :], prm_ref[1:2, :], LN_EPS)
        o_ref[s] = o * _silu(g_ref[s])

    @pl.when(tb == pl.num_programs(1) - 1)
    def _():
        for s in range(n_seq):
            st_out_ref[s] = sts[s]


def _ret_mixer(p3, cos_t, sin_t, st0, prm, consts, tt):
    b, t, _ = p3.shape
    bw = BR_WIDTH
    pb = SEQ_PAIR
    base = RWKV_COLS // bw
    tok = lambda j: pl.BlockSpec((pb, tt, bw), lambda bi, ti, j=j: (bi, ti, base + j))
    tab = pl.BlockSpec((tt, bw), lambda bi, ti: (ti, 0))
    st_spec = pl.BlockSpec((pb, bw, bw), lambda bi, ti: (bi, 0, 0))
    dec, bdm, hl = consts["ret_dec"], consts["bdm"], consts["head_lanes"]
    return pl.pallas_call(
        _ret_kernel,
        grid=(b // pb, t // tt),
        in_specs=[tok(0), tok(1), tok(2), tok(3), tab, tab, st_spec, _const_spec(prm.shape),
                  _const_spec(dec.shape), _const_spec(bdm.shape), _const_spec(hl.shape)],
        out_specs=[pl.BlockSpec((pb, tt, bw), lambda bi, ti: (bi, ti, 0)), st_spec],
        out_shape=[jax.ShapeDtypeStruct((b, t, bw), F32), jax.ShapeDtypeStruct((b, bw, bw), F32)],
        scratch_shapes=[pltpu.VMEM((pb, bw, bw), F32)],
        compiler_params=_params(("arbitrary", "arbitrary"), 40),
        name="ret_mixer",
    )(p3, p3, p3, p3, cos_t, sin_t, st0, prm, dec, bdm, hl)


def _gated_mixer_tail(seqs, norm_g, st_sc, st_out_ref, o_ref, tril_bf, pair_ones_bf, st_mask,
                      lanes_k, lanes_v, ones_v_bf):
    n_seq = len(seqs)
    tt = seqs[0][0].shape[0]
    n_chunks = tt // CHUNK
    tiles = [tuple(z[c * CHUNK:(c + 1) * CHUNK] for z in seqs[s][0:4])
             for c in range(n_chunks) for s in range(n_seq)]
    parts = _gla_state_free(tiles, tril_bf, pair_ones_bf, st_mask, lanes_k, lanes_v)
    sts = [st_sc[s] for s in range(n_seq)]
    outs = [[] for _ in range(n_seq)]
    for c in range(n_chunks):
        for s in range(n_seq):
            o_intra, q_dec, upd, dec = parts[c * n_seq + s]
            outs[s].append(o_intra + _dot(q_dec, sts[s], NT))
            sts[s] = sts[s] * dec + upd
    for s in range(n_seq):
        st_sc[s] = sts[s]
        o = outs[s][0] if len(outs[s]) == 1 else jnp.concatenate(outs[s], axis=0)
        o_ref[s] = _head_rms_norm(o, ones_v_bf, norm_g) * _silu(seqs[s][4])

    @pl.when(pl.program_id(1) == pl.num_programs(1) - 1)
    def _():
        for s in range(n_seq):
            st_out_ref[s] = sts[s]


def _hgrn_kernel(layer, q_ref, f_ref, i_ref, g_ref, st0_ref, lbl_ref, ng_ref, tril_ref, bdm_ref, hl_ref,
                 o_ref, st_out_ref, st_sc):
    @pl.when(pl.program_id(1) == 0)
    def _():
        st_sc[...] = st0_ref[...]

    logits = lbl_ref[...]
    ex = jnp.exp(logits - jnp.max(logits, axis=0, keepdims=True))
    sm = ex / jnp.sum(ex, axis=0, keepdims=True)
    lb = jnp.zeros((1, BR_WIDTH), F32)
    for d in range(1, layer + 1):
        lb = lb + sm[d:d + 1, :]
    seqs = []
    for s in range(q_ref.shape[0]):
        fz = f_ref[s]
        f = lb + (1.0 - lb) * _sigmoid(fz)
        k = (1.0 - lb) * _sigmoid(-fz)
        seqs.append((_silu(q_ref[s]), k, i_ref[s], jnp.log(f), g_ref[s]))
    bdm = bdm_ref[...]
    ones_bf = bdm.astype(BF16)
    hl = hl_ref[...]
    _gated_mixer_tail(seqs, ng_ref[...], st_sc, st_out_ref, o_ref, tril_ref[...], ones_bf, bdm, hl, hl,
                      ones_bf)


def _hgrn_mixer(p3, st0, lb_logits, norm_g, consts, layer, tt):
    b, t, _ = p3.shape
    bw = BR_WIDTH
    pb = SEQ_PAIR
    base = (RWKV_COLS + 4 * bw) // bw
    tok = lambda j: pl.BlockSpec((pb, tt, bw), lambda bi, ti, j=j: (bi, ti, base + j))
    st_spec = pl.BlockSpec((pb, bw, bw), lambda bi, ti: (bi, 0, 0))
    tril, bdm, hl = consts["tril"], consts["bdm"], consts["head_lanes"]
    return pl.pallas_call(
        functools.partial(_hgrn_kernel, layer),
        grid=(b // pb, t // tt),
        in_specs=[tok(0), tok(1), tok(2), tok(3), st_spec, _const_spec(lb_logits.shape),
                  _const_spec(norm_g.shape), _const_spec(tril.shape), _const_spec(bdm.shape),
                  _const_spec(hl.shape)],
        out_specs=[pl.BlockSpec((pb, tt, bw), lambda bi, ti: (bi, ti, 0)), st_spec],
        out_shape=[jax.ShapeDtypeStruct((b, t, bw), F32), jax.ShapeDtypeStruct((b, bw, bw), F32)],
        scratch_shapes=[pltpu.VMEM((pb, bw, bw), F32)],
        compiler_params=_params(("arbitrary", "arbitrary"), 48),
        name="hgrn_mixer",
    )(p3, p3, p3, p3, st0, lb_logits, norm_g, tril, bdm, hl)


def _gla_kernel(q_ref, k_ref, v_ref, g_ref, la_ref, st0_ref, ng_ref, tril_ref, pair_ref, mask_ref,
                bdm_ref, hlk_ref, hlv_ref, o_ref, st_out_ref, st_sc):
    @pl.when(pl.program_id(1) == 0)
    def _():
        st_sc[...] = st0_ref[...]

    seqs = [(q_ref[s] * (GLA_DK ** -0.5), k_ref[s], v_ref[s], la_ref[s], g_ref[s])
            for s in range(q_ref.shape[0])]
    _gated_mixer_tail(seqs, ng_ref[...], st_sc, st_out_ref, o_ref, tril_ref[...], pair_ref[...],
                      mask_ref[...], hlk_ref[...], hlv_ref[...], bdm_ref[...].astype(BF16))


def _gla_mixer(p3, aux3, st0, norm_g, consts, tt):
    b, t, _ = p3.shape
    bw, kw = BR_WIDTH, GLA_KW
    pb = SEQ_PAIR
    gla0 = RWKV_COLS + 8 * bw
    st_spec = pl.BlockSpec((pb, bw, kw), lambda bi, ti: (bi, 0, 0))
    names = ("tril", "gla_pair", "gla_mask", "bdm", "gla_head_lanes", "head_lanes")
    return pl.pallas_call(
        _gla_kernel,
        grid=(b // pb, t // tt),
        in_specs=[pl.BlockSpec((pb, tt, kw), lambda bi, ti: (bi, ti, gla0 // kw)),
                  pl.BlockSpec((pb, tt, kw), lambda bi, ti: (bi, ti, gla0 // kw + 1)),
                  pl.BlockSpec((pb, tt, bw), lambda bi, ti: (bi, ti, (gla0 + 2 * kw) // bw)),
                  pl.BlockSpec((pb, tt, bw), lambda bi, ti: (bi, ti, (gla0 + 2 * kw) // bw + 1)),
                  pl.BlockSpec((pb, tt, kw), lambda bi, ti: (bi, ti, 4 * bw // kw)),
                  st_spec, _const_spec(norm_g.shape)] + [_const_spec(consts[nm].shape) for nm in names],
        out_specs=[pl.BlockSpec((pb, tt, bw), lambda bi, ti: (bi, ti, 0)), st_spec],
        out_shape=[jax.ShapeDtypeStruct((b, t, bw), F32), jax.ShapeDtypeStruct((b, bw, kw), F32)],
        scratch_shapes=[pltpu.VMEM((pb, bw, kw), F32)],
        compiler_params=_params(("arbitrary", "arbitrary"), 48),
        name="gla_mixer",
    )(p3, p3, p3, p3, aux3, st0, norm_g, *[consts[nm] for nm in names])


def _merge_kernel(emit_bf16, x_ref, o0_ref, o1_ref, o2_ref, o3_ref, wg_ref, bg_ref, wbr_ref, wo_ref,
                  ln_ref, *out_refs):
    branches = (o0_ref, o1_ref, o2_ref, o3_ref)
    half = x_ref.shape[0] // 2
    rows = [slice(0, half), slice(half, 2 * half)]
    x = [x_ref[r, :] for r in rows]
    xb = [v.astype(BF16) for v in x]
    merged = [None, None]
    for m in range(N_BRANCH):
        pre = [jnp.dot(xb[h], wg_ref[m], preferred_element_type=F32) for h in range(2)]
        proj = [jnp.dot(branches[m][rows[h], :].astype(BF16), wbr_ref[m], preferred_element_type=F32)
                for h in range(2)]
        for h in range(2):
            term = _sigmoid(pre[h] + bg_ref[m:m + 1, :]) * proj[h]
            merged[h] = term if merged[h] is None else merged[h] + term
    y = [ALPHA * x[h] + jnp.dot(merged[h].astype(BF16), wo_ref[...], preferred_element_type=F32)
         for h in range(2)]
    for h in range(2):
        yn = _layer_norm_rows(y[h], ln_ref[0:1, :], ln_ref[1:2, :])
        out_refs[0][rows[h], :] = yn
        if emit_bf16:
            out_refs[1][rows[h], :] = yn.astype(BF16)


def _merge(x2, outs, wg, bg, wbr, wo, ln, tm, emit_bf16):
    n = x2.shape[0]
    row = pl.BlockSpec((tm, D_MODEL), lambda i: (i, 0))
    br = pl.BlockSpec((tm, BR_WIDTH), lambda i: (i, 0))
    out_specs = [row]
    out_shape = [jax.ShapeDtypeStruct((n, D_MODEL), F32)]
    if emit_bf16:
        out_specs.append(row)
        out_shape.append(jax.ShapeDtypeStruct((n, D_MODEL), BF16))
    return pl.pallas_call(
        functools.partial(_merge_kernel, emit_bf16),
        grid=(n // tm,),
        in_specs=[row, br, br, br, br, _const_spec(wg.shape), _const_spec(bg.shape),
                  _const_spec(wbr.shape), _const_spec(wo.shape), _const_spec(ln.shape)],
        out_specs=out_specs, out_shape=out_shape,
        compiler_params=_params(("arbitrary",), 56),
        name="merge",
    )(x2, *outs, wg, bg, wbr, wo, ln)


FF_SPLIT = 2
FF_PART = D_FF // FF_SPLIT
FFN_PARTS = (1024, 1024, 768)


def _ffn_kernel(x_ref, wg_ref, wu_ref, wd_ref, ln_ref, o_ref):
    x = x_ref[...]
    xb = x.astype(BF16)
    acc = ALPHA * x
    lo = 0
    for width in FFN_PARTS:
        cs = slice(lo, lo + width)
        lo += width
        h = (_silu(jnp.dot(xb, wg_ref[:, cs], preferred_element_type=F32))
             * jnp.dot(xb, wu_ref[:, cs], preferred_element_type=F32))
        acc = acc + jnp.dot(h.astype(BF16), wd_ref[cs, :], preferred_element_type=F32)
    o_ref[...] = _layer_norm_rows(acc, ln_ref[0:1, :], ln_ref[1:2, :])


def _ffn(x2, wg, wu, wd, ln, tm):
    n = x2.shape[0]
    row = pl.BlockSpec((tm, D_MODEL), lambda i: (i, 0))
    return pl.pallas_call(
        _ffn_kernel,
        grid=(n // tm,),
        in_specs=[row, _const_spec(wg.shape), _const_spec(wu.shape), _const_spec(wd.shape),
                  _const_spec(ln.shape)],
        out_specs=row,
        out_shape=jax.ShapeDtypeStruct((n, D_MODEL), F32),
        compiler_params=_params(("arbitrary",), 60),
        name="ffn",
    )(x2, wg, wu, wd, ln)


def _router_kernel(x_ref, wr_ref, br_ref, tril_ref, rank_ref, wsel_ref, cnt_ref):
    logits = _dot3(x_ref[...], wr_ref[...]) + br_ref[...]
    lane = lax.broadcasted_iota(jnp.int32, logits.shape, 1)
    neg = jnp.float32(-jnp.inf)
    logits = jnp.where(lane < N_EXPERTS, logits, neg)
    m1 = jnp.max(logits, axis=1, keepdims=True)
    lane_f = lane.astype(F32)
    i1 = jnp.min(jnp.where(logits == m1, lane_f, float(LANE)), axis=1, keepdims=True)
    first = lane_f == i1
    rest = jnp.where(first, neg, logits)
    m2 = jnp.max(rest, axis=1, keepdims=True)
    i2 = jnp.min(jnp.where(rest == m2, lane_f, float(LANE)), axis=1, keepdims=True)
    second = lane_f == i2
    e = jnp.exp(m2 - m1)
    w1 = 1.0 / (1.0 + e)
    w2 = e / (1.0 + e)
    sel = jnp.where(first, 1.0, jnp.where(second, 1.0, 0.0))
    wsel_ref[...] = jnp.where(first, w1, jnp.where(second, w2, 0.0))
    sel_bf = sel.astype(BF16)
    rank = jnp.dot(tril_ref[...], sel_bf, preferred_element_type=F32)
    rank_ref[...] = jnp.where(sel > 0.5, rank, -1.0)
    ones = jnp.ones((SUBLANE, MOE_SUB), BF16)
    row = lax.broadcasted_iota(jnp.int32, (SUBLANE, LANE), 0)
    cnt = jnp.zeros((SUBLANE, LANE), F32)
    for s in range(sel.shape[0] // MOE_SUB):
        part = jnp.dot(ones, sel_bf[s * MOE_SUB:(s + 1) * MOE_SUB, :], preferred_element_type=F32)
        cnt = jnp.where(row == s, part, cnt)
    cnt_ref[0] = cnt.astype(jnp.int32)


def _router(x2, wr, br, tril, tm):
    n = x2.shape[0]
    nt = n // tm
    col = pl.BlockSpec((tm, LANE), lambda i: (i, 0))
    return pl.pallas_call(
        _router_kernel,
        grid=(nt,),
        in_specs=[pl.BlockSpec((tm, D_MODEL), lambda i: (i, 0)), _const_spec(wr.shape),
                  _const_spec(br.shape), _const_spec(tril.shape)],
        out_specs=[col, col, pl.BlockSpec((1, SUBLANE, LANE), lambda i: (i, 0, 0))],
        out_shape=[jax.ShapeDtypeStruct((n, LANE), F32), jax.ShapeDtypeStruct((n, LANE), F32),
                   jax.ShapeDtypeStruct((nt, SUBLANE, LANE), jnp.int32)],
        compiler_params=_params(("arbitrary",), 40),
        name="router",
    )(x2, wr, br, tril)


def _moe_kernel(rows, cnt_ref, off_ref, end_ref, x_ref, xb_ref, rrow_ref, rank_ref, wsel_ref, wg_ref, wu_ref,
                wd_ref, ln_ref, o_ref, xg_sc, yb_sc):
    i = pl.program_id(0)
    e = pl.program_id(1)
    c = pl.program_id(2)
    n_e = pl.num_programs(1)
    n_c = pl.num_programs(2)
    cnt = cnt_ref[i * N_EXPERTS + e]
    n_blk = (cnt + rows - 1) // rows
    tm = xb_ref.shape[0]
    n_sub = tm // MOE_SUB

    def windows(s):
        idx = (i * n_sub + s) * N_EXPERTS + e
        off = off_ref[idx]
        end = end_ref[idx]
        a0 = (off // BF16_ROWS) * BF16_ROWS
        n_win = jnp.where(end > off, (end - a0 + MOE_WIN - 1) // MOE_WIN, 0)
        return a0, n_win

    @pl.when((e == 0) & (c == 0))
    def _():
        o_ref[...] = ALPHA * x_ref[...]

    toks = [slice(s * MOE_SUB, (s + 1) * MOE_SUB) for s in range(n_sub)]

    def gather_piece(s, r0):
        slot = (lax.broadcasted_iota(jnp.int32, (MOE_WIN, MOE_SUB), 0) + r0).astype(F32)
        onehot = jnp.where(rrow_ref[0, :, toks[s]] == slot, 1.0, 0.0).astype(BF16)
        return jnp.dot(onehot, xb_ref[toks[s], :], preferred_element_type=F32)

    def gather_add(r0, piece):
        cur = xg_sc[pl.ds(r0, MOE_WIN), :].astype(F32)
        xg_sc[pl.ds(r0, MOE_WIN), :] = (cur + piece).astype(BF16)

    @pl.when(c == 0)
    def _():
        xg_sc[...] = jnp.zeros(xg_sc.shape, xg_sc.dtype)
        wins = [windows(s) for s in range(n_sub)]
        starts = [pl.multiple_of(a0, BF16_ROWS) for a0, _ in wins]
        pieces = [gather_piece(s, starts[s]) for s in range(n_sub)]
        for s in range(n_sub):
            gather_add(starts[s], pieces[s])
        for s in range(n_sub):
            a0, n_win = wins[s]

            def more(wi, carry, s=s, a0=a0):
                r0 = pl.multiple_of(a0 + wi * MOE_WIN, BF16_ROWS)
                gather_add(r0, gather_piece(s, r0))
                return carry

            lax.fori_loop(1, n_win, more, 0)

    def expert(blk, carry):
        r0 = pl.multiple_of(blk * rows, SUBLANE)
        xg = xg_sc[pl.ds(r0, rows), :]
        h = (_silu(jnp.dot(xg, wg_ref[0], preferred_element_type=F32))
             * jnp.dot(xg, wu_ref[0], preferred_element_type=F32))
        yb = jnp.dot(h.astype(BF16), wd_ref[0], preferred_element_type=F32)

        @pl.when(c == 0)
        def _():
            yb_sc[pl.ds(r0, rows), :] = yb

        @pl.when(c > 0)
        def _():
            yb_sc[pl.ds(r0, rows), :] = yb_sc[pl.ds(r0, rows), :] + yb
        return carry

    lax.fori_loop(0, n_blk, expert, 0)

    @pl.when(c == 0)
    def _():
        tail = pl.multiple_of(n_blk * rows, BF16_ROWS)
        yb_sc[pl.ds(tail, MOE_TAIL), :] = jnp.zeros((MOE_TAIL, D_MODEL), F32)

    @pl.when(c == n_c - 1)
    def _():
        mine = lax.broadcasted_iota(jnp.int32, (tm, LANE), 1) == e
        rank_col = jnp.sum(jnp.where(mine, rank_ref[...], 0.0), axis=1, keepdims=True)
        w_col = jnp.sum(jnp.where(mine, wsel_ref[...], 0.0), axis=1, keepdims=True)

        def scatter_piece(s, r0):
            slot = (lax.broadcasted_iota(jnp.int32, (MOE_SUB, MOE_WIN), 1) + r0).astype(F32)
            onehot = jnp.where(rank_col[toks[s]] == slot, 1.0, 0.0).astype(BF16)
            return jnp.dot(onehot, yb_sc[pl.ds(r0, MOE_WIN), :].astype(BF16), preferred_element_type=F32)

        wins = [windows(s) for s in range(n_sub)]
        starts = [pl.multiple_of(a0, BF16_ROWS) for a0, _ in wins]
        backs = [scatter_piece(s, starts[s]) for s in range(n_sub)]
        for s in range(n_sub):
            o_ref[toks[s], :] = o_ref[toks[s], :] + w_col[toks[s]] * backs[s]
        for s in range(n_sub):
            a0, n_win = wins[s]

            def more(wi, carry, s=s, a0=a0):
                r0 = pl.multiple_of(a0 + wi * MOE_WIN, BF16_ROWS)
                o_ref[toks[s], :] = o_ref[toks[s], :] + w_col[toks[s]] * scatter_piece(s, r0)
                return carry

            lax.fori_loop(1, n_win, more, 0)

    @pl.when((e == n_e - 1) & (c == n_c - 1))
    def _():
        o_ref[...] = _layer_norm_rows(o_ref[...], ln_ref[0:1, :], ln_ref[1:2, :])


def _moe(x2, xb2, counts, offs, ends, rank_row, rank, wsel, wg, wu, wd, ln, tm, rows):
    n = x2.shape[0]
    nt = n // tm
    cap = -(-tm // rows) * rows + MOE_TAIL
    tile = lambda i, e, c, *_: (i, 0)
    grid_spec = pltpu.PrefetchScalarGridSpec(
        num_scalar_prefetch=3,
        grid=(nt, N_EXPERTS, FF_SPLIT),
        in_specs=[
            pl.BlockSpec((tm, D_MODEL), tile),
            pl.BlockSpec((tm, D_MODEL), tile),
            pl.BlockSpec((1, 1, tm), lambda i, e, c, *_: (e, 0, i)),
            pl.BlockSpec((tm, LANE), tile),
            pl.BlockSpec((tm, LANE), tile),
            pl.BlockSpec((1, D_MODEL, FF_PART), lambda i, e, c, *_: (e, 0, c)),
            pl.BlockSpec((1, D_MODEL, FF_PART), lambda i, e, c, *_: (e, 0, c)),
            pl.BlockSpec((1, FF_PART, D_MODEL), lambda i, e, c, *_: (e, c, 0)),
            pl.BlockSpec((2, D_MODEL), lambda i, e, c, *_: (0, 0)),
        ],
        out_specs=pl.BlockSpec((tm, D_MODEL), tile),
        scratch_shapes=[pltpu.VMEM((cap, D_MODEL), BF16), pltpu.VMEM((cap, D_MODEL), F32)],
    )
    return pl.pallas_call(
        functools.partial(_moe_kernel, rows),
        grid_spec=grid_spec,
        out_shape=jax.ShapeDtypeStruct((n, D_MODEL), F32),
        compiler_params=_params(("arbitrary", "arbitrary", "arbitrary"), 56),
        name="moe",
    )(counts, offs, ends, x2, xb2, rank_row, rank, wsel, wg, wu, wd, ln)


def _tile_sizes(b, t):
    n = b * t
    tm = min(512, n)
    tm_proj = min(512, t)
    tt = min(512, t)
    tm_moe = min(1024, n)
    rows = 288 if tm_moe == 1024 else 160
    return tm, tm_proj, tt, tm_moe, rows


def _to_block_diag(s):
    b, h, r, c = s.shape
    eye = jnp.eye(h, dtype=s.dtype)
    return jnp.einsum("bhrc,hg->bhrgc", s, eye).reshape(b, h * r, h * c)


def _from_block_diag(s, r, c):
    b = s.shape[0]
    s5 = s.reshape(b, N_HEADS, r, N_HEADS, c)
    return jnp.stack([s5[:, h, :, h, :] for h in range(N_HEADS)], axis=1)


def _pad_cols(a, width):
    return jnp.pad(a, ((0, 0), (0, width - a.shape[1])))


def _prep_layer(l, p):
    d = D_MODEL
    bw = BR_WIDTH
    w = {}
    w["win"] = p["w_in"][l].astype(BF16)
    mu = p["rwkv_mu_x"][l]
    if l >= 1:
        v1, v2, v0, mu_v = p["rwkv_v1"][l - 1], p["rwkv_v2"][l - 1], p["rwkv_v0"][l - 1], p["rwkv_mu_v"][l - 1]
    else:
        v1, v2 = jnp.zeros((d, RWKV_V_LORA), F32), jnp.zeros((RWKV_V_LORA, bw), F32)
        v0, mu_v = jnp.zeros((bw,), F32), jnp.zeros((d,), F32)
    first = [p["rwkv_w1"][l], p["rwkv_a1"][l], p["rwkv_g1"][l], v1, p["gla_w1"][l]]
    shift_mu = [mu[0], mu[1], mu[2], mu_v, jnp.zeros((d,), F32)]
    w["wl1"] = _pad_cols(jnp.concatenate(first, axis=1), LORA_COLS).astype(BF16)
    w["wl1mu"] = _pad_cols(jnp.concatenate([m[:, None] * a for m, a in zip(shift_mu, first)], axis=1),
                           LORA_COLS).astype(BF16)
    second = jax.scipy.linalg.block_diag(p["rwkv_w2"][l], p["rwkv_a2"][l], p["rwkv_g2"][l], v2, p["gla_w2"][l])
    w["w2"] = jnp.pad(second, ((0, LORA_COLS - second.shape[0]), (0, 0))).astype(BF16)
    w["bias"] = jnp.concatenate([p["rwkv_w0"][l], p["rwkv_a0"][l], jnp.zeros((bw,), F32), v0,
                                 p["gla_b"][l]])[None]
    rows = [p["rwkv_mu_rkv"][l].reshape(RWKV_COLS)]
    rows += [jnp.pad(p[name][l], (0, RWKV_COLS - bw))
             for name in ("rwkv_k_k", "rwkv_k_a", "rwkv_r_k", "rwkv_ln_g", "rwkv_ln_b")]
    rows += [jnp.zeros((RWKV_COLS,), F32)] * (SUBLANE - len(rows))
    w["rwkv_prm"] = jnp.stack(rows)
    w["ret_prm"] = jnp.stack([p["ret_gn_g"][l], p["ret_gn_b"][l]])
    w["hgrn_ng"] = p["hgrn_norm_g"][l][None]
    w["gla_ng"] = p["gla_norm_g"][l][None]
    w["wg"] = p["w_gate"][l].astype(BF16)
    w["bg"] = p["b_gate"][l]
    w["wbr"] = p["w_br"][l].astype(BF16)
    w["wo"] = p["w_o"][l].astype(BF16)
    w["ln1"] = jnp.stack([p["ln1_g"][l], p["ln1_b"][l]])
    w["ln2"] = jnp.stack([p["ln2_g"][l], p["ln2_b"][l]])
    j = l // 2
    if l % 2 == 0:
        w["ffn"] = (p["ffn_w_gate"][j].astype(BF16), p["ffn_w_up"][j].astype(BF16),
                    p["ffn_w_down"][j].astype(BF16))
    else:
        wr = _pad_cols(p["router_w"][j], LANE)
        br = _pad_cols(p["router_b"][j][None], LANE)
        w["moe"] = (wr, br, p["moe_w_gate"][j].astype(BF16), p["moe_w_up"][j].astype(BF16),
                    p["moe_w_down"][j].astype(BF16))
    return w


def _mixer_consts():
    bdm = _np_block_mask(HEAD_DIM, HEAD_DIM)
    lg = np.log1p(-np.exp2(-5.0 - np.arange(N_HEADS, dtype=np.float64)))
    lg_l = np.repeat(lg, HEAD_DIM)[None, :]
    t = np.arange(CHUNK, dtype=np.float64)[:, None]
    s_side = (np.arange(N_HEADS * CHUNK) % CHUNK)[None, :].astype(np.float64)
    lg_side = np.repeat(lg, CHUNK)[None, :]
    d_mat = np.where(s_side <= t, np.exp((t - s_side) * lg_side), 0.0)
    dec = np.zeros((3 * CHUNK + SUBLANE, BR_WIDTH), np.float64)
    dec[0:CHUNK] = np.exp((t + 1.0) * lg_l)
    dec[CHUNK:2 * CHUNK] = np.exp((CHUNK - 1.0 - t) * lg_l)
    dec[2 * CHUNK:3 * CHUNK] = d_mat
    dec[3 * CHUNK] = np.exp(CHUNK * lg_l[0])
    return {
        "tril": jnp.asarray(np.tril(np.ones((CHUNK, CHUNK), np.float32)), BF16),
        "bdm": jnp.asarray(bdm, F32),
        "head_lanes": jnp.asarray(_np_head_lanes(HEAD_DIM), F32),
        "strict": jnp.asarray(_np_causal_side_by_side(True), F32),
        "incl": jnp.asarray(_np_causal_side_by_side(False), F32),
        "eye": jnp.asarray(np.tile(np.eye(CHUNK, dtype=np.float32), (1, N_HEADS)), F32),
        "ret_dec": jnp.asarray(dec, F32),
        "gla_pair": jnp.asarray(_np_block_mask(GLA_DK, HEAD_DIM), BF16),
        "gla_mask": jnp.asarray(_np_block_mask(HEAD_DIM, GLA_DK), F32),
        "gla_head_lanes": jnp.asarray(_np_head_lanes(GLA_DK), F32),
    }


def _rope_tables(pos0, t):
    half = HEAD_DIM // 2
    pos = pos0 + jnp.arange(t, dtype=F32)
    inv = ROPE_THETA ** (-jnp.arange(half, dtype=F32) / half)
    ang = pos[:, None] * inv[None]
    cos = jnp.cos(ang)
    sin = jnp.sin(ang)
    cos_t = jnp.tile(jnp.concatenate([cos, cos], axis=1), (1, N_HEADS))
    sin_t = jnp.tile(jnp.concatenate([-sin, sin], axis=1), (1, N_HEADS))
    return cos_t, sin_t


def _previous_rows(x, x_last, tm):
    b, t, d = x.shape
    per_seq = t // tm
    tails = x.reshape(b, per_seq, tm, d)[:, :, tm - 1, :]
    prev = jnp.concatenate([x_last[:, None, :], tails[:, :per_seq - 1, :]], axis=1)
    return prev.reshape(b * per_seq, 1, d)


def _run_trunk(x, pos0, s_rwkv, c_shift, s_ret, s_hgrn, s_gla, prm, layers, consts):
    b, t, d = x.shape
    n = b * t
    tm, tm_proj, tt, tm_moe, rows = _tile_sizes(b, t)
    cos_t, sin_t = _rope_tables(pos0, t)
    v_first = None
    new_rwkv, new_shift, new_ret, new_hgrn, new_gla = [], [], [], [], []
    for l in range(DEPTH):
        w = layers[l]
        x_in = x
        x_last = c_shift[l]
        p2, aux2 = _in_proj(x.reshape(n, d), _previous_rows(x, x_last, tm_proj), w["win"], w["wl1"],
                            w["wl1mu"], w["w2"], w["bias"], tm_proj)
        p3 = p2.reshape(b, t, IN_COLS)
        aux3 = aux2.reshape(b, t, AUX_COLS)
        pad = (-b) % SUBLANE
        x_last_p = jnp.concatenate([x_last, jnp.zeros((pad, d), F32)], axis=0) if pad else x_last
        rkv_last = _rows_matmul(x_last_p, w["win"][:, :RWKV_COLS])[:b, None, :]

        res = _rwkv_mixer(p3, aux3, v_first, rkv_last, _to_block_diag(s_rwkv[l]), w["rwkv_prm"], consts, tt)
        if v_first is None:
            o_rwkv, v_first, st_rwkv = res
        else:
            o_rwkv, st_rwkv = res
        o_ret, st_ret = _ret_mixer(p3, cos_t, sin_t, _to_block_diag(jnp.swapaxes(s_ret[l], -1, -2)),
                                   w["ret_prm"], consts, tt)
        o_hgrn, st_hgrn = _hgrn_mixer(p3, _to_block_diag(jnp.swapaxes(s_hgrn[l], -1, -2)),
                                      prm["hgrn_lb_logits"], w["hgrn_ng"], consts, l, tt)
        o_gla, st_gla = _gla_mixer(p3, aux3, _to_block_diag(jnp.swapaxes(s_gla[l], -1, -2)), w["gla_ng"],
                                   consts, tt)

        outs = [o.reshape(n, BR_WIDTH) for o in (o_rwkv, o_ret, o_hgrn, o_gla)]
        is_moe = l % 2 == 1
        merged = _merge(x.reshape(n, d), outs, w["wg"], w["bg"], w["wbr"], w["wo"], w["ln1"], tm, is_moe)
        if not is_moe:
            x1 = merged[0]
            x2 = _ffn(x1, *w["ffn"], w["ln2"], tm_moe)
        else:
            x1, x1b = merged
            wr, br, mg, mu_, md = w["moe"]
            tril_m = jnp.asarray(np.tril(np.ones((tm_moe, tm_moe), np.float32), -1), BF16)
            rank, wsel, cnt = _router(x1, wr, br, tril_m, tm_moe)
            rank_row = rank[:, :N_EXPERTS].T.reshape(N_EXPERTS, 1, n)
            per_sub = cnt[:, :tm_moe // MOE_SUB, :N_EXPERTS]
            ends = jnp.cumsum(per_sub, axis=1)
            x2 = _moe(x1, x1b, ends[:, -1, :].reshape(-1), (ends - per_sub).reshape(-1), ends.reshape(-1),
                      rank_row, rank, wsel, mg, mu_, md, w["ln2"], tm_moe, rows)
        x = x2.reshape(b, t, d)

        new_rwkv.append(_from_block_diag(st_rwkv, HEAD_DIM, HEAD_DIM))
        new_shift.append(x_in[:, -1])
        new_ret.append(jnp.swapaxes(_from_block_diag(st_ret, HEAD_DIM, HEAD_DIM), -1, -2))
        new_hgrn.append(jnp.swapaxes(_from_block_diag(st_hgrn, HEAD_DIM, HEAD_DIM), -1, -2))
        new_gla.append(jnp.swapaxes(_from_block_diag(st_gla, HEAD_DIM, GLA_DK), -1, -2))
    return (x, jnp.stack(new_rwkv), jnp.stack(new_shift), jnp.stack(new_ret), jnp.stack(new_hgrn),
            jnp.stack(new_gla))


def kernel(x_prompt, x_sample, state_rwkv, cache_shift, state_ret, state_hgrn, state_gla, w_in, rwkv_mu_rkv, rwkv_mu_x, rwkv_mu_v, rwkv_w0, rwkv_w1, rwkv_w2, rwkv_a0, rwkv_a1, rwkv_a2, rwkv_v0, rwkv_v1, rwkv_v2, rwkv_g1, rwkv_g2, rwkv_k_k, rwkv_k_a, rwkv_r_k, rwkv_ln_g, rwkv_ln_b, ret_gn_g, ret_gn_b, hgrn_lb_logits, hgrn_norm_g, gla_w1, gla_w2, gla_b, gla_norm_g, w_br, w_gate, b_gate, w_o, ln1_g, ln1_b, ln2_g, ln2_b, ffn_w_gate, ffn_w_up, ffn_w_down, router_w, router_b, moe_w_gate, moe_w_up, moe_w_down):
    prm = {
        'w_in': w_in, 'rwkv_mu_rkv': rwkv_mu_rkv, 'rwkv_mu_x': rwkv_mu_x, 'rwkv_mu_v': rwkv_mu_v,
        'rwkv_w0': rwkv_w0, 'rwkv_w1': rwkv_w1, 'rwkv_w2': rwkv_w2,
        'rwkv_a0': rwkv_a0, 'rwkv_a1': rwkv_a1, 'rwkv_a2': rwkv_a2,
        'rwkv_v0': rwkv_v0, 'rwkv_v1': rwkv_v1, 'rwkv_v2': rwkv_v2,
        'rwkv_g1': rwkv_g1, 'rwkv_g2': rwkv_g2, 'rwkv_k_k': rwkv_k_k, 'rwkv_k_a': rwkv_k_a,
        'rwkv_r_k': rwkv_r_k, 'rwkv_ln_g': rwkv_ln_g, 'rwkv_ln_b': rwkv_ln_b,
        'ret_gn_g': ret_gn_g, 'ret_gn_b': ret_gn_b, 'hgrn_lb_logits': hgrn_lb_logits,
        'hgrn_norm_g': hgrn_norm_g, 'gla_w1': gla_w1, 'gla_w2': gla_w2, 'gla_b': gla_b,
        'gla_norm_g': gla_norm_g, 'w_br': w_br, 'w_gate': w_gate, 'b_gate': b_gate, 'w_o': w_o,
        'ln1_g': ln1_g, 'ln1_b': ln1_b, 'ln2_g': ln2_g, 'ln2_b': ln2_b,
        'ffn_w_gate': ffn_w_gate, 'ffn_w_up': ffn_w_up, 'ffn_w_down': ffn_w_down,
        'router_w': router_w, 'router_b': router_b,
        'moe_w_gate': moe_w_gate, 'moe_w_up': moe_w_up, 'moe_w_down': moe_w_down,
    }
    layers = [_prep_layer(l, prm) for l in range(DEPTH)]
    consts = _mixer_consts()
    bp = x_prompt.shape[0]
    zero_hd = jnp.zeros((DEPTH, bp, N_HEADS, HEAD_DIM, HEAD_DIM), F32)
    zero_shift = jnp.zeros((DEPTH, bp, D_MODEL), F32)
    zero_gla = jnp.zeros((DEPTH, bp, N_HEADS, GLA_DK, HEAD_DIM), F32)
    prompt = _run_trunk(x_prompt, 0.0, zero_hd, zero_shift, zero_hd, zero_hd, zero_gla, prm, layers, consts)
    sample = _run_trunk(x_sample, float(PAST_LEN), state_rwkv, cache_shift, state_ret, state_hgrn,
                        state_gla, prm, layers, consts)
    y_p, p_rwkv, p_shift, p_ret, p_hgrn, p_gla = prompt
    y_s, s_rwkv, s_shift, s_ret, s_hgrn, s_gla = sample
    return (y_p, y_s, p_rwkv, p_shift, p_ret, p_hgrn, p_gla, s_rwkv, s_shift, s_ret, s_hgrn, s_gla)
```

```python
import functools
import math

import numpy as np
import jax
import jax.numpy as jnp
import jax.scipy.linalg
from jax import lax
from jax.experimental import pallas as pl
from jax.experimental.pallas import tpu as pltpu

F32 = jnp.float32
BF16 = jnp.bfloat16

D_MODEL = 1024
DEPTH = 2
PAST_LEN = 4096
CHUNK = 64
SUB = 16
N_BRANCH = 4
BR_WIDTH = D_MODEL // N_BRANCH
HEAD_DIM = 64
N_HEADS = BR_WIDTH // HEAD_DIM
GLA_DK = HEAD_DIM // 2
GLA_KW = N_HEADS * GLA_DK
GLA_GATE_RANK = 16
GLA_TAU = 16.0
RWKV_W_LORA = 32
RWKV_A_LORA = 32
RWKV_V_LORA = 16
RWKV_G_LORA = 64
RWKV_GN_EPS = 64e-5
ROPE_THETA = 10000.0
LN_EPS = 1e-5
D_FF = 2816
N_EXPERTS = 8
ALPHA = (2.0 * DEPTH) ** 0.25
RWKV_COLS = 3 * BR_WIDTH
IN_COLS = 3584
LORA_COLS = 256
AUX_COLS = 4 * BR_WIDTH + GLA_KW
SEQ_GROUP = 8
MOE_LAST_SIZES = 5

MOE_SUB = 256
MOE_WIN = 112
MOE_TAIL = 128

LANE = 128
SUBLANE = 8
BF16_ROWS = 16
LOG2_E = 1.4426950408889634

NN = (((1,), (0,)), ((), ()))
NT = (((1,), (1,)), ((), ()))
TN = (((0,), (0,)), ((), ()))


def _params(sem, vmem_mib):
    return pltpu.CompilerParams(dimension_semantics=sem, vmem_limit_bytes=vmem_mib * 1024 * 1024)


def _const_spec(shape):
    nd = len(shape)
    return pl.BlockSpec(shape, lambda *_: (0,) * nd, pipeline_mode=pl.Buffered(1))


def _dot(a, b, dims=NN):
    return lax.dot_general(a.astype(BF16), b.astype(BF16), dims, preferred_element_type=F32)


def _split(x):
    hi = x.astype(BF16)
    lo = (x - hi.astype(F32)).astype(BF16)
    return hi, lo


def _dot_exact_lhs(a_bf, x):
    hi, lo = _split(x)
    return (jnp.dot(a_bf, hi, preferred_element_type=F32)
            + jnp.dot(a_bf, lo, preferred_element_type=F32))


def _dot_exact_rhs(x, b_bf):
    hi, lo = _split(x)
    return (jnp.dot(hi, b_bf, preferred_element_type=F32)
            + jnp.dot(lo, b_bf, preferred_element_type=F32))


def _dot3(a, b, dims=NN):
    ah, al = _split(a)
    bh, bl = _split(b)
    d = functools.partial(lax.dot_general, dimension_numbers=dims, preferred_element_type=F32)
    return d(ah, bh) + (d(ah, bl) + d(al, bh))


def _sigmoid(x):
    return 1.0 / (1.0 + jnp.exp(-x))


def _softplus(x):
    return jnp.maximum(x, 0.0) + jnp.log(1.0 + jnp.exp(-jnp.abs(x)))


def _silu(x):
    return x * _sigmoid(x)


def _layer_norm_rows(y, g, b):
    mu = jnp.mean(y, axis=-1, keepdims=True)
    yc = y - mu
    var = jnp.mean(yc * yc, axis=-1, keepdims=True)
    return yc * lax.rsqrt(var + LN_EPS) * g + b


def _np_block_mask(rows_per_head, cols_per_head):
    r = np.arange(N_HEADS * rows_per_head)[:, None] // rows_per_head
    c = np.arange(N_HEADS * cols_per_head)[None, :] // cols_per_head
    return (r == c).astype(np.float32)


def _np_head_lanes(cols_per_head):
    m = np.zeros((SUBLANE, N_HEADS * cols_per_head), np.float32)
    for h in range(N_HEADS):
        m[h, h * cols_per_head:(h + 1) * cols_per_head] = 1.0
    return m


def _np_causal_side_by_side(strict):
    t = np.arange(CHUNK)[:, None]
    s = np.arange(N_HEADS * CHUNK)[None, :] % CHUNK
    return ((s < t) if strict else (s <= t)).astype(np.float32)


def _in_proj_kernel(x_ref, prev_ref, win_ref, wl1_ref, wl1mu_ref, w2_ref, bias_ref, p_ref, aux_ref):
    x = x_ref[...]
    row = lax.broadcasted_iota(jnp.int32, x.shape, 0)
    xx = jnp.where(row == 0, prev_ref[0], pltpu.roll(x, 1, axis=0)) - x
    xb = x.astype(BF16)
    h = (jnp.dot(xb, wl1_ref[...], preferred_element_type=F32)
         + jnp.dot(xx.astype(BF16), wl1mu_ref[...], preferred_element_type=F32))
    lane = lax.broadcasted_iota(jnp.int32, h.shape, 1)
    act = jnp.where(lane < RWKV_W_LORA, jnp.tanh(h), h)
    g_lo = RWKV_W_LORA + RWKV_A_LORA
    in_g = jnp.where(lane >= g_lo, jnp.where(lane < g_lo + RWKV_G_LORA, 1.0, 0.0), 0.0)
    act = jnp.where(in_g > 0.5, _sigmoid(h), act)
    z = jnp.dot(act.astype(BF16), w2_ref[...], preferred_element_type=F32) + bias_ref[...]
    p_ref[...] = jnp.dot(xb, win_ref[...], preferred_element_type=F32)
    bw = BR_WIDTH
    w_log = -_softplus(-z[:, 0:bw]) - 0.5
    aux_ref[:, 0:bw] = -jnp.exp(w_log)
    aux_ref[:, bw:2 * bw] = _sigmoid(z[:, bw:2 * bw])
    aux_ref[:, 2 * bw:3 * bw] = z[:, 2 * bw:3 * bw]
    aux_ref[:, 3 * bw:4 * bw] = _sigmoid(z[:, 3 * bw:4 * bw])
    zg = z[:, 4 * bw:]
    aux_ref[:, 4 * bw:] = (jnp.minimum(zg, 0.0) - jnp.log(1.0 + jnp.exp(-jnp.abs(zg)))) * (1.0 / GLA_TAU)


def _in_proj(x2, prev_rows, win, wl1, wl1mu, w2, bias, tm):
    n = x2.shape[0]
    return pl.pallas_call(
        _in_proj_kernel,
        grid=(n // tm,),
        in_specs=[
            pl.BlockSpec((tm, D_MODEL), lambda i: (i, 0)),
            pl.BlockSpec((1, 1, D_MODEL), lambda i: (i, 0, 0)),
            _const_spec(win.shape), _const_spec(wl1.shape), _const_spec(wl1mu.shape),
            _const_spec(w2.shape), _const_spec(bias.shape),
        ],
        out_specs=[pl.BlockSpec((tm, IN_COLS), lambda i: (i, 0)),
                   pl.BlockSpec((tm, AUX_COLS), lambda i: (i, 0))],
        out_shape=[jax.ShapeDtypeStruct((n, IN_COLS), F32), jax.ShapeDtypeStruct((n, AUX_COLS), F32)],
        compiler_params=_params(("arbitrary",), 56),
        name="in_proj",
    )(x2, prev_rows, win, wl1, wl1mu, w2, bias)


def _rows_matmul_kernel(x_ref, w_ref, o_ref):
    o_ref[...] = jnp.dot(x_ref[...].astype(BF16), w_ref[...], preferred_element_type=F32)


def _rows_matmul(x, w):
    return pl.pallas_call(
        _rows_matmul_kernel,
        out_shape=jax.ShapeDtypeStruct((x.shape[0], w.shape[1]), F32),
        name="rows_matmul",
    )(x, w)


def _stack_heads(x, head_lanes):
    xb = x.astype(BF16)
    return jnp.concatenate([xb * head_lanes[h:h + 1, :].astype(BF16) for h in range(N_HEADS)], axis=0)


def _head_sum(x, ones_bf):
    return _dot_exact_rhs(x, ones_bf)


def _head_layer_norm(y, ones_bf, g, b, eps):
    inv = 1.0 / HEAD_DIM
    mu = _head_sum(y, ones_bf) * inv
    yc = y - mu
    var = _head_sum(yc * yc, ones_bf) * inv
    return yc * lax.rsqrt(var + eps) * g + b


def _head_rms_norm(y, ones_bf, g):
    ms = _head_sum(y * y, ones_bf) * (1.0 / HEAD_DIM)
    return y * lax.rsqrt(ms + 1e-6) * g


def _pairwise_block(q, k, v, b2, pair_ones_bf):
    parts = []
    for j in range(SUB):
        lo = (j // SUBLANE) * SUBLANE
        p = q[lo:] * jnp.exp2(b2[lo:] - b2[j:j + 1]) * k[j:j + 1]
        if j % SUBLANE:
            rid = lax.broadcasted_iota(jnp.int32, p.shape, 0) + lo
            p = jnp.where(rid >= j, p, 0.0)
        parts.append(p)
    att = jnp.dot(jnp.concatenate(parts, axis=0).astype(BF16), pair_ones_bf, preferred_element_type=F32)
    outs = []
    off = 0
    for g in range(SUB // SUBLANE):
        rows = SUB - g * SUBLANE
        acc = None
        for j in range(g * SUBLANE, (g + 1) * SUBLANE):
            term = att[off:off + rows] * v[j:j + 1]
            acc = term if acc is None else acc + term
            off += rows
        if g:
            acc = jnp.concatenate([jnp.zeros((g * SUBLANE, v.shape[1]), F32), acc], axis=0)
        outs.append(acc)
    total = outs[0]
    for extra in outs[1:]:
        total = total + extra
    return total


def _gla_state_free(tiles, tril_bf, pair_ones_bf, st_mask, lanes_k, lanes_v):
    n = range(len(tiles))
    q, k, v, glog = ([t[i] for t in tiles] for i in range(4))
    b = [_dot_exact_lhs(tril_bf, glog[i]) for i in n]
    b2 = [b[i] * LOG2_E for i in n]
    blocks = [[] for _ in n]
    for blk in range(CHUNK // SUB):
        r0 = blk * SUB
        sl = slice(r0, r0 + SUB)
        o_blk = [_pairwise_block(q[i][sl], k[i][sl], v[i][sl], b2[i][sl], pair_ones_bf) for i in n]
        if blk:
            c0 = [b[i][r0 - 1:r0] for i in n]
            q_t = [q[i][sl] * jnp.exp(b[i][sl] - c0[i]) for i in n]
            k_t = [k[i][:r0] * jnp.exp(c0[i] - b[i][:r0]) for i in n]
            att = [_dot(q_t[i], _stack_heads(k_t[i], lanes_k), NT) for i in n]
            o_blk = [o_blk[i] + _dot(att[i], _stack_heads(v[i][:r0], lanes_v)) for i in n]
        for i in n:
            blocks[i].append(o_blk[i])
    b_last = [b[i][CHUNK - 1:CHUNK, :] for i in n]
    upd = [st_mask * _dot(v[i], k[i] * jnp.exp(b_last[i] - b[i]), TN) for i in n]
    return [(jnp.concatenate(blocks[i], axis=0), q[i] * jnp.exp(b[i]), upd[i], jnp.exp(b_last[i]))
            for i in n]


def _rwkv_state_free(tiles, tril_bf, hl, strict, incl, eye):
    n = range(len(tiles))
    r, k, v, kk, bv, lw = ([t[i] for t in tiles] for i in range(6))
    l = [_dot_exact_lhs(tril_bf, lw[i]) for i in n]
    l_last = [l[i][CHUNK - 1:CHUNK, :] for i in n]
    e_neg = [jnp.exp(-l[i]) for i in n]
    lhs = [jnp.concatenate([kk[i] * jnp.exp(l[i] - lw[i]), r[i] * jnp.exp(l[i])], axis=0) for i in n]
    rhs = [jnp.concatenate([_stack_heads(k[i] * e_neg[i], hl), _stack_heads(bv[i] * e_neg[i], hl)], axis=0)
           for i in n]
    amat = [_dot(lhs[i], rhs[i], NT) for i in n]
    w = N_HEADS * CHUNK
    a_ab = [amat[i][:CHUNK, w:] * strict for i in n]
    a_vk = [jnp.concatenate([amat[i][:CHUNK, :w] * strict, amat[i][CHUNK:, :w] * incl], axis=0) for i in n]
    a_rb = [amat[i][CHUNK:, w:] * incl for i in n]
    x = [eye + a_ab[i] for i in n]
    m = a_ab
    for _ in range(int(math.log2(CHUNK)) - 1):
        m_st = [_stack_heads(m[i], hl) for i in n]
        m = [_dot(m[i], m_st[i]) for i in n]
        m_st = [_stack_heads(m[i], hl) for i in n]
        x = [x[i] + _dot(x[i], m_st[i]) for i in n]
    from_v = [_dot(a_vk[i], _stack_heads(v[i], hl)) for i in n]
    e_end = [jnp.exp(l_last[i] - l[i]) for i in n]
    upd_v = [_dot(v[i], k[i] * e_end[i], TN) for i in n]
    b_end = [bv[i] * e_end[i] for i in n]
    st_dec = [jnp.exp(l_last[i]) for i in n]
    return [(lhs[i], x[i], a_rb[i], from_v[i], upd_v[i], b_end[i], st_dec[i]) for i in n]


def _rwkv_state_step(parts, sts, bdm, hl):
    n = range(len(parts))
    lhs, x, a_rb, from_v, upd_v, b_end, st_dec = ([p[i] for p in parts] for i in range(7))
    from_state = [_dot(lhs[i], sts[i], NT) for i in n]
    u = [_dot(x[i], _stack_heads(from_state[i][:CHUNK] + from_v[i][:CHUNK], hl)) for i in n]
    upd = [upd_v[i] + _dot(u[i], b_end[i], TN) for i in n]
    new = [sts[i] * st_dec[i] + upd[i] * bdm for i in n]
    y = [from_state[i][CHUNK:] + from_v[i][CHUNK:] + _dot(a_rb[i], _stack_heads(u[i], hl)) for i in n]
    return y, new


def _rwkv_kernel(has_vres, *refs):
    if has_vres:
        (rkv_ref, lw_ref, a_ref, g_ref, vg_ref, vf_ref, last_ref, st0_ref, prm_ref, tril_ref, bdm_ref,
         hl_ref, strict_ref, incl_ref, eye_ref, o_ref, st_out_ref, st_sc, prev_sc) = refs
    else:
        (rkv_ref, lw_ref, a_ref, g_ref, last_ref, st0_ref, prm_ref, tril_ref, bdm_ref,
         hl_ref, strict_ref, incl_ref, eye_ref, o_ref, v_out_ref, st_out_ref, st_sc, prev_sc) = refs
    tb = pl.program_id(1)

    @pl.when(tb == 0)
    def _():
        st_sc[...] = st0_ref[...]
        prev_sc[...] = last_ref[...]

    bw = BR_WIDTH
    k_k = prm_ref[1:2, 0:bw]
    k_a = prm_ref[2:3, 0:bw]
    r_k = prm_ref[3:4, 0:bw]
    ln_g = prm_ref[4:5, 0:bw]
    ln_b = prm_ref[5:6, 0:bw]
    bdm = bdm_ref[...]
    ones_bf = bdm.astype(BF16)
    hl = hl_ref[...]
    tril_bf = tril_ref[...]
    strict = strict_ref[...]
    incl = incl_ref[...]
    eye = eye_ref[...]
    n_seq = rkv_ref.shape[0]
    tt = rkv_ref.shape[1]
    seqs = []
    for s in range(n_seq):
        rkv = rkv_ref[s]
        row = lax.broadcasted_iota(jnp.int32, rkv.shape, 0)
        prev = jnp.where(row == 0, prev_sc[s], pltpu.roll(rkv, 1, axis=0))
        prev_sc[s] = rkv[tt - 1:tt, :]
        mixed = rkv + (prev - rkv) * prm_ref[0:1, :]
        r = mixed[:, 0:bw]
        k = mixed[:, bw:2 * bw]
        v = mixed[:, 2 * bw:]
        a = a_ref[s]
        if has_vres:
            v = v + (vf_ref[s] - v) * vg_ref[s]
        else:
            v_out_ref[s] = v
        kk = k * k_k
        kk = kk * lax.rsqrt(jnp.maximum(_head_sum(kk * kk, ones_bf), 1e-24))
        k = k * (1.0 + (a - 1.0) * k_a)
        seqs.append((r, k, v, kk, -(kk * a), lw_ref[s]))
    n_chunks = tt // CHUNK
    tiles = [tuple(z[c * CHUNK:(c + 1) * CHUNK] for z in seqs[s])
             for c in range(n_chunks) for s in range(n_seq)]
    parts = _rwkv_state_free(tiles, tril_bf, hl, strict, incl, eye)
    sts = [st_sc[s] for s in range(n_seq)]
    ys = [[] for _ in range(n_seq)]
    for c in range(n_chunks):
        y_c, sts = _rwkv_state_step(parts[c * n_seq:(c + 1) * n_seq], sts, bdm, hl)
        for s in range(n_seq):
            ys[s].append(y_c[s])
    for s in range(n_seq):
        r, k, v = seqs[s][0:3]
        st_sc[s] = sts[s]
        y = ys[s][0] if len(ys[s]) == 1 else jnp.concatenate(ys[s], axis=0)
        y = _head_layer_norm(y, ones_bf, ln_g, ln_b, RWKV_GN_EPS)
        bonus = _head_sum(r * k * r_k, ones_bf) * v
        o_ref[s] = (y + bonus) * g_ref[s]

    @pl.when(tb == pl.num_programs(1) - 1)
    def _():
        for s in range(n_seq):
            st_out_ref[s] = sts[s]


def _rwkv_mixer(p3, aux3, v_first, rkv_last, st0, prm, consts, tt):
    b, t, _ = p3.shape
    bw = BR_WIDTH
    pb = min(b, SEQ_GROUP)
    has_vres = v_first is not None
    tok = lambda j: pl.BlockSpec((pb, tt, bw), lambda bi, ti, j=j: (bi, ti, j))
    in_specs = [pl.BlockSpec((pb, tt, RWKV_COLS), lambda bi, ti: (bi, ti, 0)),
                tok(0), tok(1), tok(2)]
    args = [p3, aux3, aux3, aux3]
    if has_vres:
        in_specs += [tok(3), pl.BlockSpec((pb, tt, bw), lambda bi, ti: (bi, ti, 0))]
        args += [aux3, v_first]
    in_specs += [pl.BlockSpec((pb, 1, RWKV_COLS), lambda bi, ti: (bi, 0, 0)),
                 pl.BlockSpec((pb, bw, bw), lambda bi, ti: (bi, 0, 0)),
                 _const_spec(prm.shape)]
    args += [rkv_last, st0, prm]
    for name in ("tril", "bdm", "head_lanes", "strict", "incl", "eye"):
        in_specs.append(_const_spec(consts[name].shape))
        args.append(consts[name])
    seq = pl.BlockSpec((pb, tt, bw), lambda bi, ti: (bi, ti, 0))
    st_spec = pl.BlockSpec((pb, bw, bw), lambda bi, ti: (bi, 0, 0))
    seq_shape = jax.ShapeDtypeStruct((b, t, bw), F32)
    st_shape = jax.ShapeDtypeStruct((b, bw, bw), F32)
    if has_vres:
        out_specs, out_shape = [seq, st_spec], [seq_shape, st_shape]
    else:
        out_specs, out_shape = [seq, seq, st_spec], [seq_shape, seq_shape, st_shape]
    return pl.pallas_call(
        functools.partial(_rwkv_kernel, has_vres),
        grid=(b // pb, t // tt),
        in_specs=in_specs, out_specs=out_specs, out_shape=out_shape,
        scratch_shapes=[pltpu.VMEM((pb, bw, bw), F32), pltpu.VMEM((pb, 1, RWKV_COLS), F32)],
        compiler_params=_params(("arbitrary", "arbitrary"), 48),
        name="rwkv_mixer",
    )(*args)


def _rot_half(z):
    w = z.shape[1]
    half = HEAD_DIM // 2
    lane = lax.broadcasted_iota(jnp.int32, z.shape, 1)
    first = (lane % HEAD_DIM) < half
    return jnp.where(first, pltpu.roll(z, w - half, axis=1), pltpu.roll(z, half, axis=1))


def _ret_kernel(q_ref, k_ref, v_ref, g_ref, cos_ref, sin_ref, st0_ref, prm_ref, dec_ref, bdm_ref, hl_ref,
                o_ref, st_out_ref, st_sc):
    tb = pl.program_id(1)

    @pl.when(tb == 0)
    def _():
        st_sc[...] = st0_ref[...]

    cos = cos_ref[...]
    sin = sin_ref[...]
    bdm = bdm_ref[...]
    hl = hl_ref[...]
    ones_bf = bdm.astype(BF16)
    q_dec = dec_ref[0:CHUNK, :]
    k_dec = dec_ref[CHUNK:2 * CHUNK, :]
    d_mat = dec_ref[2 * CHUNK:3 * CHUNK, :]
    s_dec = dec_ref[3 * CHUNK:3 * CHUNK + 1, :]
    n_seq = q_ref.shape[0]
    tt = q_ref.shape[1]
    qs, ks, vs = [], [], []
    for s in range(n_seq):
        q = q_ref[s]
        k = k_ref[s]
        qs.append(q * cos + _rot_half(q) * sin)
        ks.append((k * cos + _rot_half(k) * sin) * (HEAD_DIM ** -0.5))
        vs.append(v_ref[s])
    sts = [st_sc[s] for s in range(n_seq)]
    outs = [[] for _ in range(n_seq)]
    for c in range(tt // CHUNK):
        sl = slice(c * CHUNK, (c + 1) * CHUNK)
        for s in range(n_seq):
            qc, kc, vc = qs[s][sl], ks[s][sl], vs[s][sl]
            att = _dot(qc, _stack_heads(kc, hl), NT) * d_mat
            outs[s].append(_dot(qc * q_dec, sts[s], NT) + _dot(att, _stack_heads(vc, hl)))
            sts[s] = sts[s] * s_dec + bdm * _dot(vc, kc * k_dec, TN)
    for s in range(n_seq):
        st_sc[s] = sts[s]
        o = outs[s][0] if len(outs[s]) == 1 else jnp.concatenate(outs[s], axis=0)
        o = _head_layer_norm(o, ones_bf, prm_ref[0:1, :], prm_ref[1:2, :], LN_EPS)
        o_ref[s] = o * _silu(g_ref[s])

    @pl.when(tb == pl.num_programs(1) - 1)
    def _():
        for s in range(n_seq):
            st_out_ref[s] = sts[s]


def _ret_mixer(p3, cos_t, sin_t, st0, prm, consts, tt):
    b, t, _ = p3.shape
    bw = BR_WIDTH
    pb = min(b, SEQ_GROUP)
    base = RWKV_COLS // bw
    tok = lambda j: pl.BlockSpec((pb, tt, bw), lambda bi, ti, j=j: (bi, ti, base + j))
    tab = pl.BlockSpec((tt, bw), lambda bi, ti: (ti, 0))
    st_spec = pl.BlockSpec((pb, bw, bw), lambda bi, ti: (bi, 0, 0))
    dec, bdm, hl = consts["ret_dec"], consts["bdm"], consts["head_lanes"]
    return pl.pallas_call(
        _ret_kernel,
        grid=(b // pb, t // tt),
        in_specs=[tok(0), tok(1), tok(2), tok(3), tab, tab, st_spec, _const_spec(prm.shape),
                  _const_spec(dec.shape), _const_spec(bdm.shape), _const_spec(hl.shape)],
        out_specs=[pl.BlockSpec((pb, tt, bw), lambda bi, ti: (bi, ti, 0)), st_spec],
        out_shape=[jax.ShapeDtypeStruct((b, t, bw), F32), jax.ShapeDtypeStruct((b, bw, bw), F32)],
        scratch_shapes=[pltpu.VMEM((pb, bw, bw), F32)],
        compiler_params=_params(("arbitrary", "arbitrary"), 40),
        name="ret_mixer",
    )(p3, p3, p3, p3, cos_t, sin_t, st0, prm, dec, bdm, hl)


def _gated_mixer_tail(seqs, norm_g, st_sc, st_out_ref, o_ref, tril_bf, pair_ones_bf, st_mask,
                      lanes_k, lanes_v, ones_v_bf):
    n_seq = len(seqs)
    tt = seqs[0][0].shape[0]
    n_chunks = tt // CHUNK
    tiles = [tuple(z[c * CHUNK:(c + 1) * CHUNK] for z in seqs[s][0:4])
             for c in range(n_chunks) for s in range(n_seq)]
    parts = _gla_state_free(tiles, tril_bf, pair_ones_bf, st_mask, lanes_k, lanes_v)
    sts = [st_sc[s] for s in range(n_seq)]
    outs = [[] for _ in range(n_seq)]
    for c in range(n_chunks):
        for s in range(n_seq):
            o_intra, q_dec, upd, dec = parts[c * n_seq + s]
            outs[s].append(o_intra + _dot(q_dec, sts[s], NT))
            sts[s] = sts[s] * dec + upd
    for s in range(n_seq):
        st_sc[s] = sts[s]
        o = outs[s][0] if len(outs[s]) == 1 else jnp.concatenate(outs[s], axis=0)
        o_ref[s] = _head_rms_norm(o, ones_v_bf, norm_g) * _silu(seqs[s][4])

    @pl.when(pl.program_id(1) == pl.num_programs(1) - 1)
    def _():
        for s in range(n_seq):
            st_out_ref[s] = sts[s]


def _hgrn_kernel(layer, q_ref, f_ref, i_ref, g_ref, st0_ref, lbl_ref, ng_ref, tril_ref, bdm_ref, hl_ref,
                 o_ref, st_out_ref, st_sc):
    @pl.when(pl.program_id(1) == 0)
    def _():
        st_sc[...] = st0_ref[...]

    logits = lbl_ref[...]
    ex = jnp.exp(logits - jnp.max(logits, axis=0, keepdims=True))
    sm = ex / jnp.sum(ex, axis=0, keepdims=True)
    lb = jnp.zeros((1, BR_WIDTH), F32)
    for d in range(1, layer + 1):
        lb = lb + sm[d:d + 1, :]
    seqs = []
    for s in range(q_ref.shape[0]):
        fz = f_ref[s]
        f = lb + (1.0 - lb) * _sigmoid(fz)
        k = (1.0 - lb) * _sigmoid(-fz)
        seqs.append((_silu(q_ref[s]), k, i_ref[s], jnp.log(f), g_ref[s]))
    bdm = bdm_ref[...]
    ones_bf = bdm.astype(BF16)
    hl = hl_ref[...]
    _gated_mixer_tail(seqs, ng_ref[...], st_sc, st_out_ref, o_ref, tril_ref[...], ones_bf, bdm, hl, hl,
                      ones_bf)


def _hgrn_mixer(p3, st0, lb_logits, norm_g, consts, layer, tt):
    b, t, _ = p3.shape
    bw = BR_WIDTH
    pb = min(b, SEQ_GROUP)
    base = (RWKV_COLS + 4 * bw) // bw
    tok = lambda j: pl.BlockSpec((pb, tt, bw), lambda bi, ti, j=j: (bi, ti, base + j))
    st_spec = pl.BlockSpec((pb, bw, bw), lambda bi, ti: (bi, 0, 0))
    tril, bdm, hl = consts["tril"], consts["bdm"], consts["head_lanes"]
    return pl.pallas_call(
        functools.partial(_hgrn_kernel, layer),
        grid=(b // pb, t // tt),
        in_specs=[tok(0), tok(1), tok(2), tok(3), st_spec, _const_spec(lb_logits.shape),
                  _const_spec(norm_g.shape), _const_spec(tril.shape), _const_spec(bdm.shape),
                  _const_spec(hl.shape)],
        out_specs=[pl.BlockSpec((pb, tt, bw), lambda bi, ti: (bi, ti, 0)), st_spec],
        out_shape=[jax.ShapeDtypeStruct((b, t, bw), F32), jax.ShapeDtypeStruct((b, bw, bw), F32)],
        scratch_shapes=[pltpu.VMEM((pb, bw, bw), F32)],
        compiler_params=_params(("arbitrary", "arbitrary"), 48),
        name="hgrn_mixer",
    )(p3, p3, p3, p3, st0, lb_logits, norm_g, tril, bdm, hl)


def _gla_kernel(q_ref, k_ref, v_ref, g_ref, la_ref, st0_ref, ng_ref, tril_ref, pair_ref, mask_ref,
                bdm_ref, hlk_ref, hlv_ref, o_ref, st_out_ref, st_sc):
    @pl.when(pl.program_id(1) == 0)
    def _():
        st_sc[...] = st0_ref[...]

    seqs = [(q_ref[s] * (GLA_DK ** -0.5), k_ref[s], v_ref[s], la_ref[s], g_ref[s])
            for s in range(q_ref.shape[0])]
    _gated_mixer_tail(seqs, ng_ref[...], st_sc, st_out_ref, o_ref, tril_ref[...], pair_ref[...],
                      mask_ref[...], hlk_ref[...], hlv_ref[...], bdm_ref[...].astype(BF16))


def _gla_mixer(p3, aux3, st0, norm_g, consts, tt):
    b, t, _ = p3.shape
    bw, kw = BR_WIDTH, GLA_KW
    pb = min(b, SEQ_GROUP)
    gla0 = RWKV_COLS + 8 * bw
    st_spec = pl.BlockSpec((pb, bw, kw), lambda bi, ti: (bi, 0, 0))
    names = ("tril", "gla_pair", "gla_mask", "bdm", "gla_head_lanes", "head_lanes")
    return pl.pallas_call(
        _gla_kernel,
        grid=(b // pb, t // tt),
        in_specs=[pl.BlockSpec((pb, tt, kw), lambda bi, ti: (bi, ti, gla0 // kw)),
                  pl.BlockSpec((pb, tt, kw), lambda bi, ti: (bi, ti, gla0 // kw + 1)),
                  pl.BlockSpec((pb, tt, bw), lambda bi, ti: (bi, ti, (gla0 + 2 * kw) // bw)),
                  pl.BlockSpec((pb, tt, bw), lambda bi, ti: (bi, ti, (gla0 + 2 * kw) // bw + 1)),
                  pl.BlockSpec((pb, tt, kw), lambda bi, ti: (bi, ti, 4 * bw // kw)),
                  st_spec, _const_spec(norm_g.shape)] + [_const_spec(consts[nm].shape) for nm in names],
        out_specs=[pl.BlockSpec((pb, tt, bw), lambda bi, ti: (bi, ti, 0)), st_spec],
        out_shape=[jax.ShapeDtypeStruct((b, t, bw), F32), jax.ShapeDtypeStruct((b, bw, kw), F32)],
        scratch_shapes=[pltpu.VMEM((pb, bw, kw), F32)],
        compiler_params=_params(("arbitrary", "arbitrary"), 48),
        name="gla_mixer",
    )(p3, p3, p3, p3, aux3, st0, norm_g, *[consts[nm] for nm in names])


def _merge_kernel(emit_bf16, x_ref, o0_ref, o1_ref, o2_ref, o3_ref, wg_ref, bg_ref, wbr_ref, wo_ref,
                  ln_ref, *out_refs):
    branches = (o0_ref, o1_ref, o2_ref, o3_ref)
    half = x_ref.shape[0] // 2
    rows = [slice(0, half), slice(half, 2 * half)]
    x = [x_ref[r, :] for r in rows]
    xb = [v.astype(BF16) for v in x]
    merged = [None, None]
    for m in range(N_BRANCH):
        pre = [jnp.dot(xb[h], wg_ref[m], preferred_element_type=F32) for h in range(2)]
        proj = [jnp.dot(branches[m][rows[h], :].astype(BF16), wbr_ref[m], preferred_element_type=F32)
                for h in range(2)]
        for h in range(2):
            term = _sigmoid(pre[h] + bg_ref[m:m + 1, :]) * proj[h]
            merged[h] = term if merged[h] is None else merged[h] + term
    y = [ALPHA * x[h] + jnp.dot(merged[h].astype(BF16), wo_ref[...], preferred_element_type=F32)
         for h in range(2)]
    for h in range(2):
        yn = _layer_norm_rows(y[h], ln_ref[0:1, :], ln_ref[1:2, :])
        out_refs[0][rows[h], :] = yn
        if emit_bf16:
            out_refs[1][rows[h], :] = yn.astype(BF16)


def _merge(x2, outs, wg, bg, wbr, wo, ln, tm, emit_bf16):
    n = x2.shape[0]
    row = pl.BlockSpec((tm, D_MODEL), lambda i: (i, 0))
    br = pl.BlockSpec((tm, BR_WIDTH), lambda i: (i, 0))
    out_specs = [row]
    out_shape = [jax.ShapeDtypeStruct((n, D_MODEL), F32)]
    if emit_bf16:
        out_specs.append(row)
        out_shape.append(jax.ShapeDtypeStruct((n, D_MODEL), BF16))
    return pl.pallas_call(
        functools.partial(_merge_kernel, emit_bf16),
        grid=(n // tm,),
        in_specs=[row, br, br, br, br, _const_spec(wg.shape), _const_spec(bg.shape),
                  _const_spec(wbr.shape), _const_spec(wo.shape), _const_spec(ln.shape)],
        out_specs=out_specs, out_shape=out_shape,
        compiler_params=_params(("arbitrary",), 56),
        name="merge",
    )(x2, *outs, wg, bg, wbr, wo, ln)


FF_SPLIT = 2
FF_PART = D_FF // FF_SPLIT
FFN_PARTS = (1024, 1024, 768)


def _ffn_kernel(x_ref, wg_ref, wu_ref, wd_ref, ln_ref, o_ref):
    x = x_ref[...]
    xb = x.astype(BF16)
    acc = ALPHA * x
    lo = 0
    for width in FFN_PARTS:
        cs = slice(lo, lo + width)
        lo += width
        h = (_silu(jnp.dot(xb, wg_ref[:, cs], preferred_element_type=F32))
             * jnp.dot(xb, wu_ref[:, cs], preferred_element_type=F32))
        acc = acc + jnp.dot(h.astype(BF16), wd_ref[cs, :], preferred_element_type=F32)
    o_ref[...] = _layer_norm_rows(acc, ln_ref[0:1, :], ln_ref[1:2, :])


def _ffn(x2, wg, wu, wd, ln, tm):
    n = x2.shape[0]
    row = pl.BlockSpec((tm, D_MODEL), lambda i: (i, 0))
    return pl.pallas_call(
        _ffn_kernel,
        grid=(n // tm,),
        in_specs=[row, _const_spec(wg.shape), _const_spec(wu.shape), _const_spec(wd.shape),
                  _const_spec(ln.shape)],
        out_specs=row,
        out_shape=jax.ShapeDtypeStruct((n, D_MODEL), F32),
        compiler_params=_params(("arbitrary",), 60),
        name="ffn",
    )(x2, wg, wu, wd, ln)


def _router_kernel(x_ref, wr_ref, br_ref, tril_ref, rank_ref, wsel_ref, cnt_ref):
    logits = _dot3(x_ref[...], wr_ref[...]) + br_ref[...]
    lane = lax.broadcasted_iota(jnp.int32, logits.shape, 1)
    neg = jnp.float32(-jnp.inf)
    logits = jnp.where(lane < N_EXPERTS, logits, neg)
    m1 = jnp.max(logits, axis=1, keepdims=True)
    lane_f = lane.astype(F32)
    i1 = jnp.min(jnp.where(logits == m1, lane_f, float(LANE)), axis=1, keepdims=True)
    first = lane_f == i1
    rest = jnp.where(first, neg, logits)
    m2 = jnp.max(rest, axis=1, keepdims=True)
    i2 = jnp.min(jnp.where(rest == m2, lane_f, float(LANE)), axis=1, keepdims=True)
    second = lane_f == i2
    e = jnp.exp(m2 - m1)
    w1 = 1.0 / (1.0 + e)
    w2 = e / (1.0 + e)
    sel = jnp.where(first, 1.0, jnp.where(second, 1.0, 0.0))
    wsel_ref[...] = jnp.where(first, w1, jnp.where(second, w2, 0.0))
    sel_bf = sel.astype(BF16)
    rank = jnp.dot(tril_ref[...], sel_bf, preferred_element_type=F32)
    rank_ref[...] = jnp.where(sel > 0.5, rank, -1.0)
    ones = jnp.ones((SUBLANE, MOE_SUB), BF16)
    row = lax.broadcasted_iota(jnp.int32, (SUBLANE, LANE), 0)
    cnt = jnp.zeros((SUBLANE, LANE), F32)
    for s in range(sel.shape[0] // MOE_SUB):
        part = jnp.dot(ones, sel_bf[s * MOE_SUB:(s + 1) * MOE_SUB, :], preferred_element_type=F32)
        cnt = jnp.where(row == s, part, cnt)
    cnt_ref[0] = cnt.astype(jnp.int32)


def _router(x2, wr, br, tril, tm):
    n = x2.shape[0]
    nt = n // tm
    col = pl.BlockSpec((tm, LANE), lambda i: (i, 0))
    return pl.pallas_call(
        _router_kernel,
        grid=(nt,),
        in_specs=[pl.BlockSpec((tm, D_MODEL), lambda i: (i, 0)), _const_spec(wr.shape),
                  _const_spec(br.shape), _const_spec(tril.shape)],
        out_specs=[col, col, pl.BlockSpec((1, SUBLANE, LANE), lambda i: (i, 0, 0))],
        out_shape=[jax.ShapeDtypeStruct((n, LANE), F32), jax.ShapeDtypeStruct((n, LANE), F32),
                   jax.ShapeDtypeStruct((nt, SUBLANE, LANE), jnp.int32)],
        compiler_params=_params(("arbitrary",), 40),
        name="router",
    )(x2, wr, br, tril)


def _moe_kernel(rows, cnt_ref, off_ref, end_ref, x_ref, xb_ref, rrow_ref, rank_ref, wsel_ref, wg_ref, wu_ref,
                wd_ref, ln_ref, o_ref, xg_sc, yb_sc):
    i = pl.program_id(0)
    e = pl.program_id(1)
    c = pl.program_id(2)
    n_e = pl.num_programs(1)
    n_c = pl.num_programs(2)
    cnt = cnt_ref[i * N_EXPERTS + e]
    n_blk = (cnt + rows - 1) // rows
    tm = xb_ref.shape[0]
    n_sub = tm // MOE_SUB

    def windows(s):
        idx = (i * n_sub + s) * N_EXPERTS + e
        off = off_ref[idx]
        end = end_ref[idx]
        a0 = (off // BF16_ROWS) * BF16_ROWS
        n_win = jnp.where(end > off, (end - a0 + MOE_WIN - 1) // MOE_WIN, 0)
        return a0, n_win

    @pl.when((e == 0) & (c == 0))
    def _():
        o_ref[...] = ALPHA * x_ref[...]

    toks = [slice(s * MOE_SUB, (s + 1) * MOE_SUB) for s in range(n_sub)]

    def gather_piece(s, r0):
        slot = (lax.broadcasted_iota(jnp.int32, (MOE_WIN, MOE_SUB), 0) + r0).astype(F32)
        onehot = jnp.where(rrow_ref[0, :, toks[s]] == slot, 1.0, 0.0).astype(BF16)
        return jnp.dot(onehot, xb_ref[toks[s], :], preferred_element_type=F32)

    def gather_add(r0, piece):
        cur = xg_sc[pl.ds(r0, MOE_WIN), :].astype(F32)
        xg_sc[pl.ds(r0, MOE_WIN), :] = (cur + piece).astype(BF16)

    @pl.when(c == 0)
    def _():
        xg_sc[...] = jnp.zeros(xg_sc.shape, xg_sc.dtype)
        wins = [windows(s) for s in range(n_sub)]
        starts = [pl.multiple_of(a0, BF16_ROWS) for a0, _ in wins]
        pieces = [gather_piece(s, starts[s]) for s in range(n_sub)]
        for s in range(n_sub):
            gather_add(starts[s], pieces[s])
        for s in range(n_sub):
            a0, n_win = wins[s]

            def more(wi, carry, s=s, a0=a0):
                r0 = pl.multiple_of(a0 + wi * MOE_WIN, BF16_ROWS)
                gather_add(r0, gather_piece(s, r0))
                return carry

            lax.fori_loop(1, n_win, more, 0)

    def expert_rows(r0, m):
        xg = xg_sc[pl.ds(r0, m), :]
        h = (_silu(jnp.dot(xg, wg_ref[0], preferred_element_type=F32))
             * jnp.dot(xg, wu_ref[0], preferred_element_type=F32))
        yb = jnp.dot(h.astype(BF16), wd_ref[0], preferred_element_type=F32)

        @pl.when(c == 0)
        def _():
            yb_sc[pl.ds(r0, m), :] = yb

        @pl.when(c > 0)
        def _():
            yb_sc[pl.ds(r0, m), :] = yb_sc[pl.ds(r0, m), :] + yb

    def full_block(blk, carry):
        expert_rows(pl.multiple_of(blk * rows, BF16_ROWS), rows)
        return carry

    lax.fori_loop(0, n_blk - 1, full_block, 0)
    last = pl.multiple_of(jnp.maximum(n_blk - 1, 0) * rows, BF16_ROWS)
    rem = cnt - last
    sizes = [rows - k * BF16_ROWS for k in range(MOE_LAST_SIZES - 1, -1, -1)]
    used = jnp.int32(0)
    lower = 0
    for m in sizes:
        fits = (rem > lower) & (rem <= m)
        pl.when(fits)(functools.partial(expert_rows, last, m))
        used = jnp.where(fits, m, used)
        lower = m

    @pl.when(c == 0)
    def _():
        tail = pl.multiple_of(last + used, BF16_ROWS)
        yb_sc[pl.ds(tail, MOE_TAIL), :] = jnp.zeros((MOE_TAIL, D_MODEL), F32)

    @pl.when(c == n_c - 1)
    def _():
        mine = lax.broadcasted_iota(jnp.int32, (tm, LANE), 1) == e
        rank_col = jnp.sum(jnp.where(mine, rank_ref[...], 0.0), axis=1, keepdims=True)
        w_col = jnp.sum(jnp.where(mine, wsel_ref[...], 0.0), axis=1, keepdims=True)

        def scatter_piece(s, r0):
            slot = (lax.broadcasted_iota(jnp.int32, (MOE_SUB, MOE_WIN), 1) + r0).astype(F32)
            onehot = jnp.where(rank_col[toks[s]] == slot, 1.0, 0.0).astype(BF16)
            return jnp.dot(onehot, yb_sc[pl.ds(r0, MOE_WIN), :].astype(BF16), preferred_element_type=F32)

        wins = [windows(s) for s in range(n_sub)]
        starts = [pl.multiple_of(a0, BF16_ROWS) for a0, _ in wins]
        backs = [scatter_piece(s, starts[s]) for s in range(n_sub)]
        for s in range(n_sub):
            o_ref[toks[s], :] = o_ref[toks[s], :] + w_col[toks[s]] * backs[s]
        for s in range(n_sub):
            a0, n_win = wins[s]

            def more(wi, carry, s=s, a0=a0):
                r0 = pl.multiple_of(a0 + wi * MOE_WIN, BF16_ROWS)
                o_ref[toks[s], :] = o_ref[toks[s], :] + w_col[toks[s]] * scatter_piece(s, r0)
                return carry

            lax.fori_loop(1, n_win, more, 0)

    @pl.when((e == n_e - 1) & (c == n_c - 1))
    def _():
        o_ref[...] = _layer_norm_rows(o_ref[...], ln_ref[0:1, :], ln_ref[1:2, :])


def _moe(x2, xb2, counts, offs, ends, rank_row, rank, wsel, wg, wu, wd, ln, tm, rows):
    n = x2.shape[0]
    nt = n // tm
    cap = -(-tm // rows) * rows + MOE_TAIL
    tile = lambda i, e, c, *_: (i, 0)
    grid_spec = pltpu.PrefetchScalarGridSpec(
        num_scalar_prefetch=3,
        grid=(nt, N_EXPERTS, FF_SPLIT),
        in_specs=[
            pl.BlockSpec((tm, D_MODEL), tile),
            pl.BlockSpec((tm, D_MODEL), tile),
            pl.BlockSpec((1, 1, tm), lambda i, e, c, *_: (e, 0, i)),
            pl.BlockSpec((tm, LANE), tile),
            pl.BlockSpec((tm, LANE), tile),
            pl.BlockSpec((1, D_MODEL, FF_PART), lambda i, e, c, *_: (e, 0, c)),
            pl.BlockSpec((1, D_MODEL, FF_PART), lambda i, e, c, *_: (e, 0, c)),
            pl.BlockSpec((1, FF_PART, D_MODEL), lambda i, e, c, *_: (e, c, 0)),
            pl.BlockSpec((2, D_MODEL), lambda i, e, c, *_: (0, 0)),
        ],
        out_specs=pl.BlockSpec((tm, D_MODEL), tile),
        scratch_shapes=[pltpu.VMEM((cap, D_MODEL), BF16), pltpu.VMEM((cap, D_MODEL), F32)],
    )
    return pl.pallas_call(
        functools.partial(_moe_kernel, rows),
        grid_spec=grid_spec,
        out_shape=jax.ShapeDtypeStruct((n, D_MODEL), F32),
        compiler_params=_params(("arbitrary", "arbitrary", "arbitrary"), 56),
        name="moe",
    )(counts, offs, ends, x2, xb2, rank_row, rank, wsel, wg, wu, wd, ln)


def _tile_sizes(b, t):
    n = b * t
    tm = min(512, n)
    tm_proj = min(512, t)
    tt = min(512, t)
    tm_moe = min(1024, n)
    rows = 288 if tm_moe == 1024 else 160
    return tm, tm_proj, tt, tm_moe, rows


def _to_block_diag(s):
    b, h, r, c = s.shape
    eye = jnp.eye(h, dtype=s.dtype)
    return jnp.einsum("bhrc,hg->bhrgc", s, eye).reshape(b, h * r, h * c)


def _from_block_diag(s, r, c):
    b = s.shape[0]
    s5 = s.reshape(b, N_HEADS, r, N_HEADS, c)
    return jnp.stack([s5[:, h, :, h, :] for h in range(N_HEADS)], axis=1)


def _pad_cols(a, width):
    return jnp.pad(a, ((0, 0), (0, width - a.shape[1])))


def _prep_layer(l, p):
    d = D_MODEL
    bw = BR_WIDTH
    w = {}
    w["win"] = p["w_in"][l].astype(BF16)
    mu = p["rwkv_mu_x"][l]
    if l >= 1:
        v1, v2, v0, mu_v = p["rwkv_v1"][l - 1], p["rwkv_v2"][l - 1], p["rwkv_v0"][l - 1], p["rwkv_mu_v"][l - 1]
    else:
        v1, v2 = jnp.zeros((d, RWKV_V_LORA), F32), jnp.zeros((RWKV_V_LORA, bw), F32)
        v0, mu_v = jnp.zeros((bw,), F32), jnp.zeros((d,), F32)
    first = [p["rwkv_w1"][l], p["rwkv_a1"][l], p["rwkv_g1"][l], v1, p["gla_w1"][l]]
    shift_mu = [mu[0], mu[1], mu[2], mu_v, jnp.zeros((d,), F32)]
    w["wl1"] = _pad_cols(jnp.concatenate(first, axis=1), LORA_COLS).astype(BF16)
    w["wl1mu"] = _pad_cols(jnp.concatenate([m[:, None] * a for m, a in zip(shift_mu, first)], axis=1),
                           LORA_COLS).astype(BF16)
    second = jax.scipy.linalg.block_diag(p["rwkv_w2"][l], p["rwkv_a2"][l], p["rwkv_g2"][l], v2, p["gla_w2"][l])
    w["w2"] = jnp.pad(second, ((0, LORA_COLS - second.shape[0]), (0, 0))).astype(BF16)
    w["bias"] = jnp.concatenate([p["rwkv_w0"][l], p["rwkv_a0"][l], jnp.zeros((bw,), F32), v0,
                                 p["gla_b"][l]])[None]
    rows = [p["rwkv_mu_rkv"][l].reshape(RWKV_COLS)]
    rows += [jnp.pad(p[name][l], (0, RWKV_COLS - bw))
             for name in ("rwkv_k_k", "rwkv_k_a", "rwkv_r_k", "rwkv_ln_g", "rwkv_ln_b")]
    rows += [jnp.zeros((RWKV_COLS,), F32)] * (SUBLANE - len(rows))
    w["rwkv_prm"] = jnp.stack(rows)
    w["ret_prm"] = jnp.stack([p["ret_gn_g"][l], p["ret_gn_b"][l]])
    w["hgrn_ng"] = p["hgrn_norm_g"][l][None]
    w["gla_ng"] = p["gla_norm_g"][l][None]
    w["wg"] = p["w_gate"][l].astype(BF16)
    w["bg"] = p["b_gate"][l]
    w["wbr"] = p["w_br"][l].astype(BF16)
    w["wo"] = p["w_o"][l].astype(BF16)
    w["ln1"] = jnp.stack([p["ln1_g"][l], p["ln1_b"][l]])
    w["ln2"] = jnp.stack([p["ln2_g"][l], p["ln2_b"][l]])
    j = l // 2
    if l % 2 == 0:
        w["ffn"] = (p["ffn_w_gate"][j].astype(BF16), p["ffn_w_up"][j].astype(BF16),
                    p["ffn_w_down"][j].astype(BF16))
    else:
        wr = _pad_cols(p["router_w"][j], LANE)
        br = _pad_cols(p["router_b"][j][None], LANE)
        w["moe"] = (wr, br, p["moe_w_gate"][j].astype(BF16), p["moe_w_up"][j].astype(BF16),
                    p["moe_w_down"][j].astype(BF16))
    return w


def _mixer_consts():
    bdm = _np_block_mask(HEAD_DIM, HEAD_DIM)
    lg = np.log1p(-np.exp2(-5.0 - np.arange(N_HEADS, dtype=np.float64)))
    lg_l = np.repeat(lg, HEAD_DIM)[None, :]
    t = np.arange(CHUNK, dtype=np.float64)[:, None]
    s_side = (np.arange(N_HEADS * CHUNK) % CHUNK)[None, :].astype(np.float64)
    lg_side = np.repeat(lg, CHUNK)[None, :]
    d_mat = np.where(s_side <= t, np.exp((t - s_side) * lg_side), 0.0)
    dec = np.zeros((3 * CHUNK + SUBLANE, BR_WIDTH), np.float64)
    dec[0:CHUNK] = np.exp((t + 1.0) * lg_l)
    dec[CHUNK:2 * CHUNK] = np.exp((CHUNK - 1.0 - t) * lg_l)
    dec[2 * CHUNK:3 * CHUNK] = d_mat
    dec[3 * CHUNK] = np.exp(CHUNK * lg_l[0])
    return {
        "tril": jnp.asarray(np.tril(np.ones((CHUNK, CHUNK), np.float32)), BF16),
        "bdm": jnp.asarray(bdm, F32),
        "head_lanes": jnp.asarray(_np_head_lanes(HEAD_DIM), F32),
        "strict": jnp.asarray(_np_causal_side_by_side(True), F32),
        "incl": jnp.asarray(_np_causal_side_by_side(False), F32),
        "eye": jnp.asarray(np.tile(np.eye(CHUNK, dtype=np.float32), (1, N_HEADS)), F32),
        "ret_dec": jnp.asarray(dec, F32),
        "gla_pair": jnp.asarray(_np_block_mask(GLA_DK, HEAD_DIM), BF16),
        "gla_mask": jnp.asarray(_np_block_mask(HEAD_DIM, GLA_DK), F32),
        "gla_head_lanes": jnp.asarray(_np_head_lanes(GLA_DK), F32),
    }


def _rope_tables(pos0, t):
    half = HEAD_DIM // 2
    pos = pos0 + jnp.arange(t, dtype=F32)
    inv = ROPE_THETA ** (-jnp.arange(half, dtype=F32) / half)
    ang = pos[:, None] * inv[None]
    cos = jnp.cos(ang)
    sin = jnp.sin(ang)
    cos_t = jnp.tile(jnp.concatenate([cos, cos], axis=1), (1, N_HEADS))
    sin_t = jnp.tile(jnp.concatenate([-sin, sin], axis=1), (1, N_HEADS))
    return cos_t, sin_t


def _previous_rows(x, x_last, tm):
    b, t, d = x.shape
    per_seq = t // tm
    tails = x.reshape(b, per_seq, tm, d)[:, :, tm - 1, :]
    prev = jnp.concatenate([x_last[:, None, :], tails[:, :per_seq - 1, :]], axis=1)
    return prev.reshape(b * per_seq, 1, d)


def _run_trunk(x, pos0, s_rwkv, c_shift, s_ret, s_hgrn, s_gla, prm, layers, consts):
    b, t, d = x.shape
    n = b * t
    tm, tm_proj, tt, tm_moe, rows = _tile_sizes(b, t)
    cos_t, sin_t = _rope_tables(pos0, t)
    v_first = None
    new_rwkv, new_shift, new_ret, new_hgrn, new_gla = [], [], [], [], []
    for l in range(DEPTH):
        w = layers[l]
        x_in = x
        x_last = c_shift[l]
        p2, aux2 = _in_proj(x.reshape(n, d), _previous_rows(x, x_last, tm_proj), w["win"], w["wl1"],
                            w["wl1mu"], w["w2"], w["bias"], tm_proj)
        p3 = p2.reshape(b, t, IN_COLS)
        aux3 = aux2.reshape(b, t, AUX_COLS)
        pad = (-b) % SUBLANE
        x_last_p = jnp.concatenate([x_last, jnp.zeros((pad, d), F32)], axis=0) if pad else x_last
        rkv_last = _rows_matmul(x_last_p, w["win"][:, :RWKV_COLS])[:b, None, :]

        res = _rwkv_mixer(p3, aux3, v_first, rkv_last, _to_block_diag(s_rwkv[l]), w["rwkv_prm"], consts, tt)
        if v_first is None:
            o_rwkv, v_first, st_rwkv = res
        else:
            o_rwkv, st_rwkv = res
        o_ret, st_ret = _ret_mixer(p3, cos_t, sin_t, _to_block_diag(jnp.swapaxes(s_ret[l], -1, -2)),
                                   w["ret_prm"], consts, tt)
        o_hgrn, st_hgrn = _hgrn_mixer(p3, _to_block_diag(jnp.swapaxes(s_hgrn[l], -1, -2)),
                                      prm["hgrn_lb_logits"], w["hgrn_ng"], consts, l, tt)
        o_gla, st_gla = _gla_mixer(p3, aux3, _to_block_diag(jnp.swapaxes(s_gla[l], -1, -2)), w["gla_ng"],
                                   consts, tt)

        outs = [o.reshape(n, BR_WIDTH) for o in (o_rwkv, o_ret, o_hgrn, o_gla)]
        is_moe = l % 2 == 1
        merged = _merge(x.reshape(n, d), outs, w["wg"], w["bg"], w["wbr"], w["wo"], w["ln1"], tm, is_moe)
        if not is_moe:
            x1 = merged[0]
            x2 = _ffn(x1, *w["ffn"], w["ln2"], tm_moe)
        else:
            x1, x1b = merged
            wr, br, mg, mu_, md = w["moe"]
            tril_m = jnp.asarray(np.tril(np.ones((tm_moe, tm_moe), np.float32), -1), BF16)
            rank, wsel, cnt = _router(x1, wr, br, tril_m, tm_moe)
            rank_row = rank[:, :N_EXPERTS].T.reshape(N_EXPERTS, 1, n)
            per_sub = cnt[:, :tm_moe // MOE_SUB, :N_EXPERTS]
            ends = jnp.cumsum(per_sub, axis=1)
            x2 = _moe(x1, x1b, ends[:, -1, :].reshape(-1), (ends - per_sub).reshape(-1), ends.reshape(-1),
                      rank_row, rank, wsel, mg, mu_, md, w["ln2"], tm_moe, rows)
        x = x2.reshape(b, t, d)

        new_rwkv.append(_from_block_diag(st_rwkv, HEAD_DIM, HEAD_DIM))
        new_shift.append(x_in[:, -1])
        new_ret.append(jnp.swapaxes(_from_block_diag(st_ret, HEAD_DIM, HEAD_DIM), -1, -2))
        new_hgrn.append(jnp.swapaxes(_from_block_diag(st_hgrn, HEAD_DIM, HEAD_DIM), -1, -2))
        new_gla.append(jnp.swapaxes(_from_block_diag(st_gla, HEAD_DIM, GLA_DK), -1, -2))
    return (x, jnp.stack(new_rwkv), jnp.stack(new_shift), jnp.stack(new_ret), jnp.stack(new_hgrn),
            jnp.stack(new_gla))


def kernel(x_prompt, x_sample, state_rwkv, cache_shift, state_ret, state_hgrn, state_gla, w_in, rwkv_mu_rkv, rwkv_mu_x, rwkv_mu_v, rwkv_w0, rwkv_w1, rwkv_w2, rwkv_a0, rwkv_a1, rwkv_a2, rwkv_v0, rwkv_v1, rwkv_v2, rwkv_g1, rwkv_g2, rwkv_k_k, rwkv_k_a, rwkv_r_k, rwkv_ln_g, rwkv_ln_b, ret_gn_g, ret_gn_b, hgrn_lb_logits, hgrn_norm_g, gla_w1, gla_w2, gla_b, gla_norm_g, w_br, w_gate, b_gate, w_o, ln1_g, ln1_b, ln2_g, ln2_b, ffn_w_gate, ffn_w_up, ffn_w_down, router_w, router_b, moe_w_gate, moe_w_up, moe_w_down):
    prm = {
        'w_in': w_in, 'rwkv_mu_rkv': rwkv_mu_rkv, 'rwkv_mu_x': rwkv_mu_x, 'rwkv_mu_v': rwkv_mu_v,
        'rwkv_w0': rwkv_w0, 'rwkv_w1': rwkv_w1, 'rwkv_w2': rwkv_w2,
        'rwkv_a0': rwkv_a0, 'rwkv_a1': rwkv_a1, 'rwkv_a2': rwkv_a2,
        'rwkv_v0': rwkv_v0, 'rwkv_v1': rwkv_v1, 'rwkv_v2': rwkv_v2,
        'rwkv_g1': rwkv_g1, 'rwkv_g2': rwkv_g2, 'rwkv_k_k': rwkv_k_k, 'rwkv_k_a': rwkv_k_a,
        'rwkv_r_k': rwkv_r_k, 'rwkv_ln_g': rwkv_ln_g, 'rwkv_ln_b': rwkv_ln_b,
        'ret_gn_g': ret_gn_g, 'ret_gn_b': ret_gn_b, 'hgrn_lb_logits': hgrn_lb_logits,
        'hgrn_norm_g': hgrn_norm_g, 'gla_w1': gla_w1, 'gla_w2': gla_w2, 'gla_b': gla_b,
        'gla_norm_g': gla_norm_g, 'w_br': w_br, 'w_gate': w_gate, 'b_gate': b_gate, 'w_o': w_o,
        'ln1_g': ln1_g, 'ln1_b': ln1_b, 'ln2_g': ln2_g, 'ln2_b': ln2_b,
        'ffn_w_gate': ffn_w_gate, 'ffn_w_up': ffn_w_up, 'ffn_w_down': ffn_w_down,
        'router_w': router_w, 'router_b': router_b,
        'moe_w_gate': moe_w_gate, 'moe_w_up': moe_w_up, 'moe_w_down': moe_w_down,
    }
    layers = [_prep_layer(l, prm) for l in range(DEPTH)]
    consts = _mixer_consts()
    bp = x_prompt.shape[0]
    zero_hd = jnp.zeros((DEPTH, bp, N_HEADS, HEAD_DIM, HEAD_DIM), F32)
    zero_shift = jnp.zeros((DEPTH, bp, D_MODEL), F32)
    zero_gla = jnp.zeros((DEPTH, bp, N_HEADS, GLA_DK, HEAD_DIM), F32)
    prompt = _run_trunk(x_prompt, 0.0, zero_hd, zero_shift, zero_hd, zero_hd, zero_gla, prm, layers, consts)
    sample = _run_trunk(x_sample, float(PAST_LEN), state_rwkv, cache_shift, state_ret, state_hgrn,
                        state_gla, prm, layers, consts)
    y_p, p_rwkv, p_shift, p_ret, p_hgrn, p_gla = prompt
    y_s, s_rwkv, s_shift, s_ret, s_hgrn, s_gla = sample
    return (y_p, y_s, p_rwkv, p_shift, p_ret, p_hgrn, p_gla, s_rwkv, s_shift, s_ret, s_hgrn, s_gla)
```

```python
import functools
import math

import numpy as np
import jax
import jax.numpy as jnp
import jax.scipy.linalg
from jax import lax
from jax.experimental import pallas as pl
from jax.experimental.pallas import tpu as pltpu

F32 = jnp.float32
BF16 = jnp.bfloat16

D_MODEL = 1024
DEPTH = 2
PAST_LEN = 4096
CHUNK = 64
SUB = 16
N_BRANCH = 4
BR_WIDTH = D_MODEL // N_BRANCH
HEAD_DIM = 64
N_HEADS = BR_WIDTH // HEAD_DIM
GLA_DK = HEAD_DIM // 2
GLA_KW = N_HEADS * GLA_DK
GLA_GATE_RANK = 16
GLA_TAU = 16.0
RWKV_W_LORA = 32
RWKV_A_LORA = 32
RWKV_V_LORA = 16
RWKV_G_LORA = 64
RWKV_GN_EPS = 64e-5
ROPE_THETA = 10000.0
LN_EPS = 1e-5
D_FF = 2816
N_EXPERTS = 8
ALPHA = (2.0 * DEPTH) ** 0.25
RWKV_COLS = 3 * BR_WIDTH
IN_COLS = 3584
LORA_COLS = 256
AUX_COLS = 4 * BR_WIDTH + GLA_KW
SEQ_GROUP = 8
FACTOR_LIMIT = 80.0
MOE_LAST_SIZES = 5

MOE_SUB = 256
MOE_WIN = 112
MOE_TAIL = 128

LANE = 128
SUBLANE = 8
BF16_ROWS = 16
LOG2_E = 1.4426950408889634

NN = (((1,), (0,)), ((), ()))
NT = (((1,), (1,)), ((), ()))
TN = (((0,), (0,)), ((), ()))


def _params(sem, vmem_mib):
    return pltpu.CompilerParams(dimension_semantics=sem, vmem_limit_bytes=vmem_mib * 1024 * 1024)


def _const_spec(shape):
    nd = len(shape)
    return pl.BlockSpec(shape, lambda *_: (0,) * nd, pipeline_mode=pl.Buffered(1))


def _dot(a, b, dims=NN):
    return lax.dot_general(a.astype(BF16), b.astype(BF16), dims, preferred_element_type=F32)


def _split(x):
    hi = x.astype(BF16)
    lo = (x - hi.astype(F32)).astype(BF16)
    return hi, lo


def _dot_exact_lhs(a_bf, x):
    hi, lo = _split(x)
    return (jnp.dot(a_bf, hi, preferred_element_type=F32)
            + jnp.dot(a_bf, lo, preferred_element_type=F32))


def _dot_exact_rhs(x, b_bf):
    hi, lo = _split(x)
    return (jnp.dot(hi, b_bf, preferred_element_type=F32)
            + jnp.dot(lo, b_bf, preferred_element_type=F32))


def _dot3(a, b, dims=NN):
    ah, al = _split(a)
    bh, bl = _split(b)
    d = functools.partial(lax.dot_general, dimension_numbers=dims, preferred_element_type=F32)
    return d(ah, bh) + (d(ah, bl) + d(al, bh))


def _sigmoid(x):
    return 1.0 / (1.0 + jnp.exp(-x))


def _softplus(x):
    return jnp.maximum(x, 0.0) + jnp.log(1.0 + jnp.exp(-jnp.abs(x)))


def _silu(x):
    return x * _sigmoid(x)


def _layer_norm_rows(y, g, b):
    mu = jnp.mean(y, axis=-1, keepdims=True)
    yc = y - mu
    var = jnp.mean(yc * yc, axis=-1, keepdims=True)
    return yc * lax.rsqrt(var + LN_EPS) * g + b


def _np_block_mask(rows_per_head, cols_per_head):
    r = np.arange(N_HEADS * rows_per_head)[:, None] // rows_per_head
    c = np.arange(N_HEADS * cols_per_head)[None, :] // cols_per_head
    return (r == c).astype(np.float32)


def _np_head_lanes(cols_per_head):
    m = np.zeros((SUBLANE, N_HEADS * cols_per_head), np.float32)
    for h in range(N_HEADS):
        m[h, h * cols_per_head:(h + 1) * cols_per_head] = 1.0
    return m


def _np_causal_side_by_side(strict):
    t = np.arange(CHUNK)[:, None]
    s = np.arange(N_HEADS * CHUNK)[None, :] % CHUNK
    return ((s < t) if strict else (s <= t)).astype(np.float32)


def _in_proj_kernel(x_ref, prev_ref, win_ref, wl1_ref, wl1mu_ref, w2_ref, bias_ref, p_ref, aux_ref):
    x = x_ref[...]
    row = lax.broadcasted_iota(jnp.int32, x.shape, 0)
    xx = jnp.where(row == 0, prev_ref[0], pltpu.roll(x, 1, axis=0)) - x
    xb = x.astype(BF16)
    h = (jnp.dot(xb, wl1_ref[...], preferred_element_type=F32)
         + jnp.dot(xx.astype(BF16), wl1mu_ref[...], preferred_element_type=F32))
    lane = lax.broadcasted_iota(jnp.int32, h.shape, 1)
    act = jnp.where(lane < RWKV_W_LORA, jnp.tanh(h), h)
    g_lo = RWKV_W_LORA + RWKV_A_LORA
    in_g = jnp.where(lane >= g_lo, jnp.where(lane < g_lo + RWKV_G_LORA, 1.0, 0.0), 0.0)
    act = jnp.where(in_g > 0.5, _sigmoid(h), act)
    z = jnp.dot(act.astype(BF16), w2_ref[...], preferred_element_type=F32) + bias_ref[...]
    p_ref[...] = jnp.dot(xb, win_ref[...], preferred_element_type=F32)
    bw = BR_WIDTH
    w_log = -_softplus(-z[:, 0:bw]) - 0.5
    aux_ref[:, 0:bw] = -jnp.exp(w_log)
    aux_ref[:, bw:2 * bw] = _sigmoid(z[:, bw:2 * bw])
    aux_ref[:, 2 * bw:3 * bw] = z[:, 2 * bw:3 * bw]
    aux_ref[:, 3 * bw:4 * bw] = _sigmoid(z[:, 3 * bw:4 * bw])
    zg = z[:, 4 * bw:]
    aux_ref[:, 4 * bw:] = (jnp.minimum(zg, 0.0) - jnp.log(1.0 + jnp.exp(-jnp.abs(zg)))) * (1.0 / GLA_TAU)


def _in_proj(x2, prev_rows, win, wl1, wl1mu, w2, bias, tm):
    n = x2.shape[0]
    return pl.pallas_call(
        _in_proj_kernel,
        grid=(n // tm,),
        in_specs=[
            pl.BlockSpec((tm, D_MODEL), lambda i: (i, 0)),
            pl.BlockSpec((1, 1, D_MODEL), lambda i: (i, 0, 0)),
            _const_spec(win.shape), _const_spec(wl1.shape), _const_spec(wl1mu.shape),
            _const_spec(w2.shape), _const_spec(bias.shape),
        ],
        out_specs=[pl.BlockSpec((tm, IN_COLS), lambda i: (i, 0)),
                   pl.BlockSpec((tm, AUX_COLS), lambda i: (i, 0))],
        out_shape=[jax.ShapeDtypeStruct((n, IN_COLS), F32), jax.ShapeDtypeStruct((n, AUX_COLS), F32)],
        compiler_params=_params(("arbitrary",), 56),
        name="in_proj",
    )(x2, prev_rows, win, wl1, wl1mu, w2, bias)


def _rows_matmul_kernel(x_ref, w_ref, o_ref):
    o_ref[...] = jnp.dot(x_ref[...].astype(BF16), w_ref[...], preferred_element_type=F32)


def _rows_matmul(x, w):
    return pl.pallas_call(
        _rows_matmul_kernel,
        out_shape=jax.ShapeDtypeStruct((x.shape[0], w.shape[1]), F32),
        name="rows_matmul",
    )(x, w)


def _stack_heads(x, head_lanes):
    xb = x.astype(BF16)
    return jnp.concatenate([xb * head_lanes[h:h + 1, :].astype(BF16) for h in range(N_HEADS)], axis=0)


def _head_sum(x, ones_bf):
    return _dot_exact_rhs(x, ones_bf)


def _head_layer_norm(y, ones_bf, g, b, eps):
    inv = 1.0 / HEAD_DIM
    mu = _head_sum(y, ones_bf) * inv
    yc = y - mu
    var = _head_sum(yc * yc, ones_bf) * inv
    return yc * lax.rsqrt(var + eps) * g + b


def _head_rms_norm(y, ones_bf, g):
    ms = _head_sum(y * y, ones_bf) * (1.0 / HEAD_DIM)
    return y * lax.rsqrt(ms + 1e-6) * g


def _pairwise_block(q, k, v, b2, pair_ones_bf):
    parts = []
    for j in range(SUB):
        lo = (j // SUBLANE) * SUBLANE
        p = q[lo:] * jnp.exp2(b2[lo:] - b2[j:j + 1]) * k[j:j + 1]
        if j % SUBLANE:
            rid = lax.broadcasted_iota(jnp.int32, p.shape, 0) + lo
            p = jnp.where(rid >= j, p, 0.0)
        parts.append(p)
    att = jnp.dot(jnp.concatenate(parts, axis=0).astype(BF16), pair_ones_bf, preferred_element_type=F32)
    outs = []
    off = 0
    for g in range(SUB // SUBLANE):
        rows = SUB - g * SUBLANE
        acc = None
        for j in range(g * SUBLANE, (g + 1) * SUBLANE):
            term = att[off:off + rows] * v[j:j + 1]
            acc = term if acc is None else acc + term
            off += rows
        if g:
            acc = jnp.concatenate([jnp.zeros((g * SUBLANE, v.shape[1]), F32), acc], axis=0)
        outs.append(acc)
    total = outs[0]
    for extra in outs[1:]:
        total = total + extra
    return total


def _gla_intra_pairwise(tiles, b, pair_ones_bf, lanes_k, lanes_v):
    n = range(len(tiles))
    q, k, v, glog = ([t[i] for t in tiles] for i in range(4))
    b2 = [b[i] * LOG2_E for i in n]
    blocks = [[] for _ in n]
    for blk in range(CHUNK // SUB):
        r0 = blk * SUB
        sl = slice(r0, r0 + SUB)
        o_blk = [_pairwise_block(q[i][sl], k[i][sl], v[i][sl], b2[i][sl], pair_ones_bf) for i in n]
        if blk:
            c0 = [b[i][r0 - 1:r0] for i in n]
            q_t = [q[i][sl] * jnp.exp(b[i][sl] - c0[i]) for i in n]
            k_t = [k[i][:r0] * jnp.exp(c0[i] - b[i][:r0]) for i in n]
            att = [_dot(q_t[i], _stack_heads(k_t[i], lanes_k), NT) for i in n]
            o_blk = [o_blk[i] + _dot(att[i], _stack_heads(v[i][:r0], lanes_v)) for i in n]
        for i in n:
            blocks[i].append(o_blk[i])
    return [jnp.concatenate(blocks[i], axis=0) for i in n]


def _gla_intra_factored(tiles, b, lanes_k, lanes_v):
    n = range(len(tiles))
    q, k, v, _ = ([t[i] for t in tiles] for i in range(4))
    blocks = [[] for _ in n]
    for blk in range(CHUNK // SUB):
        r0 = blk * SUB
        upto = r0 + SUB
        sl = slice(r0, upto)
        if blk:
            c0 = [b[i][r0 - 1:r0] for i in n]
            q_t = [q[i][sl] * jnp.exp(b[i][sl] - c0[i]) for i in n]
            k_t = [k[i][:upto] * jnp.exp(c0[i] - b[i][:upto]) for i in n]
        else:
            q_t = [q[i][sl] * jnp.exp(b[i][sl]) for i in n]
            k_t = [k[i][:upto] * jnp.exp(-b[i][:upto]) for i in n]
        att = [_dot(q_t[i], _stack_heads(k_t[i], lanes_k), NT) for i in n]
        t_id = lax.broadcasted_iota(jnp.int32, (SUB, N_HEADS * upto), 0) + r0
        s_id = lax.broadcasted_iota(jnp.int32, (SUB, N_HEADS * upto), 1) % upto
        causal = s_id <= t_id
        o_blk = [_dot(jnp.where(causal, att[i], 0.0), _stack_heads(v[i][:upto], lanes_v)) for i in n]
        for i in n:
            blocks[i].append(o_blk[i])
    return [jnp.concatenate(blocks[i], axis=0) for i in n]


def _gla_block_range(b):
    worst = -b[SUB - 1:SUB, :]
    for blk in range(1, CHUNK // SUB):
        r0 = blk * SUB
        worst = jnp.maximum(worst, b[r0 - 1:r0, :] - b[r0 + SUB - 1:r0 + SUB, :])
    return worst


def _gla_state_parts(tiles, b, st_mask):
    n = range(len(tiles))
    q, k, v, _ = ([t[i] for t in tiles] for i in range(4))
    b_last = [b[i][CHUNK - 1:CHUNK, :] for i in n]
    upd = [st_mask * _dot(v[i], k[i] * jnp.exp(b_last[i] - b[i]), TN) for i in n]
    return [(q[i] * jnp.exp(b[i]), upd[i], jnp.exp(b_last[i])) for i in n]


def _rwkv_state_free(tiles, tril_bf, hl, strict, incl, eye):
    n = range(len(tiles))
    r, k, v, kk, bv, lw = ([t[i] for t in tiles] for i in range(6))
    l = [_dot_exact_lhs(tril_bf, lw[i]) for i in n]
    l_last = [l[i][CHUNK - 1:CHUNK, :] for i in n]
    e_neg = [jnp.exp(-l[i]) for i in n]
    lhs = [jnp.concatenate([kk[i] * jnp.exp(l[i] - lw[i]), r[i] * jnp.exp(l[i])], axis=0) for i in n]
    rhs = [jnp.concatenate([_stack_heads(k[i] * e_neg[i], hl), _stack_heads(bv[i] * e_neg[i], hl)], axis=0)
           for i in n]
    amat = [_dot(lhs[i], rhs[i], NT) for i in n]
    w = N_HEADS * CHUNK
    a_ab = [amat[i][:CHUNK, w:] * strict for i in n]
    a_vk = [jnp.concatenate([amat[i][:CHUNK, :w] * strict, amat[i][CHUNK:, :w] * incl], axis=0) for i in n]
    a_rb = [amat[i][CHUNK:, w:] * incl for i in n]
    x = [eye + a_ab[i] for i in n]
    m = a_ab
    for _ in range(int(math.log2(CHUNK)) - 1):
        m_st = [_stack_heads(m[i], hl) for i in n]
        m = [_dot(m[i], m_st[i]) for i in n]
        m_st = [_stack_heads(m[i], hl) for i in n]
        x = [x[i] + _dot(x[i], m_st[i]) for i in n]
    from_v = [_dot(a_vk[i], _stack_heads(v[i], hl)) for i in n]
    e_end = [jnp.exp(l_last[i] - l[i]) for i in n]
    upd_v = [_dot(v[i], k[i] * e_end[i], TN) for i in n]
    b_end = [bv[i] * e_end[i] for i in n]
    st_dec = [jnp.exp(l_last[i]) for i in n]
    return [(lhs[i], x[i], a_rb[i], from_v[i], upd_v[i], b_end[i], st_dec[i]) for i in n]


def _rwkv_state_step(parts, sts, bdm, hl):
    n = range(len(parts))
    lhs, x, a_rb, from_v, upd_v, b_end, st_dec = ([p[i] for p in parts] for i in range(7))
    from_state = [_dot(lhs[i], sts[i], NT) for i in n]
    u = [_dot(x[i], _stack_heads(from_state[i][:CHUNK] + from_v[i][:CHUNK], hl)) for i in n]
    upd = [upd_v[i] + _dot(u[i], b_end[i], TN) for i in n]
    new = [sts[i] * st_dec[i] + upd[i] * bdm for i in n]
    y = [from_state[i][CHUNK:] + from_v[i][CHUNK:] + _dot(a_rb[i], _stack_heads(u[i], hl)) for i in n]
    return y, new


def _rwkv_kernel(has_vres, *refs):
    if has_vres:
        (rkv_ref, lw_ref, a_ref, g_ref, vg_ref, vf_ref, last_ref, st0_ref, prm_ref, tril_ref, bdm_ref,
         hl_ref, strict_ref, incl_ref, eye_ref, o_ref, st_out_ref, st_sc, prev_sc) = refs
    else:
        (rkv_ref, lw_ref, a_ref, g_ref, last_ref, st0_ref, prm_ref, tril_ref, bdm_ref,
         hl_ref, strict_ref, incl_ref, eye_ref, o_ref, v_out_ref, st_out_ref, st_sc, prev_sc) = refs
    tb = pl.program_id(1)

    @pl.when(tb == 0)
    def _():
        st_sc[...] = st0_ref[...]
        prev_sc[...] = last_ref[...]

    bw = BR_WIDTH
    k_k = prm_ref[1:2, 0:bw]
    k_a = prm_ref[2:3, 0:bw]
    r_k = prm_ref[3:4, 0:bw]
    ln_g = prm_ref[4:5, 0:bw]
    ln_b = prm_ref[5:6, 0:bw]
    bdm = bdm_ref[...]
    ones_bf = bdm.astype(BF16)
    hl = hl_ref[...]
    tril_bf = tril_ref[...]
    strict = strict_ref[...]
    incl = incl_ref[...]
    eye = eye_ref[...]
    n_seq = rkv_ref.shape[0]
    tt = rkv_ref.shape[1]
    seqs = []
    for s in range(n_seq):
        rkv = rkv_ref[s]
        row = lax.broadcasted_iota(jnp.int32, rkv.shape, 0)
        prev = jnp.where(row == 0, prev_sc[s], pltpu.roll(rkv, 1, axis=0))
        prev_sc[s] = rkv[tt - 1:tt, :]
        mixed = rkv + (prev - rkv) * prm_ref[0:1, :]
        r = mixed[:, 0:bw]
        k = mixed[:, bw:2 * bw]
        v = mixed[:, 2 * bw:]
        a = a_ref[s]
        if has_vres:
            v = v + (vf_ref[s] - v) * vg_ref[s]
        else:
            v_out_ref[s] = v
        kk = k * k_k
        kk = kk * lax.rsqrt(jnp.maximum(_head_sum(kk * kk, ones_bf), 1e-24))
        k = k * (1.0 + (a - 1.0) * k_a)
        seqs.append((r, k, v, kk, -(kk * a), lw_ref[s]))
    n_chunks = tt // CHUNK
    tiles = [tuple(z[c * CHUNK:(c + 1) * CHUNK] for z in seqs[s])
             for c in range(n_chunks) for s in range(n_seq)]
    parts = _rwkv_state_free(tiles, tril_bf, hl, strict, incl, eye)
    sts = [st_sc[s] for s in range(n_seq)]
    ys = [[] for _ in range(n_seq)]
    for c in range(n_chunks):
        y_c, sts = _rwkv_state_step(parts[c * n_seq:(c + 1) * n_seq], sts, bdm, hl)
        for s in range(n_seq):
            ys[s].append(y_c[s])
    for s in range(n_seq):
        r, k, v = seqs[s][0:3]
        st_sc[s] = sts[s]
        y = ys[s][0] if len(ys[s]) == 1 else jnp.concatenate(ys[s], axis=0)
        y = _head_layer_norm(y, ones_bf, ln_g, ln_b, RWKV_GN_EPS)
        bonus = _head_sum(r * k * r_k, ones_bf) * v
        o_ref[s] = (y + bonus) * g_ref[s]

    @pl.when(tb == pl.num_programs(1) - 1)
    def _():
        for s in range(n_seq):
            st_out_ref[s] = sts[s]


def _rwkv_mixer(p3, aux3, v_first, rkv_last, st0, prm, consts, tt):
    b, t, _ = p3.shape
    bw = BR_WIDTH
    pb = min(b, SEQ_GROUP)
    has_vres = v_first is not None
    tok = lambda j: pl.BlockSpec((pb, tt, bw), lambda bi, ti, j=j: (bi, ti, j))
    in_specs = [pl.BlockSpec((pb, tt, RWKV_COLS), lambda bi, ti: (bi, ti, 0)),
                tok(0), tok(1), tok(2)]
    args = [p3, aux3, aux3, aux3]
    if has_vres:
        in_specs += [tok(3), pl.BlockSpec((pb, tt, bw), lambda bi, ti: (bi, ti, 0))]
        args += [aux3, v_first]
    in_specs += [pl.BlockSpec((pb, 1, RWKV_COLS), lambda bi, ti: (bi, 0, 0)),
                 pl.BlockSpec((pb, bw, bw), lambda bi, ti: (bi, 0, 0)),
                 _const_spec(prm.shape)]
    args += [rkv_last, st0, prm]
    for name in ("tril", "bdm", "head_lanes", "strict", "incl", "eye"):
        in_specs.append(_const_spec(consts[name].shape))
        args.append(consts[name])
    seq = pl.BlockSpec((pb, tt, bw), lambda bi, ti: (bi, ti, 0))
    st_spec = pl.BlockSpec((pb, bw, bw), lambda bi, ti: (bi, 0, 0))
    seq_shape = jax.ShapeDtypeStruct((b, t, bw), F32)
    st_shape = jax.ShapeDtypeStruct((b, bw, bw), F32)
    if has_vres:
        out_specs, out_shape = [seq, st_spec], [seq_shape, st_shape]
    else:
        out_specs, out_shape = [seq, seq, st_spec], [seq_shape, seq_shape, st_shape]
    return pl.pallas_call(
        functools.partial(_rwkv_kernel, has_vres),
        grid=(b // pb, t // tt),
        in_specs=in_specs, out_specs=out_specs, out_shape=out_shape,
        scratch_shapes=[pltpu.VMEM((pb, bw, bw), F32), pltpu.VMEM((pb, 1, RWKV_COLS), F32)],
        compiler_params=_params(("arbitrary", "arbitrary"), 48),
        name="rwkv_mixer",
    )(*args)


def _rot_half(z):
    w = z.shape[1]
    half = HEAD_DIM // 2
    lane = lax.broadcasted_iota(jnp.int32, z.shape, 1)
    first = (lane % HEAD_DIM) < half
    return jnp.where(first, pltpu.roll(z, w - half, axis=1), pltpu.roll(z, half, axis=1))


def _ret_kernel(q_ref, k_ref, v_ref, g_ref, cos_ref, sin_ref, st0_ref, prm_ref, dec_ref, bdm_ref, hl_ref,
                o_ref, st_out_ref, st_sc):
    tb = pl.program_id(1)

    @pl.when(tb == 0)
    def _():
        st_sc[...] = st0_ref[...]

    cos = cos_ref[...]
    sin = sin_ref[...]
    bdm = bdm_ref[...]
    hl = hl_ref[...]
    ones_bf = bdm.astype(BF16)
    q_dec = dec_ref[0:CHUNK, :]
    k_dec = dec_ref[CHUNK:2 * CHUNK, :]
    d_mat = dec_ref[2 * CHUNK:3 * CHUNK, :]
    s_dec = dec_ref[3 * CHUNK:3 * CHUNK + 1, :]
    n_seq = q_ref.shape[0]
    tt = q_ref.shape[1]
    qs, ks, vs = [], [], []
    for s in range(n_seq):
        q = q_ref[s]
        k = k_ref[s]
        qs.append(q * cos + _rot_half(q) * sin)
        ks.append((k * cos + _rot_half(k) * sin) * (HEAD_DIM ** -0.5))
        vs.append(v_ref[s])
    sts = [st_sc[s] for s in range(n_seq)]
    outs = [[] for _ in range(n_seq)]
    for c in range(tt // CHUNK):
        sl = slice(c * CHUNK, (c + 1) * CHUNK)
        for s in range(n_seq):
            qc, kc, vc = qs[s][sl], ks[s][sl], vs[s][sl]
            att = _dot(qc, _stack_heads(kc, hl), NT) * d_mat
            outs[s].append(_dot(qc * q_dec, sts[s], NT) + _dot(att, _stack_heads(vc, hl)))
            sts[s] = sts[s] * s_dec + bdm * _dot(vc, kc * k_dec, TN)
    for s in range(n_seq):
        st_sc[s] = sts[s]
        o = outs[s][0] if len(outs[s]) == 1 else jnp.concatenate(outs[s], axis=0)
        o = _head_layer_norm(o, ones_bf, prm_ref[0:1, :], prm_ref[1:2, :], LN_EPS)
        o_ref[s] = o * _silu(g_ref[s])

    @pl.when(tb == pl.num_programs(1) - 1)
    def _():
        for s in range(n_seq):
            st_out_ref[s] = sts[s]


def _ret_mixer(p3, cos_t, sin_t, st0, prm, consts, tt):
    b, t, _ = p3.shape
    bw = BR_WIDTH
    pb = min(b, SEQ_GROUP)
    base = RWKV_COLS // bw
    tok = lambda j: pl.BlockSpec((pb, tt, bw), lambda bi, ti, j=j: (bi, ti, base + j))
    tab = pl.BlockSpec((tt, bw), lambda bi, ti: (ti, 0))
    st_spec = pl.BlockSpec((pb, bw, bw), lambda bi, ti: (bi, 0, 0))
    dec, bdm, hl = consts["ret_dec"], consts["bdm"], consts["head_lanes"]
    return pl.pallas_call(
        _ret_kernel,
        grid=(b // pb, t // tt),
        in_specs=[tok(0), tok(1), tok(2), tok(3), tab, tab, st_spec, _const_spec(prm.shape),
                  _const_spec(dec.shape), _const_spec(bdm.shape), _const_spec(hl.shape)],
        out_specs=[pl.BlockSpec((pb, tt, bw), lambda bi, ti: (bi, ti, 0)), st_spec],
        out_shape=[jax.ShapeDtypeStruct((b, t, bw), F32), jax.ShapeDtypeStruct((b, bw, bw), F32)],
        scratch_shapes=[pltpu.VMEM((pb, bw, bw), F32)],
        compiler_params=_params(("arbitrary", "arbitrary"), 40),
        name="ret_mixer",
    )(p3, p3, p3, p3, cos_t, sin_t, st0, prm, dec, bdm, hl)


def _gated_mixer_tail(seqs, norm_g, st_sc, intra_sc, st_out_ref, o_ref, tril_bf, pair_ones_bf, st_mask,
                      lanes_k, lanes_v, ones_v_bf):
    n_seq = len(seqs)
    tt = seqs[0][0].shape[0]
    n_chunks = tt // CHUNK
    tiles = [tuple(z[c * CHUNK:(c + 1) * CHUNK] for z in seqs[s][0:4])
             for c in range(n_chunks) for s in range(n_seq)]
    b = [_dot_exact_lhs(tril_bf, t[3]) for t in tiles]
    worst = _gla_block_range(b[0])
    for b_i in b[1:]:
        worst = jnp.maximum(worst, _gla_block_range(b_i))
    risky = jnp.max(worst) > FACTOR_LIMIT

    def put_intra(intra):
        for idx, o_i in enumerate(intra):
            c_i, s_i = divmod(idx, n_seq)
            intra_sc[s_i, c_i * CHUNK:(c_i + 1) * CHUNK, :] = o_i

    @pl.when(risky)
    def _():
        put_intra(_gla_intra_pairwise(tiles, b, pair_ones_bf, lanes_k, lanes_v))

    @pl.when(jnp.logical_not(risky))
    def _():
        put_intra(_gla_intra_factored(tiles, b, lanes_k, lanes_v))

    parts = _gla_state_parts(tiles, b, st_mask)
    sts = [st_sc[s] for s in range(n_seq)]
    outs = [[] for _ in range(n_seq)]
    for c in range(n_chunks):
        for s in range(n_seq):
            q_dec, upd, dec = parts[c * n_seq + s]
            outs[s].append(intra_sc[s, c * CHUNK:(c + 1) * CHUNK, :] + _dot(q_dec, sts[s], NT))
            sts[s] = sts[s] * dec + upd
    for s in range(n_seq):
        st_sc[s] = sts[s]
        o = outs[s][0] if len(outs[s]) == 1 else jnp.concatenate(outs[s], axis=0)
        o_ref[s] = _head_rms_norm(o, ones_v_bf, norm_g) * _silu(seqs[s][4])

    @pl.when(pl.program_id(1) == pl.num_programs(1) - 1)
    def _():
        for s in range(n_seq):
            st_out_ref[s] = sts[s]


def _hgrn_kernel(layer, q_ref, f_ref, i_ref, g_ref, st0_ref, lbl_ref, ng_ref, tril_ref, bdm_ref, hl_ref,
                 o_ref, st_out_ref, st_sc, intra_sc):
    @pl.when(pl.program_id(1) == 0)
    def _():
        st_sc[...] = st0_ref[...]

    logits = lbl_ref[...]
    ex = jnp.exp(logits - jnp.max(logits, axis=0, keepdims=True))
    sm = ex / jnp.sum(ex, axis=0, keepdims=True)
    lb = jnp.zeros((1, BR_WIDTH), F32)
    for d in range(1, layer + 1):
        lb = lb + sm[d:d + 1, :]
    seqs = []
    for s in range(q_ref.shape[0]):
        fz = f_ref[s]
        f = lb + (1.0 - lb) * _sigmoid(fz)
        k = (1.0 - lb) * _sigmoid(-fz)
        seqs.append((_silu(q_ref[s]), k, i_ref[s], jnp.log(f), g_ref[s]))
    bdm = bdm_ref[...]
    ones_bf = bdm.astype(BF16)
    hl = hl_ref[...]
    _gated_mixer_tail(seqs, ng_ref[...], st_sc, intra_sc, st_out_ref, o_ref, tril_ref[...], ones_bf, bdm,
                      hl, hl, ones_bf)


def _hgrn_mixer(p3, st0, lb_logits, norm_g, consts, layer, tt):
    b, t, _ = p3.shape
    bw = BR_WIDTH
    pb = min(b, SEQ_GROUP)
    base = (RWKV_COLS + 4 * bw) // bw
    tok = lambda j: pl.BlockSpec((pb, tt, bw), lambda bi, ti, j=j: (bi, ti, base + j))
    st_spec = pl.BlockSpec((pb, bw, bw), lambda bi, ti: (bi, 0, 0))
    tril, bdm, hl = consts["tril"], consts["bdm"], consts["head_lanes"]
    return pl.pallas_call(
        functools.partial(_hgrn_kernel, layer),
        grid=(b // pb, t // tt),
        in_specs=[tok(0), tok(1), tok(2), tok(3), st_spec, _const_spec(lb_logits.shape),
                  _const_spec(norm_g.shape), _const_spec(tril.shape), _const_spec(bdm.shape),
                  _const_spec(hl.shape)],
        out_specs=[pl.BlockSpec((pb, tt, bw), lambda bi, ti: (bi, ti, 0)), st_spec],
        out_shape=[jax.ShapeDtypeStruct((b, t, bw), F32), jax.ShapeDtypeStruct((b, bw, bw), F32)],
        scratch_shapes=[pltpu.VMEM((pb, bw, bw), F32), pltpu.VMEM((pb, tt, bw), F32)],
        compiler_params=_params(("arbitrary", "arbitrary"), 48),
        name="hgrn_mixer",
    )(p3, p3, p3, p3, st0, lb_logits, norm_g, tril, bdm, hl)


def _gla_kernel(q_ref, k_ref, v_ref, g_ref, la_ref, st0_ref, ng_ref, tril_ref, pair_ref, mask_ref,
                bdm_ref, hlk_ref, hlv_ref, o_ref, st_out_ref, st_sc, intra_sc):
    @pl.when(pl.program_id(1) == 0)
    def _():
        st_sc[...] = st0_ref[...]

    seqs = [(q_ref[s] * (GLA_DK ** -0.5), k_ref[s], v_ref[s], la_ref[s], g_ref[s])
            for s in range(q_ref.shape[0])]
    _gated_mixer_tail(seqs, ng_ref[...], st_sc, intra_sc, st_out_ref, o_ref, tril_ref[...], pair_ref[...],
                      mask_ref[...], hlk_ref[...], hlv_ref[...], bdm_ref[...].astype(BF16))


def _gla_mixer(p3, aux3, st0, norm_g, consts, tt):
    b, t, _ = p3.shape
    bw, kw = BR_WIDTH, GLA_KW
    pb = min(b, SEQ_GROUP)
    gla0 = RWKV_COLS + 8 * bw
    st_spec = pl.BlockSpec((pb, bw, kw), lambda bi, ti: (bi, 0, 0))
    names = ("tril", "gla_pair", "gla_mask", "bdm", "gla_head_lanes", "head_lanes")
    return pl.pallas_call(
        _gla_kernel,
        grid=(b // pb, t // tt),
        in_specs=[pl.BlockSpec((pb, tt, kw), lambda bi, ti: (bi, ti, gla0 // kw)),
                  pl.BlockSpec((pb, tt, kw), lambda bi, ti: (bi, ti, gla0 // kw + 1)),
                  pl.BlockSpec((pb, tt, bw), lambda bi, ti: (bi, ti, (gla0 + 2 * kw) // bw)),
                  pl.BlockSpec((pb, tt, bw), lambda bi, ti: (bi, ti, (gla0 + 2 * kw) // bw + 1)),
                  pl.BlockSpec((pb, tt, kw), lambda bi, ti: (bi, ti, 4 * bw // kw)),
                  st_spec, _const_spec(norm_g.shape)] + [_const_spec(consts[nm].shape) for nm in names],
        out_specs=[pl.BlockSpec((pb, tt, bw), lambda bi, ti: (bi, ti, 0)), st_spec],
        out_shape=[jax.ShapeDtypeStruct((b, t, bw), F32), jax.ShapeDtypeStruct((b, bw, kw), F32)],
        scratch_shapes=[pltpu.VMEM((pb, bw, kw), F32), pltpu.VMEM((pb, tt, bw), F32)],
        compiler_params=_params(("arbitrary", "arbitrary"), 48),
        name="gla_mixer",
    )(p3, p3, p3, p3, aux3, st0, norm_g, *[consts[nm] for nm in names])


def _merge_kernel(emit_bf16, x_ref, o0_ref, o1_ref, o2_ref, o3_ref, wg_ref, bg_ref, wbr_ref, wo_ref,
                  ln_ref, *out_refs):
    branches = (o0_ref, o1_ref, o2_ref, o3_ref)
    half = x_ref.shape[0] // 2
    rows = [slice(0, half), slice(half, 2 * half)]
    x = [x_ref[r, :] for r in rows]
    xb = [v.astype(BF16) for v in x]
    merged = [None, None]
    for m in range(N_BRANCH):
        pre = [jnp.dot(xb[h], wg_ref[m], preferred_element_type=F32) for h in range(2)]
        proj = [jnp.dot(branches[m][rows[h], :].astype(BF16), wbr_ref[m], preferred_element_type=F32)
                for h in range(2)]
        for h in range(2):
            term = _sigmoid(pre[h] + bg_ref[m:m + 1, :]) * proj[h]
            merged[h] = term if merged[h] is None else merged[h] + term
    y = [ALPHA * x[h] + jnp.dot(merged[h].astype(BF16), wo_ref[...], preferred_element_type=F32)
         for h in range(2)]
    for h in range(2):
        yn = _layer_norm_rows(y[h], ln_ref[0:1, :], ln_ref[1:2, :])
        out_refs[0][rows[h], :] = yn
        if emit_bf16:
            out_refs[1][rows[h], :] = yn.astype(BF16)


def _merge(x2, outs, wg, bg, wbr, wo, ln, tm, emit_bf16):
    n = x2.shape[0]
    row = pl.BlockSpec((tm, D_MODEL), lambda i: (i, 0))
    br = pl.BlockSpec((tm, BR_WIDTH), lambda i: (i, 0))
    out_specs = [row]
    out_shape = [jax.ShapeDtypeStruct((n, D_MODEL), F32)]
    if emit_bf16:
        out_specs.append(row)
        out_shape.append(jax.ShapeDtypeStruct((n, D_MODEL), BF16))
    return pl.pallas_call(
        functools.partial(_merge_kernel, emit_bf16),
        grid=(n // tm,),
        in_specs=[row, br, br, br, br, _const_spec(wg.shape), _const_spec(bg.shape),
                  _const_spec(wbr.shape), _const_spec(wo.shape), _const_spec(ln.shape)],
        out_specs=out_specs, out_shape=out_shape,
        compiler_params=_params(("arbitrary",), 56),
        name="merge",
    )(x2, *outs, wg, bg, wbr, wo, ln)


FF_SPLIT = 2
FF_PART = D_FF // FF_SPLIT
FFN_PARTS = (1024, 1024, 768)


def _ffn_kernel(x_ref, wg_ref, wu_ref, wd_ref, ln_ref, o_ref):
    x = x_ref[...]
    xb = x.astype(BF16)
    acc = ALPHA * x
    lo = 0
    for width in FFN_PARTS:
        cs = slice(lo, lo + width)
        lo += width
        h = (_silu(jnp.dot(xb, wg_ref[:, cs], preferred_element_type=F32))
             * jnp.dot(xb, wu_ref[:, cs], preferred_element_type=F32))
        acc = acc + jnp.dot(h.astype(BF16), wd_ref[cs, :], preferred_element_type=F32)
    o_ref[...] = _layer_norm_rows(acc, ln_ref[0:1, :], ln_ref[1:2, :])


def _ffn(x2, wg, wu, wd, ln, tm):
    n = x2.shape[0]
    row = pl.BlockSpec((tm, D_MODEL), lambda i: (i, 0))
    return pl.pallas_call(
        _ffn_kernel,
        grid=(n // tm,),
        in_specs=[row, _const_spec(wg.shape), _const_spec(wu.shape), _const_spec(wd.shape),
                  _const_spec(ln.shape)],
        out_specs=row,
        out_shape=jax.ShapeDtypeStruct((n, D_MODEL), F32),
        compiler_params=_params(("arbitrary",), 60),
        name="ffn",
    )(x2, wg, wu, wd, ln)


def _router_kernel(x_ref, wr_ref, br_ref, tril_ref, rank_ref, wsel_ref, cnt_ref):
    logits = _dot3(x_ref[...], wr_ref[...]) + br_ref[...]
    lane = lax.broadcasted_iota(jnp.int32, logits.shape, 1)
    neg = jnp.float32(-jnp.inf)
    logits = jnp.where(lane < N_EXPERTS, logits, neg)
    m1 = jnp.max(logits, axis=1, keepdims=True)
    lane_f = lane.astype(F32)
    i1 = jnp.min(jnp.where(logits == m1, lane_f, float(LANE)), axis=1, keepdims=True)
    first = lane_f == i1
    rest = jnp.where(first, neg, logits)
    m2 = jnp.max(rest, axis=1, keepdims=True)
    i2 = jnp.min(jnp.where(rest == m2, lane_f, float(LANE)), axis=1, keepdims=True)
    second = lane_f == i2
    e = jnp.exp(m2 - m1)
    w1 = 1.0 / (1.0 + e)
    w2 = e / (1.0 + e)
    sel = jnp.where(first, 1.0, jnp.where(second, 1.0, 0.0))
    wsel_ref[...] = jnp.where(first, w1, jnp.where(second, w2, 0.0))
    sel_bf = sel.astype(BF16)
    rank = jnp.dot(tril_ref[...], sel_bf, preferred_element_type=F32)
    rank_ref[...] = jnp.where(sel > 0.5, rank, -1.0)
    ones = jnp.ones((SUBLANE, MOE_SUB), BF16)
    row = lax.broadcasted_iota(jnp.int32, (SUBLANE, LANE), 0)
    cnt = jnp.zeros((SUBLANE, LANE), F32)
    for s in range(sel.shape[0] // MOE_SUB):
        part = jnp.dot(ones, sel_bf[s * MOE_SUB:(s + 1) * MOE_SUB, :], preferred_element_type=F32)
        cnt = jnp.where(row == s, part, cnt)
    cnt_ref[0] = cnt.astype(jnp.int32)


def _router(x2, wr, br, tril, tm):
    n = x2.shape[0]
    nt = n // tm
    col = pl.BlockSpec((tm, LANE), lambda i: (i, 0))
    return pl.pallas_call(
        _router_kernel,
        grid=(nt,),
        in_specs=[pl.BlockSpec((tm, D_MODEL), lambda i: (i, 0)), _const_spec(wr.shape),
                  _const_spec(br.shape), _const_spec(tril.shape)],
        out_specs=[col, col, pl.BlockSpec((1, SUBLANE, LANE), lambda i: (i, 0, 0))],
        out_shape=[jax.ShapeDtypeStruct((n, LANE), F32), jax.ShapeDtypeStruct((n, LANE), F32),
                   jax.ShapeDtypeStruct((nt, SUBLANE, LANE), jnp.int32)],
        compiler_params=_params(("arbitrary",), 40),
        name="router",
    )(x2, wr, br, tril)


def _moe_kernel(rows, cnt_ref, off_ref, end_ref, x_ref, xb_ref, rrow_ref, rank_ref, wsel_ref, wg_ref, wu_ref,
                wd_ref, ln_ref, o_ref, xg_sc, yb_sc):
    i = pl.program_id(0)
    e = pl.program_id(1)
    c = pl.program_id(2)
    n_e = pl.num_programs(1)
    n_c = pl.num_programs(2)
    cnt = cnt_ref[i * N_EXPERTS + e]
    n_blk = (cnt + rows - 1) // rows
    tm = xb_ref.shape[0]
    n_sub = tm // MOE_SUB

    def windows(s):
        idx = (i * n_sub + s) * N_EXPERTS + e
        off = off_ref[idx]
        end = end_ref[idx]
        a0 = (off // BF16_ROWS) * BF16_ROWS
        n_win = jnp.where(end > off, (end - a0 + MOE_WIN - 1) // MOE_WIN, 0)
        return a0, n_win

    @pl.when((e == 0) & (c == 0))
    def _():
        o_ref[...] = ALPHA * x_ref[...]

    toks = [slice(s * MOE_SUB, (s + 1) * MOE_SUB) for s in range(n_sub)]

    def gather_piece(s, r0):
        slot = (lax.broadcasted_iota(jnp.int32, (MOE_WIN, MOE_SUB), 0) + r0).astype(F32)
        onehot = jnp.where(rrow_ref[0, :, toks[s]] == slot, 1.0, 0.0).astype(BF16)
        return jnp.dot(onehot, xb_ref[toks[s], :], preferred_element_type=F32)

    def gather_add(r0, piece):
        cur = xg_sc[pl.ds(r0, MOE_WIN), :].astype(F32)
        xg_sc[pl.ds(r0, MOE_WIN), :] = (cur + piece).astype(BF16)

    @pl.when(c == 0)
    def _():
        xg_sc[...] = jnp.zeros(xg_sc.shape, xg_sc.dtype)
        wins = [windows(s) for s in range(n_sub)]
        starts = [pl.multiple_of(a0, BF16_ROWS) for a0, _ in wins]
        pieces = [gather_piece(s, starts[s]) for s in range(n_sub)]
        for s in range(n_sub):
            gather_add(starts[s], pieces[s])
        for s in range(n_sub):
            a0, n_win = wins[s]

            def more(wi, carry, s=s, a0=a0):
                r0 = pl.multiple_of(a0 + wi * MOE_WIN, BF16_ROWS)
                gather_add(r0, gather_piece(s, r0))
                return carry

            lax.fori_loop(1, n_win, more, 0)

    def expert_rows(r0, m):
        xg = xg_sc[pl.ds(r0, m), :]
        h = (_silu(jnp.dot(xg, wg_ref[0], preferred_element_type=F32))
             * jnp.dot(xg, wu_ref[0], preferred_element_type=F32))
        yb = jnp.dot(h.astype(BF16), wd_ref[0], preferred_element_type=F32)

        @pl.when(c == 0)
        def _():
            yb_sc[pl.ds(r0, m), :] = yb

        @pl.when(c > 0)
        def _():
            yb_sc[pl.ds(r0, m), :] = yb_sc[pl.ds(r0, m), :] + yb

    def full_block(blk, carry):
        expert_rows(pl.multiple_of(blk * rows, BF16_ROWS), rows)
        return carry

    lax.fori_loop(0, n_blk - 1, full_block, 0)
    last = pl.multiple_of(jnp.maximum(n_blk - 1, 0) * rows, BF16_ROWS)
    rem = cnt - last
    sizes = [rows - k * BF16_ROWS for k in range(MOE_LAST_SIZES - 1, -1, -1)]
    used = jnp.int32(0)
    lower = 0
    for m in sizes:
        fits = (rem > lower) & (rem <= m)
        pl.when(fits)(functools.partial(expert_rows, last, m))
        used = jnp.where(fits, m, used)
        lower = m

    @pl.when(c == 0)
    def _():
        tail = pl.multiple_of(last + used, BF16_ROWS)
        yb_sc[pl.ds(tail, MOE_TAIL), :] = jnp.zeros((MOE_TAIL, D_MODEL), F32)

    @pl.when(c == n_c - 1)
    def _():
        mine = lax.broadcasted_iota(jnp.int32, (tm, LANE), 1) == e
        rank_col = jnp.sum(jnp.where(mine, rank_ref[...], 0.0), axis=1, keepdims=True)
        w_col = jnp.sum(jnp.where(mine, wsel_ref[...], 0.0), axis=1, keepdims=True)

        def scatter_piece(s, r0):
            slot = (lax.broadcasted_iota(jnp.int32, (MOE_SUB, MOE_WIN), 1) + r0).astype(F32)
            onehot = jnp.where(rank_col[toks[s]] == slot, 1.0, 0.0).astype(BF16)
            return jnp.dot(onehot, yb_sc[pl.ds(r0, MOE_WIN), :].astype(BF16), preferred_element_type=F32)

        wins = [windows(s) for s in range(n_sub)]
        starts = [pl.multiple_of(a0, BF16_ROWS) for a0, _ in wins]
        backs = [scatter_piece(s, starts[s]) for s in range(n_sub)]
        for s in range(n_sub):
            o_ref[toks[s], :] = o_ref[toks[s], :] + w_col[toks[s]] * backs[s]
        for s in range(n_sub):
            a0, n_win = wins[s]

            def more(wi, carry, s=s, a0=a0):
                r0 = pl.multiple_of(a0 + wi * MOE_WIN, BF16_ROWS)
                o_ref[toks[s], :] = o_ref[toks[s], :] + w_col[toks[s]] * scatter_piece(s, r0)
                return carry

            lax.fori_loop(1, n_win, more, 0)

    @pl.when((e == n_e - 1) & (c == n_c - 1))
    def _():
        o_ref[...] = _layer_norm_rows(o_ref[...], ln_ref[0:1, :], ln_ref[1:2, :])


def _moe(x2, xb2, counts, offs, ends, rank_row, rank, wsel, wg, wu, wd, ln, tm, rows):
    n = x2.shape[0]
    nt = n // tm
    cap = -(-tm // rows) * rows + MOE_TAIL
    tile = lambda i, e, c, *_: (i, 0)
    grid_spec = pltpu.PrefetchScalarGridSpec(
        num_scalar_prefetch=3,
        grid=(nt, N_EXPERTS, FF_SPLIT),
        in_specs=[
            pl.BlockSpec((tm, D_MODEL), tile),
            pl.BlockSpec((tm, D_MODEL), tile),
            pl.BlockSpec((1, 1, tm), lambda i, e, c, *_: (e, 0, i)),
            pl.BlockSpec((tm, LANE), tile),
            pl.BlockSpec((tm, LANE), tile),
            pl.BlockSpec((1, D_MODEL, FF_PART), lambda i, e, c, *_: (e, 0, c)),
            pl.BlockSpec((1, D_MODEL, FF_PART), lambda i, e, c, *_: (e, 0, c)),
            pl.BlockSpec((1, FF_PART, D_MODEL), lambda i, e, c, *_: (e, c, 0)),
            pl.BlockSpec((2, D_MODEL), lambda i, e, c, *_: (0, 0)),
        ],
        out_specs=pl.BlockSpec((tm, D_MODEL), tile),
        scratch_shapes=[pltpu.VMEM((cap, D_MODEL), BF16), pltpu.VMEM((cap, D_MODEL), F32)],
    )
    return pl.pallas_call(
        functools.partial(_moe_kernel, rows),
        grid_spec=grid_spec,
        out_shape=jax.ShapeDtypeStruct((n, D_MODEL), F32),
        compiler_params=_params(("arbitrary", "arbitrary", "arbitrary"), 56),
        name="moe",
    )(counts, offs, ends, x2, xb2, rank_row, rank, wsel, wg, wu, wd, ln)


def _tile_sizes(b, t):
    n = b * t
    tm = min(512, n)
    tm_proj = min(512, t)
    tt = min(512, t)
    tm_moe = min(1024, n)
    rows = 288 if tm_moe == 1024 else 160
    return tm, tm_proj, tt, tm_moe, rows


def _to_block_diag(s):
    b, h, r, c = s.shape
    eye = jnp.eye(h, dtype=s.dtype)
    return jnp.einsum("bhrc,hg->bhrgc", s, eye).reshape(b, h * r, h * c)


def _from_block_diag(s, r, c):
    b = s.shape[0]
    s5 = s.reshape(b, N_HEADS, r, N_HEADS, c)
    return jnp.stack([s5[:, h, :, h, :] for h in range(N_HEADS)], axis=1)


def _pad_cols(a, width):
    return jnp.pad(a, ((0, 0), (0, width - a.shape[1])))


def _prep_layer(l, p):
    d = D_MODEL
    bw = BR_WIDTH
    w = {}
    w["win"] = p["w_in"][l].astype(BF16)
    mu = p["rwkv_mu_x"][l]
    if l >= 1:
        v1, v2, v0, mu_v = p["rwkv_v1"][l - 1], p["rwkv_v2"][l - 1], p["rwkv_v0"][l - 1], p["rwkv_mu_v"][l - 1]
    else:
        v1, v2 = jnp.zeros((d, RWKV_V_LORA), F32), jnp.zeros((RWKV_V_LORA, bw), F32)
        v0, mu_v = jnp.zeros((bw,), F32), jnp.zeros((d,), F32)
    first = [p["rwkv_w1"][l], p["rwkv_a1"][l], p["rwkv_g1"][l], v1, p["gla_w1"][l]]
    shift_mu = [mu[0], mu[1], mu[2], mu_v, jnp.zeros((d,), F32)]
    w["wl1"] = _pad_cols(jnp.concatenate(first, axis=1), LORA_COLS).astype(BF16)
    w["wl1mu"] = _pad_cols(jnp.concatenate([m[:, None] * a for m, a in zip(shift_mu, first)], axis=1),
                           LORA_COLS).astype(BF16)
    second = jax.scipy.linalg.block_diag(p["rwkv_w2"][l], p["rwkv_a2"][l], p["rwkv_g2"][l], v2, p["gla_w2"][l])
    w["w2"] = jnp.pad(second, ((0, LORA_COLS - second.shape[0]), (0, 0))).astype(BF16)
    w["bias"] = jnp.concatenate([p["rwkv_w0"][l], p["rwkv_a0"][l], jnp.zeros((bw,), F32), v0,
                                 p["gla_b"][l]])[None]
    rows = [p["rwkv_mu_rkv"][l].reshape(RWKV_COLS)]
    rows += [jnp.pad(p[name][l], (0, RWKV_COLS - bw))
             for name in ("rwkv_k_k", "rwkv_k_a", "rwkv_r_k", "rwkv_ln_g", "rwkv_ln_b")]
    rows += [jnp.zeros((RWKV_COLS,), F32)] * (SUBLANE - len(rows))
    w["rwkv_prm"] = jnp.stack(rows)
    w["ret_prm"] = jnp.stack([p["ret_gn_g"][l], p["ret_gn_b"][l]])
    w["hgrn_ng"] = p["hgrn_norm_g"][l][None]
    w["gla_ng"] = p["gla_norm_g"][l][None]
    w["wg"] = p["w_gate"][l].astype(BF16)
    w["bg"] = p["b_gate"][l]
    w["wbr"] = p["w_br"][l].astype(BF16)
    w["wo"] = p["w_o"][l].astype(BF16)
    w["ln1"] = jnp.stack([p["ln1_g"][l], p["ln1_b"][l]])
    w["ln2"] = jnp.stack([p["ln2_g"][l], p["ln2_b"][l]])
    j = l // 2
    if l % 2 == 0:
        w["ffn"] = (p["ffn_w_gate"][j].astype(BF16), p["ffn_w_up"][j].astype(BF16),
                    p["ffn_w_down"][j].astype(BF16))
    else:
        wr = _pad_cols(p["router_w"][j], LANE)
        br = _pad_cols(p["router_b"][j][None], LANE)
        w["moe"] = (wr, br, p["moe_w_gate"][j].astype(BF16), p["moe_w_up"][j].astype(BF16),
                    p["moe_w_down"][j].astype(BF16))
    return w


def _mixer_consts():
    bdm = _np_block_mask(HEAD_DIM, HEAD_DIM)
    lg = np.log1p(-np.exp2(-5.0 - np.arange(N_HEADS, dtype=np.float64)))
    lg_l = np.repeat(lg, HEAD_DIM)[None, :]
    t = np.arange(CHUNK, dtype=np.float64)[:, None]
    s_side = (np.arange(N_HEADS * CHUNK) % CHUNK)[None, :].astype(np.float64)
    lg_side = np.repeat(lg, CHUNK)[None, :]
    d_mat = np.where(s_side <= t, np.exp((t - s_side) * lg_side), 0.0)
    dec = np.zeros((3 * CHUNK + SUBLANE, BR_WIDTH), np.float64)
    dec[0:CHUNK] = np.exp((t + 1.0) * lg_l)
    dec[CHUNK:2 * CHUNK] = np.exp((CHUNK - 1.0 - t) * lg_l)
    dec[2 * CHUNK:3 * CHUNK] = d_mat
    dec[3 * CHUNK] = np.exp(CHUNK * lg_l[0])
    return {
        "tril": jnp.asarray(np.tril(np.ones((CHUNK, CHUNK), np.float32)), BF16),
        "bdm": jnp.asarray(bdm, F32),
        "head_lanes": jnp.asarray(_np_head_lanes(HEAD_DIM), F32),
        "strict": jnp.asarray(_np_causal_side_by_side(True), F32),
        "incl": jnp.asarray(_np_causal_side_by_side(False), F32),
        "eye": jnp.asarray(np.tile(np.eye(CHUNK, dtype=np.float32), (1, N_HEADS)), F32),
        "ret_dec": jnp.asarray(dec, F32),
        "gla_pair": jnp.asarray(_np_block_mask(GLA_DK, HEAD_DIM), BF16),
        "gla_mask": jnp.asarray(_np_block_mask(HEAD_DIM, GLA_DK), F32),
        "gla_head_lanes": jnp.asarray(_np_head_lanes(GLA_DK), F32),
    }


def _rope_tables(pos0, t):
    half = HEAD_DIM // 2
    pos = pos0 + jnp.arange(t, dtype=F32)
    inv = ROPE_THETA ** (-jnp.arange(half, dtype=F32) / half)
    ang = pos[:, None] * inv[None]
    cos = jnp.cos(ang)
    sin = jnp.sin(ang)
    cos_t = jnp.tile(jnp.concatenate([cos, cos], axis=1), (1, N_HEADS))
    sin_t = jnp.tile(jnp.concatenate([-sin, sin], axis=1), (1, N_HEADS))
    return cos_t, sin_t


def _previous_rows(x, x_last, tm):
    b, t, d = x.shape
    per_seq = t // tm
    tails = x.reshape(b, per_seq, tm, d)[:, :, tm - 1, :]
    prev = jnp.concatenate([x_last[:, None, :], tails[:, :per_seq - 1, :]], axis=1)
    return prev.reshape(b * per_seq, 1, d)


def _run_trunk(x, pos0, s_rwkv, c_shift, s_ret, s_hgrn, s_gla, prm, layers, consts):
    b, t, d = x.shape
    n = b * t
    tm, tm_proj, tt, tm_moe, rows = _tile_sizes(b, t)
    cos_t, sin_t = _rope_tables(pos0, t)
    v_first = None
    new_rwkv, new_shift, new_ret, new_hgrn, new_gla = [], [], [], [], []
    for l in range(DEPTH):
        w = layers[l]
        x_in = x
        x_last = c_shift[l]
        p2, aux2 = _in_proj(x.reshape(n, d), _previous_rows(x, x_last, tm_proj), w["win"], w["wl1"],
                            w["wl1mu"], w["w2"], w["bias"], tm_proj)
        p3 = p2.reshape(b, t, IN_COLS)
        aux3 = aux2.reshape(b, t, AUX_COLS)
        pad = (-b) % SUBLANE
        x_last_p = jnp.concatenate([x_last, jnp.zeros((pad, d), F32)], axis=0) if pad else x_last
        rkv_last = _rows_matmul(x_last_p, w["win"][:, :RWKV_COLS])[:b, None, :]

        res = _rwkv_mixer(p3, aux3, v_first, rkv_last, _to_block_diag(s_rwkv[l]), w["rwkv_prm"], consts, tt)
        if v_first is None:
            o_rwkv, v_first, st_rwkv = res
        else:
            o_rwkv, st_rwkv = res
        o_ret, st_ret = _ret_mixer(p3, cos_t, sin_t, _to_block_diag(jnp.swapaxes(s_ret[l], -1, -2)),
                                   w["ret_prm"], consts, tt)
        o_hgrn, st_hgrn = _hgrn_mixer(p3, _to_block_diag(jnp.swapaxes(s_hgrn[l], -1, -2)),
                                      prm["hgrn_lb_logits"], w["hgrn_ng"], consts, l, tt)
        o_gla, st_gla = _gla_mixer(p3, aux3, _to_block_diag(jnp.swapaxes(s_gla[l], -1, -2)), w["gla_ng"],
                                   consts, tt)

        outs = [o.reshape(n, BR_WIDTH) for o in (o_rwkv, o_ret, o_hgrn, o_gla)]
        is_moe = l % 2 == 1
        merged = _merge(x.reshape(n, d), outs, w["wg"], w["bg"], w["wbr"], w["wo"], w["ln1"], tm, is_moe)
        if not is_moe:
            x1 = merged[0]
            x2 = _ffn(x1, *w["ffn"], w["ln2"], tm_moe)
        else:
            x1, x1b = merged
            wr, br, mg, mu_, md = w["moe"]
            tril_m = jnp.asarray(np.tril(np.ones((tm_moe, tm_moe), np.float32), -1), BF16)
            rank, wsel, cnt = _router(x1, wr, br, tril_m, tm_moe)
            rank_row = rank[:, :N_EXPERTS].T.reshape(N_EXPERTS, 1, n)
            per_sub = cnt[:, :tm_moe // MOE_SUB, :N_EXPERTS]
            ends = jnp.cumsum(per_sub, axis=1)
            x2 = _moe(x1, x1b, ends[:, -1, :].reshape(-1), (ends - per_sub).reshape(-1), ends.reshape(-1),
                      rank_row, rank, wsel, mg, mu_, md, w["ln2"], tm_moe, rows)
        x = x2.reshape(b, t, d)

        new_rwkv.append(_from_block_diag(st_rwkv, HEAD_DIM, HEAD_DIM))
        new_shift.append(x_in[:, -1])
        new_ret.append(jnp.swapaxes(_from_block_diag(st_ret, HEAD_DIM, HEAD_DIM), -1, -2))
        new_hgrn.append(jnp.swapaxes(_from_block_diag(st_hgrn, HEAD_DIM, HEAD_DIM), -1, -2))
        new_gla.append(jnp.swapaxes(_from_block_diag(st_gla, HEAD_DIM, GLA_DK), -1, -2))
    return (x, jnp.stack(new_rwkv), jnp.stack(new_shift), jnp.stack(new_ret), jnp.stack(new_hgrn),
            jnp.stack(new_gla))


def kernel(x_prompt, x_sample, state_rwkv, cache_shift, state_ret, state_hgrn, state_gla, w_in, rwkv_mu_rkv, rwkv_mu_x, rwkv_mu_v, rwkv_w0, rwkv_w1, rwkv_w2, rwkv_a0, rwkv_a1, rwkv_a2, rwkv_v0, rwkv_v1, rwkv_v2, rwkv_g1, rwkv_g2, rwkv_k_k, rwkv_k_a, rwkv_r_k, rwkv_ln_g, rwkv_ln_b, ret_gn_g, ret_gn_b, hgrn_lb_logits, hgrn_norm_g, gla_w1, gla_w2, gla_b, gla_norm_g, w_br, w_gate, b_gate, w_o, ln1_g, ln1_b, ln2_g, ln2_b, ffn_w_gate, ffn_w_up, ffn_w_down, router_w, router_b, moe_w_gate, moe_w_up, moe_w_down):
    prm = {
        'w_in': w_in, 'rwkv_mu_rkv': rwkv_mu_rkv, 'rwkv_mu_x': rwkv_mu_x, 'rwkv_mu_v': rwkv_mu_v,
        'rwkv_w0': rwkv_w0, 'rwkv_w1': rwkv_w1, 'rwkv_w2': rwkv_w2,
        'rwkv_a0': rwkv_a0, 'rwkv_a1': rwkv_a1, 'rwkv_a2': rwkv_a2,
        'rwkv_v0': rwkv_v0, 'rwkv_v1': rwkv_v1, 'rwkv_v2': rwkv_v2,
        'rwkv_g1': rwkv_g1, 'rwkv_g2': rwkv_g2, 'rwkv_k_k': rwkv_k_k, 'rwkv_k_a': rwkv_k_a,
        'rwkv_r_k': rwkv_r_k, 'rwkv_ln_g': rwkv_ln_g, 'rwkv_ln_b': rwkv_ln_b,
        'ret_gn_g': ret_gn_g, 'ret_gn_b': ret_gn_b, 'hgrn_lb_logits': hgrn_lb_logits,
        'hgrn_norm_g': hgrn_norm_g, 'gla_w1': gla_w1, 'gla_w2': gla_w2, 'gla_b': gla_b,
        'gla_norm_g': gla_norm_g, 'w_br': w_br, 'w_gate': w_gate, 'b_gate': b_gate, 'w_o': w_o,
        'ln1_g': ln1_g, 'ln1_b': ln1_b, 'ln2_g': ln2_g, 'ln2_b': ln2_b,
        'ffn_w_gate': ffn_w_gate, 'ffn_w_up': ffn_w_up, 'ffn_w_down': ffn_w_down,
        'router_w': router_w, 'router_b': router_b,
        'moe_w_gate': moe_w_gate, 'moe_w_up': moe_w_up, 'moe_w_down': moe_w_down,
    }
    layers = [_prep_layer(l, prm) for l in range(DEPTH)]
    consts = _mixer_consts()
    bp = x_prompt.shape[0]
    zero_hd = jnp.zeros((DEPTH, bp, N_HEADS, HEAD_DIM, HEAD_DIM), F32)
    zero_shift = jnp.zeros((DEPTH, bp, D_MODEL), F32)
    zero_gla = jnp.zeros((DEPTH, bp, N_HEADS, GLA_DK, HEAD_DIM), F32)
    prompt = _run_trunk(x_prompt, 0.0, zero_hd, zero_shift, zero_hd, zero_hd, zero_gla, prm, layers, consts)
    sample = _run_trunk(x_sample, float(PAST_LEN), state_rwkv, cache_shift, state_ret, state_hgrn,
                        state_gla, prm, layers, consts)
    y_p, p_rwkv, p_shift, p_ret, p_hgrn, p_gla = prompt
    y_s, s_rwkv, s_shift, s_ret, s_hgrn, s_gla = sample
    return (y_p, y_s, p_rwkv, p_shift, p_ret, p_hgrn, p_gla, s_rwkv, s_shift, s_ret, s_hgrn, s_gla)
```

```python
import functools
import math

import numpy as np
import jax
import jax.numpy as jnp
import jax.scipy.linalg
from jax import lax
from jax.experimental import pallas as pl
from jax.experimental.pallas import tpu as pltpu

F32 = jnp.float32
BF16 = jnp.bfloat16

D_MODEL = 1024
DEPTH = 2
PAST_LEN = 4096
CHUNK = 64
SUB = 16
N_BRANCH = 4
BR_WIDTH = D_MODEL // N_BRANCH
HEAD_DIM = 64
N_HEADS = BR_WIDTH // HEAD_DIM
GLA_DK = HEAD_DIM // 2
GLA_KW = N_HEADS * GLA_DK
GLA_GATE_RANK = 16
GLA_TAU = 16.0
RWKV_W_LORA = 32
RWKV_A_LORA = 32
RWKV_V_LORA = 16
RWKV_G_LORA = 64
RWKV_GN_EPS = 64e-5
ROPE_THETA = 10000.0
LN_EPS = 1e-5
D_FF = 2816
N_EXPERTS = 8
ALPHA = (2.0 * DEPTH) ** 0.25
RWKV_COLS = 3 * BR_WIDTH
IN_COLS = 3584
LORA_COLS = 256
AUX_COLS = 4 * BR_WIDTH + GLA_KW
SEQ_GROUP = 8
FACTOR_LIMIT = 80.0
MOE_LAST_SIZES = 5

MOE_SUB = 256
MOE_WIN = 112
MOE_TAIL = 128

LANE = 128
SUBLANE = 8
BF16_ROWS = 16
LOG2_E = 1.4426950408889634

NN = (((1,), (0,)), ((), ()))
NT = (((1,), (1,)), ((), ()))
TN = (((0,), (0,)), ((), ()))


def _params(sem, vmem_mib):
    return pltpu.CompilerParams(dimension_semantics=sem, vmem_limit_bytes=vmem_mib * 1024 * 1024)


def _const_spec(shape):
    nd = len(shape)
    return pl.BlockSpec(shape, lambda *_: (0,) * nd, pipeline_mode=pl.Buffered(1))


def _dot(a, b, dims=NN):
    return lax.dot_general(a.astype(BF16), b.astype(BF16), dims, preferred_element_type=F32)


def _split(x):
    hi = x.astype(BF16)
    lo = (x - hi.astype(F32)).astype(BF16)
    return hi, lo


def _dot_exact_lhs(a_bf, x):
    hi, lo = _split(x)
    return (jnp.dot(a_bf, hi, preferred_element_type=F32)
            + jnp.dot(a_bf, lo, preferred_element_type=F32))


def _dot_exact_rhs(x, b_bf):
    hi, lo = _split(x)
    return (jnp.dot(hi, b_bf, preferred_element_type=F32)
            + jnp.dot(lo, b_bf, preferred_element_type=F32))


def _dot3(a, b, dims=NN):
    ah, al = _split(a)
    bh, bl = _split(b)
    d = functools.partial(lax.dot_general, dimension_numbers=dims, preferred_element_type=F32)
    return d(ah, bh) + (d(ah, bl) + d(al, bh))


def _select_dot(x, sel_bf, dims, x_first):
    hi = x.astype(BF16)
    rest = x - hi.astype(F32)
    mid = rest.astype(BF16)
    lo = (rest - mid.astype(F32)).astype(BF16)
    d = functools.partial(lax.dot_general, dimension_numbers=dims, preferred_element_type=F32)
    out = None
    for piece in (hi, mid, lo):
        term = d(piece, sel_bf) if x_first else d(sel_bf, piece)
        out = term if out is None else out + term
    return out


def _state_from_value_rows(nat, unfold_bf, mask):
    return _select_dot(nat, unfold_bf, NN, True) * mask


def _state_to_value_rows(st, fold_bf):
    return _select_dot(st, fold_bf, NN, True)


def _state_from_key_rows(nat, fold_bf, mask):
    return _select_dot(nat, fold_bf, NT, False) * mask


def _state_to_key_rows(st, fold_bf):
    return _select_dot(st, fold_bf, TN, True)


def _sigmoid(x):
    return 1.0 / (1.0 + jnp.exp(-x))


def _softplus(x):
    return jnp.maximum(x, 0.0) + jnp.log(1.0 + jnp.exp(-jnp.abs(x)))


def _silu(x):
    return x * _sigmoid(x)


def _layer_norm_rows(y, g, b):
    mu = jnp.mean(y, axis=-1, keepdims=True)
    yc = y - mu
    var = jnp.mean(yc * yc, axis=-1, keepdims=True)
    return yc * lax.rsqrt(var + LN_EPS) * g + b


def _np_block_mask(rows_per_head, cols_per_head):
    r = np.arange(N_HEADS * rows_per_head)[:, None] // rows_per_head
    c = np.arange(N_HEADS * cols_per_head)[None, :] // cols_per_head
    return (r == c).astype(np.float32)


def _np_head_lanes(cols_per_head):
    m = np.zeros((SUBLANE, N_HEADS * cols_per_head), np.float32)
    for h in range(N_HEADS):
        m[h, h * cols_per_head:(h + 1) * cols_per_head] = 1.0
    return m


def _np_causal_side_by_side(strict):
    t = np.arange(CHUNK)[:, None]
    s = np.arange(N_HEADS * CHUNK)[None, :] % CHUNK
    return ((s < t) if strict else (s <= t)).astype(np.float32)


def _in_proj_kernel(x_ref, prev_ref, win_ref, wl1_ref, wl1mu_ref, w2_ref, bias_ref, p_ref, aux_ref):
    x = x_ref[...]
    row = lax.broadcasted_iota(jnp.int32, x.shape, 0)
    xx = jnp.where(row == 0, prev_ref[0], pltpu.roll(x, 1, axis=0)) - x
    xb = x.astype(BF16)
    h = (jnp.dot(xb, wl1_ref[...], preferred_element_type=F32)
         + jnp.dot(xx.astype(BF16), wl1mu_ref[...], preferred_element_type=F32))
    lane = lax.broadcasted_iota(jnp.int32, h.shape, 1)
    act = jnp.where(lane < RWKV_W_LORA, jnp.tanh(h), h)
    g_lo = RWKV_W_LORA + RWKV_A_LORA
    in_g = jnp.where(lane >= g_lo, jnp.where(lane < g_lo + RWKV_G_LORA, 1.0, 0.0), 0.0)
    act = jnp.where(in_g > 0.5, _sigmoid(h), act)
    z = jnp.dot(act.astype(BF16), w2_ref[...], preferred_element_type=F32) + bias_ref[...]
    p_ref[...] = jnp.dot(xb, win_ref[...], preferred_element_type=F32)
    bw = BR_WIDTH
    w_log = -_softplus(-z[:, 0:bw]) - 0.5
    aux_ref[:, 0:bw] = -jnp.exp(w_log)
    aux_ref[:, bw:2 * bw] = _sigmoid(z[:, bw:2 * bw])
    aux_ref[:, 2 * bw:3 * bw] = z[:, 2 * bw:3 * bw]
    aux_ref[:, 3 * bw:4 * bw] = _sigmoid(z[:, 3 * bw:4 * bw])
    zg = z[:, 4 * bw:]
    aux_ref[:, 4 * bw:] = (jnp.minimum(zg, 0.0) - jnp.log(1.0 + jnp.exp(-jnp.abs(zg)))) * (1.0 / GLA_TAU)


def _in_proj(x2, prev_rows, win, wl1, wl1mu, w2, bias, tm):
    n = x2.shape[0]
    return pl.pallas_call(
        _in_proj_kernel,
        grid=(n // tm,),
        in_specs=[
            pl.BlockSpec((tm, D_MODEL), lambda i: (i, 0)),
            pl.BlockSpec((1, 1, D_MODEL), lambda i: (i, 0, 0)),
            _const_spec(win.shape), _const_spec(wl1.shape), _const_spec(wl1mu.shape),
            _const_spec(w2.shape), _const_spec(bias.shape),
        ],
        out_specs=[pl.BlockSpec((tm, IN_COLS), lambda i: (i, 0)),
                   pl.BlockSpec((tm, AUX_COLS), lambda i: (i, 0))],
        out_shape=[jax.ShapeDtypeStruct((n, IN_COLS), F32), jax.ShapeDtypeStruct((n, AUX_COLS), F32)],
        compiler_params=_params(("arbitrary",), 56),
        name="in_proj",
    )(x2, prev_rows, win, wl1, wl1mu, w2, bias)


def _rows_matmul_kernel(x_ref, w_ref, o_ref):
    o_ref[...] = jnp.dot(x_ref[...].astype(BF16), w_ref[...], preferred_element_type=F32)


def _rows_matmul(x, w):
    return pl.pallas_call(
        _rows_matmul_kernel,
        out_shape=jax.ShapeDtypeStruct((x.shape[0], w.shape[1]), F32),
        name="rows_matmul",
    )(x, w)


def _stack_heads(x, head_lanes):
    xb = x.astype(BF16)
    return jnp.concatenate([xb * head_lanes[h:h + 1, :].astype(BF16) for h in range(N_HEADS)], axis=0)


def _head_sum(x, ones_bf):
    return _dot_exact_rhs(x, ones_bf)


def _head_layer_norm(y, ones_bf, g, b, eps):
    inv = 1.0 / HEAD_DIM
    mu = _head_sum(y, ones_bf) * inv
    yc = y - mu
    var = _head_sum(yc * yc, ones_bf) * inv
    return yc * lax.rsqrt(var + eps) * g + b


def _head_rms_norm(y, ones_bf, g):
    ms = _head_sum(y * y, ones_bf) * (1.0 / HEAD_DIM)
    return y * lax.rsqrt(ms + 1e-6) * g


def _pairwise_block(q, k, v, b2, pair_ones_bf):
    parts = []
    for j in range(SUB):
        lo = (j // SUBLANE) * SUBLANE
        p = q[lo:] * jnp.exp2(b2[lo:] - b2[j:j + 1]) * k[j:j + 1]
        if j % SUBLANE:
            rid = lax.broadcasted_iota(jnp.int32, p.shape, 0) + lo
            p = jnp.where(rid >= j, p, 0.0)
        parts.append(p)
    att = jnp.dot(jnp.concatenate(parts, axis=0).astype(BF16), pair_ones_bf, preferred_element_type=F32)
    outs = []
    off = 0
    for g in range(SUB // SUBLANE):
        rows = SUB - g * SUBLANE
        acc = None
        for j in range(g * SUBLANE, (g + 1) * SUBLANE):
            term = att[off:off + rows] * v[j:j + 1]
            acc = term if acc is None else acc + term
            off += rows
        if g:
            acc = jnp.concatenate([jnp.zeros((g * SUBLANE, v.shape[1]), F32), acc], axis=0)
        outs.append(acc)
    total = outs[0]
    for extra in outs[1:]:
        total = total + extra
    return total


def _gla_intra_pairwise(tiles, b, pair_ones_bf, lanes_k, lanes_v):
    n = range(len(tiles))
    q, k, v, glog = ([t[i] for t in tiles] for i in range(4))
    b2 = [b[i] * LOG2_E for i in n]
    blocks = [[] for _ in n]
    for blk in range(CHUNK // SUB):
        r0 = blk * SUB
        sl = slice(r0, r0 + SUB)
        o_blk = [_pairwise_block(q[i][sl], k[i][sl], v[i][sl], b2[i][sl], pair_ones_bf) for i in n]
        if blk:
            c0 = [b[i][r0 - 1:r0] for i in n]
            q_t = [q[i][sl] * jnp.exp(b[i][sl] - c0[i]) for i in n]
            k_t = [k[i][:r0] * jnp.exp(c0[i] - b[i][:r0]) for i in n]
            att = [_dot(q_t[i], _stack_heads(k_t[i], lanes_k), NT) for i in n]
            o_blk = [o_blk[i] + _dot(att[i], _stack_heads(v[i][:r0], lanes_v)) for i in n]
        for i in n:
            blocks[i].append(o_blk[i])
    return [jnp.concatenate(blocks[i], axis=0) for i in n]


def _gla_intra_factored(tiles, b, lanes_k, lanes_v):
    n = range(len(tiles))
    q, k, v, _ = ([t[i] for t in tiles] for i in range(4))
    blocks = [[] for _ in n]
    for blk in range(CHUNK // SUB):
        r0 = blk * SUB
        upto = r0 + SUB
        sl = slice(r0, upto)
        if blk:
            c0 = [b[i][r0 - 1:r0] for i in n]
            q_t = [q[i][sl] * jnp.exp(b[i][sl] - c0[i]) for i in n]
            k_t = [k[i][:upto] * jnp.exp(c0[i] - b[i][:upto]) for i in n]
        else:
            q_t = [q[i][sl] * jnp.exp(b[i][sl]) for i in n]
            k_t = [k[i][:upto] * jnp.exp(-b[i][:upto]) for i in n]
        att = [_dot(q_t[i], _stack_heads(k_t[i], lanes_k), NT) for i in n]
        t_id = lax.broadcasted_iota(jnp.int32, (SUB, N_HEADS * upto), 0) + r0
        s_id = lax.broadcasted_iota(jnp.int32, (SUB, N_HEADS * upto), 1) % upto
        causal = s_id <= t_id
        o_blk = [_dot(jnp.where(causal, att[i], 0.0), _stack_heads(v[i][:upto], lanes_v)) for i in n]
        for i in n:
            blocks[i].append(o_blk[i])
    return [jnp.concatenate(blocks[i], axis=0) for i in n]


def _gla_block_range(b):
    worst = -b[SUB - 1:SUB, :]
    for blk in range(1, CHUNK // SUB):
        r0 = blk * SUB
        worst = jnp.maximum(worst, b[r0 - 1:r0, :] - b[r0 + SUB - 1:r0 + SUB, :])
    return worst


def _gla_state_parts(tiles, b, st_mask):
    n = range(len(tiles))
    q, k, v, _ = ([t[i] for t in tiles] for i in range(4))
    b_last = [b[i][CHUNK - 1:CHUNK, :] for i in n]
    upd = [st_mask * _dot(v[i], k[i] * jnp.exp(b_last[i] - b[i]), TN) for i in n]
    return [(q[i] * jnp.exp(b[i]), upd[i], jnp.exp(b_last[i])) for i in n]


def _rwkv_state_free(tiles, tril_bf, hl, strict, incl, eye):
    n = range(len(tiles))
    r, k, v, kk, bv, lw = ([t[i] for t in tiles] for i in range(6))
    l = [_dot_exact_lhs(tril_bf, lw[i]) for i in n]
    l_last = [l[i][CHUNK - 1:CHUNK, :] for i in n]
    e_neg = [jnp.exp(-l[i]) for i in n]
    lhs = [jnp.concatenate([kk[i] * jnp.exp(l[i] - lw[i]), r[i] * jnp.exp(l[i])], axis=0) for i in n]
    rhs = [jnp.concatenate([_stack_heads(k[i] * e_neg[i], hl), _stack_heads(bv[i] * e_neg[i], hl)], axis=0)
           for i in n]
    amat = [_dot(lhs[i], rhs[i], NT) for i in n]
    w = N_HEADS * CHUNK
    a_ab = [amat[i][:CHUNK, w:] * strict for i in n]
    a_vk = [jnp.concatenate([amat[i][:CHUNK, :w] * strict, amat[i][CHUNK:, :w] * incl], axis=0) for i in n]
    a_rb = [amat[i][CHUNK:, w:] * incl for i in n]
    x = [eye + a_ab[i] for i in n]
    m = a_ab
    for _ in range(int(math.log2(CHUNK)) - 1):
        m_st = [_stack_heads(m[i], hl) for i in n]
        m = [_dot(m[i], m_st[i]) for i in n]
        m_st = [_stack_heads(m[i], hl) for i in n]
        x = [x[i] + _dot(x[i], m_st[i]) for i in n]
    from_v = [_dot(a_vk[i], _stack_heads(v[i], hl)) for i in n]
    e_end = [jnp.exp(l_last[i] - l[i]) for i in n]
    upd_v = [_dot(v[i], k[i] * e_end[i], TN) for i in n]
    b_end = [bv[i] * e_end[i] for i in n]
    st_dec = [jnp.exp(l_last[i]) for i in n]
    return [(lhs[i], x[i], a_rb[i], from_v[i], upd_v[i], b_end[i], st_dec[i]) for i in n]


def _rwkv_state_step(parts, sts, bdm, hl):
    n = range(len(parts))
    lhs, x, a_rb, from_v, upd_v, b_end, st_dec = ([p[i] for p in parts] for i in range(7))
    from_state = [_dot(lhs[i], sts[i], NT) for i in n]
    u = [_dot(x[i], _stack_heads(from_state[i][:CHUNK] + from_v[i][:CHUNK], hl)) for i in n]
    upd = [upd_v[i] + _dot(u[i], b_end[i], TN) for i in n]
    new = [sts[i] * st_dec[i] + upd[i] * bdm for i in n]
    y = [from_state[i][CHUNK:] + from_v[i][CHUNK:] + _dot(a_rb[i], _stack_heads(u[i], hl)) for i in n]
    return y, new


def _rwkv_kernel(has_vres, *refs):
    if has_vres:
        (rkv_ref, lw_ref, a_ref, g_ref, vg_ref, vf_ref, last_ref, st0_ref, prm_ref, tril_ref, bdm_ref,
         hl_ref, strict_ref, incl_ref, eye_ref, fold_ref, unfold_ref, o_ref, st_out_ref, st_sc,
         prev_sc) = refs
    else:
        (rkv_ref, lw_ref, a_ref, g_ref, last_ref, st0_ref, prm_ref, tril_ref, bdm_ref,
         hl_ref, strict_ref, incl_ref, eye_ref, fold_ref, unfold_ref, o_ref, v_out_ref, st_out_ref, st_sc,
         prev_sc) = refs
    tb = pl.program_id(1)

    @pl.when(tb == 0)
    def _():
        for s in range(st0_ref.shape[0]):
            st_sc[s] = _state_from_value_rows(st0_ref[s], unfold_ref[...], bdm_ref[...])
        prev_sc[...] = last_ref[...]

    bw = BR_WIDTH
    k_k = prm_ref[1:2, 0:bw]
    k_a = prm_ref[2:3, 0:bw]
    r_k = prm_ref[3:4, 0:bw]
    ln_g = prm_ref[4:5, 0:bw]
    ln_b = prm_ref[5:6, 0:bw]
    bdm = bdm_ref[...]
    ones_bf = bdm.astype(BF16)
    hl = hl_ref[...]
    tril_bf = tril_ref[...]
    strict = strict_ref[...]
    incl = incl_ref[...]
    eye = eye_ref[...]
    n_seq = rkv_ref.shape[0]
    tt = rkv_ref.shape[1]
    seqs = []
    for s in range(n_seq):
        rkv = rkv_ref[s]
        row = lax.broadcasted_iota(jnp.int32, rkv.shape, 0)
        prev = jnp.where(row == 0, prev_sc[s], pltpu.roll(rkv, 1, axis=0))
        prev_sc[s] = rkv[tt - 1:tt, :]
        mixed = rkv + (prev - rkv) * prm_ref[0:1, :]
        r = mixed[:, 0:bw]
        k = mixed[:, bw:2 * bw]
        v = mixed[:, 2 * bw:]
        a = a_ref[s]
        if has_vres:
            v = v + (vf_ref[s] - v) * vg_ref[s]
        else:
            v_out_ref[s] = v
        kk = k * k_k
        kk = kk * lax.rsqrt(jnp.maximum(_head_sum(kk * kk, ones_bf), 1e-24))
        k = k * (1.0 + (a - 1.0) * k_a)
        seqs.append((r, k, v, kk, -(kk * a), lw_ref[s]))
    n_chunks = tt // CHUNK
    tiles = [tuple(z[c * CHUNK:(c + 1) * CHUNK] for z in seqs[s])
             for c in range(n_chunks) for s in range(n_seq)]
    parts = _rwkv_state_free(tiles, tril_bf, hl, strict, incl, eye)
    sts = [st_sc[s] for s in range(n_seq)]
    ys = [[] for _ in range(n_seq)]
    for c in range(n_chunks):
        y_c, sts = _rwkv_state_step(parts[c * n_seq:(c + 1) * n_seq], sts, bdm, hl)
        for s in range(n_seq):
            ys[s].append(y_c[s])
    for s in range(n_seq):
        r, k, v = seqs[s][0:3]
        st_sc[s] = sts[s]
        y = ys[s][0] if len(ys[s]) == 1 else jnp.concatenate(ys[s], axis=0)
        y = _head_layer_norm(y, ones_bf, ln_g, ln_b, RWKV_GN_EPS)
        bonus = _head_sum(r * k * r_k, ones_bf) * v
        o_ref[s] = (y + bonus) * g_ref[s]

    @pl.when(tb == pl.num_programs(1) - 1)
    def _():
        for s in range(n_seq):
            st_out_ref[s] = _state_to_value_rows(sts[s], fold_ref[...])


def _rwkv_mixer(p3, aux3, v_first, rkv_last, st0, prm, consts, tt):
    b, t, _ = p3.shape
    bw = BR_WIDTH
    pb = min(b, SEQ_GROUP)
    has_vres = v_first is not None
    tok = lambda j: pl.BlockSpec((pb, tt, bw), lambda bi, ti, j=j: (bi, ti, j))
    in_specs = [pl.BlockSpec((pb, tt, RWKV_COLS), lambda bi, ti: (bi, ti, 0)),
                tok(0), tok(1), tok(2)]
    args = [p3, aux3, aux3, aux3]
    if has_vres:
        in_specs += [tok(3), pl.BlockSpec((pb, tt, bw), lambda bi, ti: (bi, ti, 0))]
        args += [aux3, v_first]
    st_spec = pl.BlockSpec((pb, bw, HEAD_DIM), lambda bi, ti: (bi, 0, 0))
    in_specs += [pl.BlockSpec((pb, 1, RWKV_COLS), lambda bi, ti: (bi, 0, 0)), st_spec,
                 _const_spec(prm.shape)]
    args += [rkv_last, st0, prm]
    for name in ("tril", "bdm", "head_lanes", "strict", "incl", "eye", "fold", "unfold"):
        in_specs.append(_const_spec(consts[name].shape))
        args.append(consts[name])
    seq = pl.BlockSpec((pb, tt, bw), lambda bi, ti: (bi, ti, 0))
    seq_shape = jax.ShapeDtypeStruct((b, t, bw), F32)
    st_shape = jax.ShapeDtypeStruct((b, bw, HEAD_DIM), F32)
    if has_vres:
        out_specs, out_shape = [seq, st_spec], [seq_shape, st_shape]
    else:
        out_specs, out_shape = [seq, seq, st_spec], [seq_shape, seq_shape, st_shape]
    return pl.pallas_call(
        functools.partial(_rwkv_kernel, has_vres),
        grid=(b // pb, t // tt),
        in_specs=in_specs, out_specs=out_specs, out_shape=out_shape,
        scratch_shapes=[pltpu.VMEM((pb, bw, bw), F32), pltpu.VMEM((pb, 1, RWKV_COLS), F32)],
        compiler_params=_params(("arbitrary", "arbitrary"), 48),
        name="rwkv_mixer",
    )(*args)


def _rot_half(z):
    w = z.shape[1]
    half = HEAD_DIM // 2
    lane = lax.broadcasted_iota(jnp.int32, z.shape, 1)
    first = (lane % HEAD_DIM) < half
    return jnp.where(first, pltpu.roll(z, w - half, axis=1), pltpu.roll(z, half, axis=1))


def _ret_kernel(q_ref, k_ref, v_ref, g_ref, cos_ref, sin_ref, st0_ref, prm_ref, dec_ref, bdm_ref, hl_ref,
                fold_ref, o_ref, st_out_ref, st_sc):
    tb = pl.program_id(1)

    @pl.when(tb == 0)
    def _():
        for s in range(st0_ref.shape[0]):
            st_sc[s] = _state_from_key_rows(st0_ref[s], fold_ref[...], bdm_ref[...])

    cos = cos_ref[...]
    sin = sin_ref[...]
    bdm = bdm_ref[...]
    hl = hl_ref[...]
    ones_bf = bdm.astype(BF16)
    q_dec = dec_ref[0:CHUNK, :]
    k_dec = dec_ref[CHUNK:2 * CHUNK, :]
    d_mat = dec_ref[2 * CHUNK:3 * CHUNK, :]
    s_dec = dec_ref[3 * CHUNK:3 * CHUNK + 1, :]
    n_seq = q_ref.shape[0]
    tt = q_ref.shape[1]
    qs, ks, vs = [], [], []
    for s in range(n_seq):
        q = q_ref[s]
        k = k_ref[s]
        qs.append(q * cos + _rot_half(q) * sin)
        ks.append((k * cos + _rot_half(k) * sin) * (HEAD_DIM ** -0.5))
        vs.append(v_ref[s])
    sts = [st_sc[s] for s in range(n_seq)]
    outs = [[] for _ in range(n_seq)]
    for c in range(tt // CHUNK):
        sl = slice(c * CHUNK, (c + 1) * CHUNK)
        for s in range(n_seq):
            qc, kc, vc = qs[s][sl], ks[s][sl], vs[s][sl]
            att = _dot(qc, _stack_heads(kc, hl), NT) * d_mat
            outs[s].append(_dot(qc * q_dec, sts[s], NT) + _dot(att, _stack_heads(vc, hl)))
            sts[s] = sts[s] * s_dec + bdm * _dot(vc, kc * k_dec, TN)
    for s in range(n_seq):
        st_sc[s] = sts[s]
        o = outs[s][0] if len(outs[s]) == 1 else jnp.concatenate(outs[s], axis=0)
        o = _head_layer_norm(o, ones_bf, prm_ref[0:1, :], prm_ref[1:2, :], LN_EPS)
        o_ref[s] = o * _silu(g_ref[s])

    @pl.when(tb == pl.num_programs(1) - 1)
    def _():
        for s in range(n_seq):
            st_out_ref[s] = _state_to_key_rows(sts[s], fold_ref[...])


def _ret_mixer(p3, cos_t, sin_t, st0, prm, consts, tt):
    b, t, _ = p3.shape
    bw = BR_WIDTH
    pb = min(b, SEQ_GROUP)
    base = RWKV_COLS // bw
    tok = lambda j: pl.BlockSpec((pb, tt, bw), lambda bi, ti, j=j: (bi, ti, base + j))
    tab = pl.BlockSpec((tt, bw), lambda bi, ti: (ti, 0))
    st_spec = pl.BlockSpec((pb, bw, HEAD_DIM), lambda bi, ti: (bi, 0, 0))
    dec, bdm, hl, fold = consts["ret_dec"], consts["bdm"], consts["head_lanes"], consts["fold"]
    return pl.pallas_call(
        _ret_kernel,
        grid=(b // pb, t // tt),
        in_specs=[tok(0), tok(1), tok(2), tok(3), tab, tab, st_spec, _const_spec(prm.shape),
                  _const_spec(dec.shape), _const_spec(bdm.shape), _const_spec(hl.shape),
                  _const_spec(fold.shape)],
        out_specs=[pl.BlockSpec((pb, tt, bw), lambda bi, ti: (bi, ti, 0)), st_spec],
        out_shape=[jax.ShapeDtypeStruct((b, t, bw), F32), jax.ShapeDtypeStruct((b, bw, HEAD_DIM), F32)],
        scratch_shapes=[pltpu.VMEM((pb, bw, bw), F32)],
        compiler_params=_params(("arbitrary", "arbitrary"), 40),
        name="ret_mixer",
    )(p3, p3, p3, p3, cos_t, sin_t, st0, prm, dec, bdm, hl, fold)


def _gated_mixer_tail(seqs, norm_g, st_sc, intra_sc, st_out_ref, o_ref, tril_bf, pair_ones_bf, st_mask,
                      lanes_k, lanes_v, ones_v_bf, fold_bf):
    n_seq = len(seqs)
    tt = seqs[0][0].shape[0]
    n_chunks = tt // CHUNK
    tiles = [tuple(z[c * CHUNK:(c + 1) * CHUNK] for z in seqs[s][0:4])
             for c in range(n_chunks) for s in range(n_seq)]
    b = [_dot_exact_lhs(tril_bf, t[3]) for t in tiles]
    worst = _gla_block_range(b[0])
    for b_i in b[1:]:
        worst = jnp.maximum(worst, _gla_block_range(b_i))
    risky = jnp.max(worst) > FACTOR_LIMIT

    def put_intra(intra):
        for idx, o_i in enumerate(intra):
            c_i, s_i = divmod(idx, n_seq)
            intra_sc[s_i, c_i * CHUNK:(c_i + 1) * CHUNK, :] = o_i

    @pl.when(risky)
    def _():
        put_intra(_gla_intra_pairwise(tiles, b, pair_ones_bf, lanes_k, lanes_v))

    @pl.when(jnp.logical_not(risky))
    def _():
        put_intra(_gla_intra_factored(tiles, b, lanes_k, lanes_v))

    parts = _gla_state_parts(tiles, b, st_mask)
    sts = [st_sc[s] for s in range(n_seq)]
    outs = [[] for _ in range(n_seq)]
    for c in range(n_chunks):
        for s in range(n_seq):
            q_dec, upd, dec = parts[c * n_seq + s]
            outs[s].append(intra_sc[s, c * CHUNK:(c + 1) * CHUNK, :] + _dot(q_dec, sts[s], NT))
            sts[s] = sts[s] * dec + upd
    for s in range(n_seq):
        st_sc[s] = sts[s]
        o = outs[s][0] if len(outs[s]) == 1 else jnp.concatenate(outs[s], axis=0)
        o_ref[s] = _head_rms_norm(o, ones_v_bf, norm_g) * _silu(seqs[s][4])

    @pl.when(pl.program_id(1) == pl.num_programs(1) - 1)
    def _():
        for s in range(n_seq):
            st_out_ref[s] = _state_to_key_rows(sts[s], fold_bf)


def _hgrn_kernel(layer, q_ref, f_ref, i_ref, g_ref, st0_ref, lbl_ref, ng_ref, tril_ref, bdm_ref, hl_ref,
                 fold_ref, o_ref, st_out_ref, st_sc, intra_sc):
    @pl.when(pl.program_id(1) == 0)
    def _():
        for s in range(st0_ref.shape[0]):
            st_sc[s] = _state_from_key_rows(st0_ref[s], fold_ref[...], bdm_ref[...])

    logits = lbl_ref[...]
    ex = jnp.exp(logits - jnp.max(logits, axis=0, keepdims=True))
    sm = ex / jnp.sum(ex, axis=0, keepdims=True)
    lb = jnp.zeros((1, BR_WIDTH), F32)
    for d in range(1, layer + 1):
        lb = lb + sm[d:d + 1, :]
    seqs = []
    for s in range(q_ref.shape[0]):
        fz = f_ref[s]
        f = lb + (1.0 - lb) * _sigmoid(fz)
        k = (1.0 - lb) * _sigmoid(-fz)
        seqs.append((_silu(q_ref[s]), k, i_ref[s], jnp.log(f), g_ref[s]))
    bdm = bdm_ref[...]
    ones_bf = bdm.astype(BF16)
    hl = hl_ref[...]
    _gated_mixer_tail(seqs, ng_ref[...], st_sc, intra_sc, st_out_ref, o_ref, tril_ref[...], ones_bf, bdm,
                      hl, hl, ones_bf, fold_ref[...])


def _hgrn_mixer(p3, st0, lb_logits, norm_g, consts, layer, tt):
    b, t, _ = p3.shape
    bw = BR_WIDTH
    pb = min(b, SEQ_GROUP)
    base = (RWKV_COLS + 4 * bw) // bw
    tok = lambda j: pl.BlockSpec((pb, tt, bw), lambda bi, ti, j=j: (bi, ti, base + j))
    st_spec = pl.BlockSpec((pb, bw, HEAD_DIM), lambda bi, ti: (bi, 0, 0))
    tril, bdm, hl, fold = consts["tril"], consts["bdm"], consts["head_lanes"], consts["fold"]
    return pl.pallas_call(
        functools.partial(_hgrn_kernel, layer),
        grid=(b // pb, t // tt),
        in_specs=[tok(0), tok(1), tok(2), tok(3), st_spec, _const_spec(lb_logits.shape),
                  _const_spec(norm_g.shape), _const_spec(tril.shape), _const_spec(bdm.shape),
                  _const_spec(hl.shape), _const_spec(fold.shape)],
        out_specs=[pl.BlockSpec((pb, tt, bw), lambda bi, ti: (bi, ti, 0)), st_spec],
        out_shape=[jax.ShapeDtypeStruct((b, t, bw), F32), jax.ShapeDtypeStruct((b, bw, HEAD_DIM), F32)],
        scratch_shapes=[pltpu.VMEM((pb, bw, bw), F32), pltpu.VMEM((pb, tt, bw), F32)],
        compiler_params=_params(("arbitrary", "arbitrary"), 48),
        name="hgrn_mixer",
    )(p3, p3, p3, p3, st0, lb_logits, norm_g, tril, bdm, hl, fold)


def _gla_kernel(q_ref, k_ref, v_ref, g_ref, la_ref, st0_ref, ng_ref, tril_ref, pair_ref, mask_ref,
                bdm_ref, hlk_ref, hlv_ref, fold_ref, o_ref, st_out_ref, st_sc, intra_sc):
    @pl.when(pl.program_id(1) == 0)
    def _():
        for s in range(st0_ref.shape[0]):
            st_sc[s] = _state_from_key_rows(st0_ref[s], fold_ref[...], mask_ref[...])

    seqs = [(q_ref[s] * (GLA_DK ** -0.5), k_ref[s], v_ref[s], la_ref[s], g_ref[s])
            for s in range(q_ref.shape[0])]
    _gated_mixer_tail(seqs, ng_ref[...], st_sc, intra_sc, st_out_ref, o_ref, tril_ref[...], pair_ref[...],
                      mask_ref[...], hlk_ref[...], hlv_ref[...], bdm_ref[...].astype(BF16), fold_ref[...])


def _gla_mixer(p3, aux3, st0, norm_g, consts, tt):
    b, t, _ = p3.shape
    bw, kw = BR_WIDTH, GLA_KW
    pb = min(b, SEQ_GROUP)
    gla0 = RWKV_COLS + 8 * bw
    st_spec = pl.BlockSpec((pb, kw, HEAD_DIM), lambda bi, ti: (bi, 0, 0))
    names = ("tril", "gla_pair", "gla_mask", "bdm", "gla_head_lanes", "head_lanes", "fold")
    return pl.pallas_call(
        _gla_kernel,
        grid=(b // pb, t // tt),
        in_specs=[pl.BlockSpec((pb, tt, kw), lambda bi, ti: (bi, ti, gla0 // kw)),
                  pl.BlockSpec((pb, tt, kw), lambda bi, ti: (bi, ti, gla0 // kw + 1)),
                  pl.BlockSpec((pb, tt, bw), lambda bi, ti: (bi, ti, (gla0 + 2 * kw) // bw)),
                  pl.BlockSpec((pb, tt, bw), lambda bi, ti: (bi, ti, (gla0 + 2 * kw) // bw + 1)),
                  pl.BlockSpec((pb, tt, kw), lambda bi, ti: (bi, ti, 4 * bw // kw)),
                  st_spec, _const_spec(norm_g.shape)] + [_const_spec(consts[nm].shape) for nm in names],
        out_specs=[pl.BlockSpec((pb, tt, bw), lambda bi, ti: (bi, ti, 0)), st_spec],
        out_shape=[jax.ShapeDtypeStruct((b, t, bw), F32), jax.ShapeDtypeStruct((b, kw, HEAD_DIM), F32)],
        scratch_shapes=[pltpu.VMEM((pb, bw, kw), F32), pltpu.VMEM((pb, tt, bw), F32)],
        compiler_params=_params(("arbitrary", "arbitrary"), 48),
        name="gla_mixer",
    )(p3, p3, p3, p3, aux3, st0, norm_g, *[consts[nm] for nm in names])


def _merge_kernel(emit_bf16, x_ref, o0_ref, o1_ref, o2_ref, o3_ref, wg_ref, bg_ref, wbr_ref, wo_ref,
                  ln_ref, *out_refs):
    branches = (o0_ref, o1_ref, o2_ref, o3_ref)
    half = x_ref.shape[0] // 2
    rows = [slice(0, half), slice(half, 2 * half)]
    x = [x_ref[r, :] for r in rows]
    xb = [v.astype(BF16) for v in x]
    merged = [None, None]
    for m in range(N_BRANCH):
        pre = [jnp.dot(xb[h], wg_ref[m], preferred_element_type=F32) for h in range(2)]
        proj = [jnp.dot(branches[m][rows[h], :].astype(BF16), wbr_ref[m], preferred_element_type=F32)
                for h in range(2)]
        for h in range(2):
            term = _sigmoid(pre[h] + bg_ref[m:m + 1, :]) * proj[h]
            merged[h] = term if merged[h] is None else merged[h] + term
    y = [ALPHA * x[h] + jnp.dot(merged[h].astype(BF16), wo_ref[...], preferred_element_type=F32)
         for h in range(2)]
    for h in range(2):
        yn = _layer_norm_rows(y[h], ln_ref[0:1, :], ln_ref[1:2, :])
        out_refs[0][rows[h], :] = yn
        if emit_bf16:
            out_refs[1][rows[h], :] = yn.astype(BF16)


def _merge(x2, outs, wg, bg, wbr, wo, ln, tm, emit_bf16):
    n = x2.shape[0]
    row = pl.BlockSpec((tm, D_MODEL), lambda i: (i, 0))
    br = pl.BlockSpec((tm, BR_WIDTH), lambda i: (i, 0))
    out_specs = [row]
    out_shape = [jax.ShapeDtypeStruct((n, D_MODEL), F32)]
    if emit_bf16:
        out_specs.append(row)
        out_shape.append(jax.ShapeDtypeStruct((n, D_MODEL), BF16))
    return pl.pallas_call(
        functools.partial(_merge_kernel, emit_bf16),
        grid=(n // tm,),
        in_specs=[row, br, br, br, br, _const_spec(wg.shape), _const_spec(bg.shape),
                  _const_spec(wbr.shape), _const_spec(wo.shape), _const_spec(ln.shape)],
        out_specs=out_specs, out_shape=out_shape,
        compiler_params=_params(("arbitrary",), 56),
        name="merge",
    )(x2, *outs, wg, bg, wbr, wo, ln)


FF_SPLIT = 2
FF_PART = D_FF // FF_SPLIT
FFN_PARTS = (1024, 1024, 768)


def _ffn_kernel(x_ref, wg_ref, wu_ref, wd_ref, ln_ref, o_ref):
    x = x_ref[...]
    xb = x.astype(BF16)
    acc = ALPHA * x
    lo = 0
    for width in FFN_PARTS:
        cs = slice(lo, lo + width)
        lo += width
        h = (_silu(jnp.dot(xb, wg_ref[:, cs], preferred_element_type=F32))
             * jnp.dot(xb, wu_ref[:, cs], preferred_element_type=F32))
        acc = acc + jnp.dot(h.astype(BF16), wd_ref[cs, :], preferred_element_type=F32)
    o_ref[...] = _layer_norm_rows(acc, ln_ref[0:1, :], ln_ref[1:2, :])


def _ffn(x2, wg, wu, wd, ln, tm):
    n = x2.shape[0]
    row = pl.BlockSpec((tm, D_MODEL), lambda i: (i, 0))
    return pl.pallas_call(
        _ffn_kernel,
        grid=(n // tm,),
        in_specs=[row, _const_spec(wg.shape), _const_spec(wu.shape), _const_spec(wd.shape),
                  _const_spec(ln.shape)],
        out_specs=row,
        out_shape=jax.ShapeDtypeStruct((n, D_MODEL), F32),
        compiler_params=_params(("arbitrary",), 60),
        name="ffn",
    )(x2, wg, wu, wd, ln)


def _router_kernel(x_ref, wr_ref, br_ref, tril_ref, rank_ref, wsel_ref, cnt_ref):
    logits = _dot3(x_ref[...], wr_ref[...]) + br_ref[...]
    lane = lax.broadcasted_iota(jnp.int32, logits.shape, 1)
    neg = jnp.float32(-jnp.inf)
    logits = jnp.where(lane < N_EXPERTS, logits, neg)
    m1 = jnp.max(logits, axis=1, keepdims=True)
    lane_f = lane.astype(F32)
    i1 = jnp.min(jnp.where(logits == m1, lane_f, float(LANE)), axis=1, keepdims=True)
    first = lane_f == i1
    rest = jnp.where(first, neg, logits)
    m2 = jnp.max(rest, axis=1, keepdims=True)
    i2 = jnp.min(jnp.where(rest == m2, lane_f, float(LANE)), axis=1, keepdims=True)
    second = lane_f == i2
    e = jnp.exp(m2 - m1)
    w1 = 1.0 / (1.0 + e)
    w2 = e / (1.0 + e)
    sel = jnp.where(first, 1.0, jnp.where(second, 1.0, 0.0))
    wsel_ref[...] = jnp.where(first, w1, jnp.where(second, w2, 0.0))
    sel_bf = sel.astype(BF16)
    rank = jnp.dot(tril_ref[...], sel_bf, preferred_element_type=F32)
    rank_ref[...] = jnp.where(sel > 0.5, rank, -1.0)
    ones = jnp.ones((SUBLANE, MOE_SUB), BF16)
    row = lax.broadcasted_iota(jnp.int32, (SUBLANE, LANE), 0)
    cnt = jnp.zeros((SUBLANE, LANE), F32)
    for s in range(sel.shape[0] // MOE_SUB):
        part = jnp.dot(ones, sel_bf[s * MOE_SUB:(s + 1) * MOE_SUB, :], preferred_element_type=F32)
        cnt = jnp.where(row == s, part, cnt)
    cnt_ref[0] = cnt.astype(jnp.int32)


def _router(x2, wr, br, tril, tm):
    n = x2.shape[0]
    nt = n // tm
    col = pl.BlockSpec((tm, LANE), lambda i: (i, 0))
    return pl.pallas_call(
        _router_kernel,
        grid=(nt,),
        in_specs=[pl.BlockSpec((tm, D_MODEL), lambda i: (i, 0)), _const_spec(wr.shape),
                  _const_spec(br.shape), _const_spec(tril.shape)],
        out_specs=[col, col, pl.BlockSpec((1, SUBLANE, LANE), lambda i: (i, 0, 0))],
        out_shape=[jax.ShapeDtypeStruct((n, LANE), F32), jax.ShapeDtypeStruct((n, LANE), F32),
                   jax.ShapeDtypeStruct((nt, SUBLANE, LANE), jnp.int32)],
        compiler_params=_params(("arbitrary",), 40),
        name="router",
    )(x2, wr, br, tril)


def _moe_kernel(rows, cnt_ref, off_ref, end_ref, x_ref, xb_ref, rrow_ref, rank_ref, wsel_ref, wg_ref, wu_ref,
                wd_ref, ln_ref, o_ref, xg_sc, yb_sc):
    i = pl.program_id(0)
    e = pl.program_id(1)
    c = pl.program_id(2)
    n_e = pl.num_programs(1)
    n_c = pl.num_programs(2)
    cnt = cnt_ref[i * N_EXPERTS + e]
    n_blk = (cnt + rows - 1) // rows
    tm = xb_ref.shape[0]
    n_sub = tm // MOE_SUB

    def windows(s):
        idx = (i * n_sub + s) * N_EXPERTS + e
        off = off_ref[idx]
        end = end_ref[idx]
        a0 = (off // BF16_ROWS) * BF16_ROWS
        n_win = jnp.where(end > off, (end - a0 + MOE_WIN - 1) // MOE_WIN, 0)
        return a0, n_win

    @pl.when((e == 0) & (c == 0))
    def _():
        o_ref[...] = ALPHA * x_ref[...]

    toks = [slice(s * MOE_SUB, (s + 1) * MOE_SUB) for s in range(n_sub)]

    def gather_piece(s, r0):
        slot = (lax.broadcasted_iota(jnp.int32, (MOE_WIN, MOE_SUB), 0) + r0).astype(F32)
        onehot = jnp.where(rrow_ref[0, :, toks[s]] == slot, 1.0, 0.0).astype(BF16)
        return jnp.dot(onehot, xb_ref[toks[s], :], preferred_element_type=F32)

    def gather_add(r0, piece):
        cur = xg_sc[pl.ds(r0, MOE_WIN), :].astype(F32)
        xg_sc[pl.ds(r0, MOE_WIN), :] = (cur + piece).astype(BF16)

    @pl.when(c == 0)
    def _():
        xg_sc[...] = jnp.zeros(xg_sc.shape, xg_sc.dtype)
        wins = [windows(s) for s in range(n_sub)]
        starts = [pl.multiple_of(a0, BF16_ROWS) for a0, _ in wins]
        pieces = [gather_piece(s, starts[s]) for s in range(n_sub)]
        for s in range(n_sub):
            gather_add(starts[s], pieces[s])
        for s in range(n_sub):
            a0, n_win = wins[s]

            def more(wi, carry, s=s, a0=a0):
                r0 = pl.multiple_of(a0 + wi * MOE_WIN, BF16_ROWS)
                gather_add(r0, gather_piece(s, r0))
                return carry

            lax.fori_loop(1, n_win, more, 0)

    def expert_rows(r0, m):
        xg = xg_sc[pl.ds(r0, m), :]
        h = (_silu(jnp.dot(xg, wg_ref[0], preferred_element_type=F32))
             * jnp.dot(xg, wu_ref[0], preferred_element_type=F32))
        yb = jnp.dot(h.astype(BF16), wd_ref[0], preferred_element_type=F32)

        @pl.when(c == 0)
        def _():
            yb_sc[pl.ds(r0, m), :] = yb

        @pl.when(c > 0)
        def _():
            yb_sc[pl.ds(r0, m), :] = yb_sc[pl.ds(r0, m), :] + yb

    def full_block(blk, carry):
        expert_rows(pl.multiple_of(blk * rows, BF16_ROWS), rows)
        return carry

    lax.fori_loop(0, n_blk - 1, full_block, 0)
    last = pl.multiple_of(jnp.maximum(n_blk - 1, 0) * rows, BF16_ROWS)
    rem = cnt - last
    sizes = [rows - k * BF16_ROWS for k in range(MOE_LAST_SIZES - 1, -1, -1)]
    used = jnp.int32(0)
    lower = 0
    for m in sizes:
        fits = (rem > lower) & (rem <= m)
        pl.when(fits)(functools.partial(expert_rows, last, m))
        used = jnp.where(fits, m, used)
        lower = m

    @pl.when(c == 0)
    def _():
        tail = pl.multiple_of(last + used, BF16_ROWS)
        yb_sc[pl.ds(tail, MOE_TAIL), :] = jnp.zeros((MOE_TAIL, D_MODEL), F32)

    @pl.when(c == n_c - 1)
    def _():
        mine = lax.broadcasted_iota(jnp.int32, (tm, LANE), 1) == e
        rank_col = jnp.sum(jnp.where(mine, rank_ref[...], 0.0), axis=1, keepdims=True)
        w_col = jnp.sum(jnp.where(mine, wsel_ref[...], 0.0), axis=1, keepdims=True)

        def scatter_piece(s, r0):
            slot = (lax.broadcasted_iota(jnp.int32, (MOE_SUB, MOE_WIN), 1) + r0).astype(F32)
            onehot = jnp.where(rank_col[toks[s]] == slot, 1.0, 0.0).astype(BF16)
            return jnp.dot(onehot, yb_sc[pl.ds(r0, MOE_WIN), :].astype(BF16), preferred_element_type=F32)

        wins = [windows(s) for s in range(n_sub)]
        starts = [pl.multiple_of(a0, BF16_ROWS) for a0, _ in wins]
        backs = [scatter_piece(s, starts[s]) for s in range(n_sub)]
        for s in range(n_sub):
            o_ref[toks[s], :] = o_ref[toks[s], :] + w_col[toks[s]] * backs[s]
        for s in range(n_sub):
            a0, n_win = wins[s]

            def more(wi, carry, s=s, a0=a0):
                r0 = pl.multiple_of(a0 + wi * MOE_WIN, BF16_ROWS)
                o_ref[toks[s], :] = o_ref[toks[s], :] + w_col[toks[s]] * scatter_piece(s, r0)
                return carry

            lax.fori_loop(1, n_win, more, 0)

    @pl.when((e == n_e - 1) & (c == n_c - 1))
    def _():
        o_ref[...] = _layer_norm_rows(o_ref[...], ln_ref[0:1, :], ln_ref[1:2, :])


def _moe(x2, xb2, counts, offs, ends, rank_row, rank, wsel, wg, wu, wd, ln, tm, rows):
    n = x2.shape[0]
    nt = n // tm
    cap = -(-tm // rows) * rows + MOE_TAIL
    tile = lambda i, e, c, *_: (i, 0)
    grid_spec = pltpu.PrefetchScalarGridSpec(
        num_scalar_prefetch=3,
        grid=(nt, N_EXPERTS, FF_SPLIT),
        in_specs=[
            pl.BlockSpec((tm, D_MODEL), tile),
            pl.BlockSpec((tm, D_MODEL), tile),
            pl.BlockSpec((1, 1, tm), lambda i, e, c, *_: (e, 0, i)),
            pl.BlockSpec((tm, LANE), tile),
            pl.BlockSpec((tm, LANE), tile),
            pl.BlockSpec((1, D_MODEL, FF_PART), lambda i, e, c, *_: (e, 0, c)),
            pl.BlockSpec((1, D_MODEL, FF_PART), lambda i, e, c, *_: (e, 0, c)),
            pl.BlockSpec((1, FF_PART, D_MODEL), lambda i, e, c, *_: (e, c, 0)),
            pl.BlockSpec((2, D_MODEL), lambda i, e, c, *_: (0, 0)),
        ],
        out_specs=pl.BlockSpec((tm, D_MODEL), tile),
        scratch_shapes=[pltpu.VMEM((cap, D_MODEL), BF16), pltpu.VMEM((cap, D_MODEL), F32)],
    )
    return pl.pallas_call(
        functools.partial(_moe_kernel, rows),
        grid_spec=grid_spec,
        out_shape=jax.ShapeDtypeStruct((n, D_MODEL), F32),
        compiler_params=_params(("arbitrary", "arbitrary", "arbitrary"), 56),
        name="moe",
    )(counts, offs, ends, x2, xb2, rank_row, rank, wsel, wg, wu, wd, ln)


def _tile_sizes(b, t):
    n = b * t
    tm = min(512, n)
    tm_proj = min(512, t)
    tt = min(512, t)
    tm_moe = min(1024, n)
    rows = 288 if tm_moe == 1024 else 160
    return tm, tm_proj, tt, tm_moe, rows


def _head_rows(s):
    b, h, r, c = s.shape
    return s.reshape(b, h * r, c)


def _pad_cols(a, width):
    return jnp.pad(a, ((0, 0), (0, width - a.shape[1])))


def _prep_layer(l, p):
    d = D_MODEL
    bw = BR_WIDTH
    w = {}
    w["win"] = p["w_in"][l].astype(BF16)
    mu = p["rwkv_mu_x"][l]
    if l >= 1:
        v1, v2, v0, mu_v = p["rwkv_v1"][l - 1], p["rwkv_v2"][l - 1], p["rwkv_v0"][l - 1], p["rwkv_mu_v"][l - 1]
    else:
        v1, v2 = jnp.zeros((d, RWKV_V_LORA), F32), jnp.zeros((RWKV_V_LORA, bw), F32)
        v0, mu_v = jnp.zeros((bw,), F32), jnp.zeros((d,), F32)
    first = [p["rwkv_w1"][l], p["rwkv_a1"][l], p["rwkv_g1"][l], v1, p["gla_w1"][l]]
    shift_mu = [mu[0], mu[1], mu[2], mu_v, jnp.zeros((d,), F32)]
    w["wl1"] = _pad_cols(jnp.concatenate(first, axis=1), LORA_COLS).astype(BF16)
    w["wl1mu"] = _pad_cols(jnp.concatenate([m[:, None] * a for m, a in zip(shift_mu, first)], axis=1),
                           LORA_COLS).astype(BF16)
    second = jax.scipy.linalg.block_diag(p["rwkv_w2"][l], p["rwkv_a2"][l], p["rwkv_g2"][l], v2, p["gla_w2"][l])
    w["w2"] = jnp.pad(second, ((0, LORA_COLS - second.shape[0]), (0, 0))).astype(BF16)
    w["bias"] = jnp.concatenate([p["rwkv_w0"][l], p["rwkv_a0"][l], jnp.zeros((bw,), F32), v0,
                                 p["gla_b"][l]])[None]
    rows = [p["rwkv_mu_rkv"][l].reshape(RWKV_COLS)]
    rows += [jnp.pad(p[name][l], (0, RWKV_COLS - bw))
             for name in ("rwkv_k_k", "rwkv_k_a", "rwkv_r_k", "rwkv_ln_g", "rwkv_ln_b")]
    rows += [jnp.zeros((RWKV_COLS,), F32)] * (SUBLANE - len(rows))
    w["rwkv_prm"] = jnp.stack(rows)
    w["ret_prm"] = jnp.stack([p["ret_gn_g"][l], p["ret_gn_b"][l]])
    w["hgrn_ng"] = p["hgrn_norm_g"][l][None]
    w["gla_ng"] = p["gla_norm_g"][l][None]
    w["wg"] = p["w_gate"][l].astype(BF16)
    w["bg"] = p["b_gate"][l]
    w["wbr"] = p["w_br"][l].astype(BF16)
    w["wo"] = p["w_o"][l].astype(BF16)
    w["ln1"] = jnp.stack([p["ln1_g"][l], p["ln1_b"][l]])
    w["ln2"] = jnp.stack([p["ln2_g"][l], p["ln2_b"][l]])
    j = l // 2
    if l % 2 == 0:
        w["ffn"] = (p["ffn_w_gate"][j].astype(BF16), p["ffn_w_up"][j].astype(BF16),
                    p["ffn_w_down"][j].astype(BF16))
    else:
        wr = _pad_cols(p["router_w"][j], LANE)
        br = _pad_cols(p["router_b"][j][None], LANE)
        w["moe"] = (wr, br, p["moe_w_gate"][j].astype(BF16), p["moe_w_up"][j].astype(BF16),
                    p["moe_w_down"][j].astype(BF16))
    return w


def _mixer_consts():
    bdm = _np_block_mask(HEAD_DIM, HEAD_DIM)
    lg = np.log1p(-np.exp2(-5.0 - np.arange(N_HEADS, dtype=np.float64)))
    lg_l = np.repeat(lg, HEAD_DIM)[None, :]
    t = np.arange(CHUNK, dtype=np.float64)[:, None]
    s_side = (np.arange(N_HEADS * CHUNK) % CHUNK)[None, :].astype(np.float64)
    lg_side = np.repeat(lg, CHUNK)[None, :]
    d_mat = np.where(s_side <= t, np.exp((t - s_side) * lg_side), 0.0)
    dec = np.zeros((3 * CHUNK + SUBLANE, BR_WIDTH), np.float64)
    dec[0:CHUNK] = np.exp((t + 1.0) * lg_l)
    dec[CHUNK:2 * CHUNK] = np.exp((CHUNK - 1.0 - t) * lg_l)
    dec[2 * CHUNK:3 * CHUNK] = d_mat
    dec[3 * CHUNK] = np.exp(CHUNK * lg_l[0])
    return {
        "tril": jnp.asarray(np.tril(np.ones((CHUNK, CHUNK), np.float32)), BF16),
        "bdm": jnp.asarray(bdm, F32),
        "head_lanes": jnp.asarray(_np_head_lanes(HEAD_DIM), F32),
        "strict": jnp.asarray(_np_causal_side_by_side(True), F32),
        "incl": jnp.asarray(_np_causal_side_by_side(False), F32),
        "eye": jnp.asarray(np.tile(np.eye(CHUNK, dtype=np.float32), (1, N_HEADS)), F32),
        "ret_dec": jnp.asarray(dec, F32),
        "gla_pair": jnp.asarray(_np_block_mask(GLA_DK, HEAD_DIM), BF16),
        "gla_mask": jnp.asarray(_np_block_mask(HEAD_DIM, GLA_DK), F32),
        "gla_head_lanes": jnp.asarray(_np_head_lanes(GLA_DK), F32),
        "fold": jnp.asarray(np.tile(np.eye(HEAD_DIM, dtype=np.float32), (N_HEADS, 1)), BF16),
        "unfold": jnp.asarray(np.tile(np.eye(HEAD_DIM, dtype=np.float32), (1, N_HEADS)), BF16),
    }


def _rope_tables(pos0, t):
    half = HEAD_DIM // 2
    pos = pos0 + jnp.arange(t, dtype=F32)
    inv = ROPE_THETA ** (-jnp.arange(half, dtype=F32) / half)
    ang = pos[:, None] * inv[None]
    cos = jnp.cos(ang)
    sin = jnp.sin(ang)
    cos_t = jnp.tile(jnp.concatenate([cos, cos], axis=1), (1, N_HEADS))
    sin_t = jnp.tile(jnp.concatenate([-sin, sin], axis=1), (1, N_HEADS))
    return cos_t, sin_t


def _previous_rows(x, x_last, tm):
    b, t, d = x.shape
    per_seq = t // tm
    tails = x.reshape(b, per_seq, tm, d)[:, :, tm - 1, :]
    prev = jnp.concatenate([x_last[:, None, :], tails[:, :per_seq - 1, :]], axis=1)
    return prev.reshape(b * per_seq, 1, d)


def _run_trunk(x, pos0, s_rwkv, c_shift, s_ret, s_hgrn, s_gla, prm, layers, consts):
    b, t, d = x.shape
    n = b * t
    tm, tm_proj, tt, tm_moe, rows = _tile_sizes(b, t)
    cos_t, sin_t = _rope_tables(pos0, t)
    v_first = None
    new_rwkv, new_shift, new_ret, new_hgrn, new_gla = [], [], [], [], []
    for l in range(DEPTH):
        w = layers[l]
        x_in = x
        x_last = c_shift[l]
        p2, aux2 = _in_proj(x.reshape(n, d), _previous_rows(x, x_last, tm_proj), w["win"], w["wl1"],
                            w["wl1mu"], w["w2"], w["bias"], tm_proj)
        p3 = p2.reshape(b, t, IN_COLS)
        aux3 = aux2.reshape(b, t, AUX_COLS)
        pad = (-b) % SUBLANE
        x_last_p = jnp.concatenate([x_last, jnp.zeros((pad, d), F32)], axis=0) if pad else x_last
        rkv_last = _rows_matmul(x_last_p, w["win"][:, :RWKV_COLS])[:b, None, :]

        res = _rwkv_mixer(p3, aux3, v_first, rkv_last, _head_rows(s_rwkv[l]), w["rwkv_prm"], consts, tt)
        if v_first is None:
            o_rwkv, v_first, st_rwkv = res
        else:
            o_rwkv, st_rwkv = res
        o_ret, st_ret = _ret_mixer(p3, cos_t, sin_t, _head_rows(s_ret[l]), w["ret_prm"], consts, tt)
        o_hgrn, st_hgrn = _hgrn_mixer(p3, _head_rows(s_hgrn[l]), prm["hgrn_lb_logits"], w["hgrn_ng"],
                                      consts, l, tt)
        o_gla, st_gla = _gla_mixer(p3, aux3, _head_rows(s_gla[l]), w["gla_ng"], consts, tt)

        outs = [o.reshape(n, BR_WIDTH) for o in (o_rwkv, o_ret, o_hgrn, o_gla)]
        is_moe = l % 2 == 1
        merged = _merge(x.reshape(n, d), outs, w["wg"], w["bg"], w["wbr"], w["wo"], w["ln1"], tm, is_moe)
        if not is_moe:
            x1 = merged[0]
            x2 = _ffn(x1, *w["ffn"], w["ln2"], tm_moe)
        else:
            x1, x1b = merged
            wr, br, mg, mu_, md = w["moe"]
            tril_m = jnp.asarray(np.tril(np.ones((tm_moe, tm_moe), np.float32), -1), BF16)
            rank, wsel, cnt = _router(x1, wr, br, tril_m, tm_moe)
            rank_row = rank[:, :N_EXPERTS].T.reshape(N_EXPERTS, 1, n)
            per_sub = cnt[:, :tm_moe // MOE_SUB, :N_EXPERTS]
            ends = jnp.cumsum(per_sub, axis=1)
            x2 = _moe(x1, x1b, ends[:, -1, :].reshape(-1), (ends - per_sub).reshape(-1), ends.reshape(-1),
                      rank_row, rank, wsel, mg, mu_, md, w["ln2"], tm_moe, rows)
        x = x2.reshape(b, t, d)

        new_rwkv.append(st_rwkv.reshape(b, N_HEADS, HEAD_DIM, HEAD_DIM))
        new_shift.append(x_in[:, -1])
        new_ret.append(st_ret.reshape(b, N_HEADS, HEAD_DIM, HEAD_DIM))
        new_hgrn.append(st_hgrn.reshape(b, N_HEADS, HEAD_DIM, HEAD_DIM))
        new_gla.append(st_gla.reshape(b, N_HEADS, GLA_DK, HEAD_DIM))
    return (x, jnp.stack(new_rwkv), jnp.stack(new_shift), jnp.stack(new_ret), jnp.stack(new_hgrn),
            jnp.stack(new_gla))


def kernel(x_prompt, x_sample, state_rwkv, cache_shift, state_ret, state_hgrn, state_gla, w_in, rwkv_mu_rkv, rwkv_mu_x, rwkv_mu_v, rwkv_w0, rwkv_w1, rwkv_w2, rwkv_a0, rwkv_a1, rwkv_a2, rwkv_v0, rwkv_v1, rwkv_v2, rwkv_g1, rwkv_g2, rwkv_k_k, rwkv_k_a, rwkv_r_k, rwkv_ln_g, rwkv_ln_b, ret_gn_g, ret_gn_b, hgrn_lb_logits, hgrn_norm_g, gla_w1, gla_w2, gla_b, gla_norm_g, w_br, w_gate, b_gate, w_o, ln1_g, ln1_b, ln2_g, ln2_b, ffn_w_gate, ffn_w_up, ffn_w_down, router_w, router_b, moe_w_gate, moe_w_up, moe_w_down):
    prm = {
        'w_in': w_in, 'rwkv_mu_rkv': rwkv_mu_rkv, 'rwkv_mu_x': rwkv_mu_x, 'rwkv_mu_v': rwkv_mu_v,
        'rwkv_w0': rwkv_w0, 'rwkv_w1': rwkv_w1, 'rwkv_w2': rwkv_w2,
        'rwkv_a0': rwkv_a0, 'rwkv_a1': rwkv_a1, 'rwkv_a2': rwkv_a2,
        'rwkv_v0': rwkv_v0, 'rwkv_v1': rwkv_v1, 'rwkv_v2': rwkv_v2,
        'rwkv_g1': rwkv_g1, 'rwkv_g2': rwkv_g2, 'rwkv_k_k': rwkv_k_k, 'rwkv_k_a': rwkv_k_a,
        'rwkv_r_k': rwkv_r_k, 'rwkv_ln_g': rwkv_ln_g, 'rwkv_ln_b': rwkv_ln_b,
        'ret_gn_g': ret_gn_g, 'ret_gn_b': ret_gn_b, 'hgrn_lb_logits': hgrn_lb_logits,
        'hgrn_norm_g': hgrn_norm_g, 'gla_w1': gla_w1, 'gla_w2': gla_w2, 'gla_b': gla_b,
        'gla_norm_g': gla_norm_g, 'w_br': w_br, 'w_gate': w_gate, 'b_gate': b_gate, 'w_o': w_o,
        'ln1_g': ln1_g, 'ln1_b': ln1_b, 'ln2_g': ln2_g, 'ln2_b': ln2_b,
        'ffn_w_gate': ffn_w_gate, 'ffn_w_up': ffn_w_up, 'ffn_w_down': ffn_w_down,
        'router_w': router_w, 'router_b': router_b,
        'moe_w_gate': moe_w_gate, 'moe_w_up': moe_w_up, 'moe_w_down': moe_w_down,
    }
    layers = [_prep_layer(l, prm) for l in range(DEPTH)]
    consts = _mixer_consts()
    bp = x_prompt.shape[0]
    zero_hd = jnp.zeros((DEPTH, bp, N_HEADS, HEAD_DIM, HEAD_DIM), F32)
    zero_shift = jnp.zeros((DEPTH, bp, D_MODEL), F32)
    zero_gla = jnp.zeros((DEPTH, bp, N_HEADS, GLA_DK, HEAD_DIM), F32)
    prompt = _run_trunk(x_prompt, 0.0, zero_hd, zero_shift, zero_hd, zero_hd, zero_gla, prm, layers, consts)
    sample = _run_trunk(x_sample, float(PAST_LEN), state_rwkv, cache_shift, state_ret, state_hgrn,
                        state_gla, prm, layers, consts)
    y_p, p_rwkv, p_shift, p_ret, p_hgrn, p_gla = prompt
    y_s, s_rwkv, s_shift, s_ret, s_hgrn, s_gla = sample
    return (y_p, y_s, p_rwkv, p_shift, p_ret, p_hgrn, p_gla, s_rwkv, s_shift, s_ret, s_hgrn, s_gla)
```

```python
import functools
import math

import numpy as np
import jax
import jax.numpy as jnp
import jax.scipy.linalg
from jax import lax
from jax.experimental import pallas as pl
from jax.experimental.pallas import tpu as pltpu

F32 = jnp.float32
BF16 = jnp.bfloat16

D_MODEL = 1024
DEPTH = 2
PAST_LEN = 4096
CHUNK = 64
SUB = 16
N_BRANCH = 4
BR_WIDTH = D_MODEL // N_BRANCH
HEAD_DIM = 64
N_HEADS = BR_WIDTH // HEAD_DIM
GLA_DK = HEAD_DIM // 2
GLA_KW = N_HEADS * GLA_DK
GLA_GATE_RANK = 16
GLA_TAU = 16.0
RWKV_W_LORA = 32
RWKV_A_LORA = 32
RWKV_V_LORA = 16
RWKV_G_LORA = 64
RWKV_GN_EPS = 64e-5
ROPE_THETA = 10000.0
LN_EPS = 1e-5
D_FF = 2816
N_EXPERTS = 8
ALPHA = (2.0 * DEPTH) ** 0.25
RWKV_COLS = 3 * BR_WIDTH
IN_COLS = 3584
LORA_COLS = 256
AUX_COLS = 4 * BR_WIDTH + GLA_KW
SEQ_GROUP = 8
FACTOR_LIMIT = 80.0
MOE_LAST_SIZES = 5

MOE_SUB = 256
MOE_WIN = 112
MOE_TAIL = 128

LANE = 128
SUBLANE = 8
BF16_ROWS = 16
LOG2_E = 1.4426950408889634

NN = (((1,), (0,)), ((), ()))
NT = (((1,), (1,)), ((), ()))
TN = (((0,), (0,)), ((), ()))


def _params(sem, vmem_mib):
    return pltpu.CompilerParams(dimension_semantics=sem, vmem_limit_bytes=vmem_mib * 1024 * 1024)


def _const_spec(shape):
    nd = len(shape)
    return pl.BlockSpec(shape, lambda *_: (0,) * nd, pipeline_mode=pl.Buffered(1))


def _dot(a, b, dims=NN):
    return lax.dot_general(a.astype(BF16), b.astype(BF16), dims, preferred_element_type=F32)


def _split(x):
    hi = x.astype(BF16)
    lo = (x - hi.astype(F32)).astype(BF16)
    return hi, lo


def _dot_exact_lhs(a_bf, x):
    hi, lo = _split(x)
    return (jnp.dot(a_bf, hi, preferred_element_type=F32)
            + jnp.dot(a_bf, lo, preferred_element_type=F32))


def _dot_exact_rhs(x, b_bf):
    hi, lo = _split(x)
    return (jnp.dot(hi, b_bf, preferred_element_type=F32)
            + jnp.dot(lo, b_bf, preferred_element_type=F32))


def _dot3(a, b, dims=NN):
    ah, al = _split(a)
    bh, bl = _split(b)
    d = functools.partial(lax.dot_general, dimension_numbers=dims, preferred_element_type=F32)
    return d(ah, bh) + (d(ah, bl) + d(al, bh))


def _select_dot(x, sel_bf, dims, x_first):
    hi = x.astype(BF16)
    rest = x - hi.astype(F32)
    mid = rest.astype(BF16)
    lo = (rest - mid.astype(F32)).astype(BF16)
    d = functools.partial(lax.dot_general, dimension_numbers=dims, preferred_element_type=F32)
    out = None
    for piece in (hi, mid, lo):
        term = d(piece, sel_bf) if x_first else d(sel_bf, piece)
        out = term if out is None else out + term
    return out


def _state_from_value_rows(nat, unfold_bf, mask):
    return _select_dot(nat, unfold_bf, NN, True) * mask


def _state_to_value_rows(st, fold_bf):
    return _select_dot(st, fold_bf, NN, True)


def _state_from_key_rows(nat, fold_bf, mask):
    return _select_dot(nat, fold_bf, NT, False) * mask


def _state_to_key_rows(st, fold_bf):
    return _select_dot(st, fold_bf, TN, True)


def _sigmoid(x):
    return 1.0 / (1.0 + jnp.exp(-x))


def _softplus(x):
    return jnp.maximum(x, 0.0) + jnp.log(1.0 + jnp.exp(-jnp.abs(x)))


def _silu(x):
    return x * _sigmoid(x)


def _layer_norm_rows(y, g, b):
    mu = jnp.mean(y, axis=-1, keepdims=True)
    yc = y - mu
    var = jnp.mean(yc * yc, axis=-1, keepdims=True)
    return yc * lax.rsqrt(var + LN_EPS) * g + b


def _np_block_mask(rows_per_head, cols_per_head):
    r = np.arange(N_HEADS * rows_per_head)[:, None] // rows_per_head
    c = np.arange(N_HEADS * cols_per_head)[None, :] // cols_per_head
    return (r == c).astype(np.float32)


def _np_head_lanes(cols_per_head):
    m = np.zeros((SUBLANE, N_HEADS * cols_per_head), np.float32)
    for h in range(N_HEADS):
        m[h, h * cols_per_head:(h + 1) * cols_per_head] = 1.0
    return m


def _np_causal_side_by_side(strict):
    t = np.arange(CHUNK)[:, None]
    s = np.arange(N_HEADS * CHUNK)[None, :] % CHUNK
    return ((s < t) if strict else (s <= t)).astype(np.float32)


def _in_proj_kernel(x_ref, prev_ref, win_ref, wl1_ref, wl1mu_ref, w2_ref, bias_ref, p_ref, aux_ref):
    x = x_ref[...]
    row = lax.broadcasted_iota(jnp.int32, x.shape, 0)
    xx = jnp.where(row == 0, prev_ref[0], pltpu.roll(x, 1, axis=0)) - x
    xb = x.astype(BF16)
    h = (jnp.dot(xb, wl1_ref[...], preferred_element_type=F32)
         + jnp.dot(xx.astype(BF16), wl1mu_ref[...], preferred_element_type=F32))
    lane = lax.broadcasted_iota(jnp.int32, h.shape, 1)
    act = jnp.where(lane < RWKV_W_LORA, jnp.tanh(h), h)
    g_lo = RWKV_W_LORA + RWKV_A_LORA
    in_g = jnp.where(lane >= g_lo, jnp.where(lane < g_lo + RWKV_G_LORA, 1.0, 0.0), 0.0)
    act = jnp.where(in_g > 0.5, _sigmoid(h), act)
    z = jnp.dot(act.astype(BF16), w2_ref[...], preferred_element_type=F32) + bias_ref[...]
    p_ref[...] = jnp.dot(xb, win_ref[...], preferred_element_type=F32)
    bw = BR_WIDTH
    w_log = -_softplus(-z[:, 0:bw]) - 0.5
    aux_ref[:, 0:bw] = -jnp.exp(w_log)
    aux_ref[:, bw:2 * bw] = _sigmoid(z[:, bw:2 * bw])
    aux_ref[:, 2 * bw:3 * bw] = z[:, 2 * bw:3 * bw]
    aux_ref[:, 3 * bw:4 * bw] = _sigmoid(z[:, 3 * bw:4 * bw])
    zg = z[:, 4 * bw:]
    aux_ref[:, 4 * bw:] = (jnp.minimum(zg, 0.0) - jnp.log(1.0 + jnp.exp(-jnp.abs(zg)))) * (1.0 / GLA_TAU)


def _in_proj(x2, prev_rows, win, wl1, wl1mu, w2, bias, tm):
    n = x2.shape[0]
    return pl.pallas_call(
        _in_proj_kernel,
        grid=(n // tm,),
        in_specs=[
            pl.BlockSpec((tm, D_MODEL), lambda i: (i, 0)),
            pl.BlockSpec((1, 1, D_MODEL), lambda i: (i, 0, 0)),
            _const_spec(win.shape), _const_spec(wl1.shape), _const_spec(wl1mu.shape),
            _const_spec(w2.shape), _const_spec(bias.shape),
        ],
        out_specs=[pl.BlockSpec((tm, IN_COLS), lambda i: (i, 0)),
                   pl.BlockSpec((tm, AUX_COLS), lambda i: (i, 0))],
        out_shape=[jax.ShapeDtypeStruct((n, IN_COLS), F32), jax.ShapeDtypeStruct((n, AUX_COLS), F32)],
        compiler_params=_params(("arbitrary",), 56),
        name="in_proj",
    )(x2, prev_rows, win, wl1, wl1mu, w2, bias)


def _rows_matmul_kernel(x_ref, w_ref, o_ref):
    o_ref[...] = jnp.dot(x_ref[...].astype(BF16), w_ref[...], preferred_element_type=F32)


def _rows_matmul(x, w):
    return pl.pallas_call(
        _rows_matmul_kernel,
        out_shape=jax.ShapeDtypeStruct((x.shape[0], w.shape[1]), F32),
        name="rows_matmul",
    )(x, w)


def _stack_heads(x, head_lanes):
    xb = x.astype(BF16)
    return jnp.concatenate([xb * head_lanes[h:h + 1, :].astype(BF16) for h in range(N_HEADS)], axis=0)


def _head_sum(x, ones_bf):
    return _dot_exact_rhs(x, ones_bf)


def _head_layer_norm(y, ones_bf, g, b, eps):
    inv = 1.0 / HEAD_DIM
    mu = _head_sum(y, ones_bf) * inv
    yc = y - mu
    var = _head_sum(yc * yc, ones_bf) * inv
    return yc * lax.rsqrt(var + eps) * g + b


def _head_rms_norm(y, ones_bf, g):
    ms = _head_sum(y * y, ones_bf) * (1.0 / HEAD_DIM)
    return y * lax.rsqrt(ms + 1e-6) * g


def _pairwise_block(q, k, v, b2, pair_ones_bf):
    parts = []
    for j in range(SUB):
        lo = (j // SUBLANE) * SUBLANE
        p = q[lo:] * jnp.exp2(b2[lo:] - b2[j:j + 1]) * k[j:j + 1]
        if j % SUBLANE:
            rid = lax.broadcasted_iota(jnp.int32, p.shape, 0) + lo
            p = jnp.where(rid >= j, p, 0.0)
        parts.append(p)
    att = jnp.dot(jnp.concatenate(parts, axis=0).astype(BF16), pair_ones_bf, preferred_element_type=F32)
    outs = []
    off = 0
    for g in range(SUB // SUBLANE):
        rows = SUB - g * SUBLANE
        acc = None
        for j in range(g * SUBLANE, (g + 1) * SUBLANE):
            term = att[off:off + rows] * v[j:j + 1]
            acc = term if acc is None else acc + term
            off += rows
        if g:
            acc = jnp.concatenate([jnp.zeros((g * SUBLANE, v.shape[1]), F32), acc], axis=0)
        outs.append(acc)
    total = outs[0]
    for extra in outs[1:]:
        total = total + extra
    return total


def _gla_intra_pairwise(tiles, b, pair_ones_bf, lanes_k, lanes_v):
    n = range(len(tiles))
    q, k, v, glog = ([t[i] for t in tiles] for i in range(4))
    b2 = [b[i] * LOG2_E for i in n]
    blocks = [[] for _ in n]
    for blk in range(CHUNK // SUB):
        r0 = blk * SUB
        sl = slice(r0, r0 + SUB)
        o_blk = [_pairwise_block(q[i][sl], k[i][sl], v[i][sl], b2[i][sl], pair_ones_bf) for i in n]
        if blk:
            c0 = [b[i][r0 - 1:r0] for i in n]
            q_t = [q[i][sl] * jnp.exp(b[i][sl] - c0[i]) for i in n]
            k_t = [k[i][:r0] * jnp.exp(c0[i] - b[i][:r0]) for i in n]
            att = [_dot(q_t[i], _stack_heads(k_t[i], lanes_k), NT) for i in n]
            o_blk = [o_blk[i] + _dot(att[i], _stack_heads(v[i][:r0], lanes_v)) for i in n]
        for i in n:
            blocks[i].append(o_blk[i])
    return [jnp.concatenate(blocks[i], axis=0) for i in n]


def _gla_intra_factored(tiles, b, lanes_k, lanes_v):
    n = range(len(tiles))
    q, k, v, _ = ([t[i] for t in tiles] for i in range(4))
    blocks = [[] for _ in n]
    for blk in range(CHUNK // SUB):
        r0 = blk * SUB
        upto = r0 + SUB
        sl = slice(r0, upto)
        if blk:
            c0 = [b[i][r0 - 1:r0] for i in n]
            q_t = [q[i][sl] * jnp.exp(b[i][sl] - c0[i]) for i in n]
            k_t = [k[i][:upto] * jnp.exp(c0[i] - b[i][:upto]) for i in n]
        else:
            q_t = [q[i][sl] * jnp.exp(b[i][sl]) for i in n]
            k_t = [k[i][:upto] * jnp.exp(-b[i][:upto]) for i in n]
        att = [_dot(q_t[i], _stack_heads(k_t[i], lanes_k), NT) for i in n]
        t_id = lax.broadcasted_iota(jnp.int32, (SUB, N_HEADS * upto), 0) + r0
        s_id = lax.broadcasted_iota(jnp.int32, (SUB, N_HEADS * upto), 1) % upto
        causal = s_id <= t_id
        o_blk = [_dot(jnp.where(causal, att[i], 0.0), _stack_heads(v[i][:upto], lanes_v)) for i in n]
        for i in n:
            blocks[i].append(o_blk[i])
    return [jnp.concatenate(blocks[i], axis=0) for i in n]


def _gla_block_range(b):
    worst = -b[SUB - 1:SUB, :]
    for blk in range(1, CHUNK // SUB):
        r0 = blk * SUB
        worst = jnp.maximum(worst, b[r0 - 1:r0, :] - b[r0 + SUB - 1:r0 + SUB, :])
    return worst


def _gla_state_parts(tiles, b, st_mask):
    n = range(len(tiles))
    q, k, v, _ = ([t[i] for t in tiles] for i in range(4))
    b_last = [b[i][CHUNK - 1:CHUNK, :] for i in n]
    upd = [st_mask * _dot(v[i], k[i] * jnp.exp(b_last[i] - b[i]), TN) for i in n]
    return [(q[i] * jnp.exp(b[i]), upd[i], jnp.exp(b_last[i])) for i in n]


def _rwkv_state_free(tiles, tril_bf, hl, strict, incl, eye):
    n = range(len(tiles))
    r, k, v, kk, bv, lw = ([t[i] for t in tiles] for i in range(6))
    l = [_dot_exact_lhs(tril_bf, lw[i]) for i in n]
    l_last = [l[i][CHUNK - 1:CHUNK, :] for i in n]
    e_neg = [jnp.exp(-l[i]) for i in n]
    lhs = [jnp.concatenate([kk[i] * jnp.exp(l[i] - lw[i]), r[i] * jnp.exp(l[i])], axis=0) for i in n]
    rhs = [jnp.concatenate([_stack_heads(k[i] * e_neg[i], hl), _stack_heads(bv[i] * e_neg[i], hl)], axis=0)
           for i in n]
    amat = [_dot(lhs[i], rhs[i], NT) for i in n]
    w = N_HEADS * CHUNK
    a_ab = [amat[i][:CHUNK, w:] * strict for i in n]
    a_vk = [jnp.concatenate([amat[i][:CHUNK, :w] * strict, amat[i][CHUNK:, :w] * incl], axis=0) for i in n]
    a_rb = [amat[i][CHUNK:, w:] * incl for i in n]
    x = [eye + a_ab[i] for i in n]
    m_st = [_stack_heads(a_ab[i], hl) for i in n]
    m = [_dot(a_ab[i], m_st[i]) for i in n]
    levels = int(math.log2(CHUNK))
    for level in range(1, levels):
        m_st = [_stack_heads(m[i], hl) for i in n]
        if level < levels - 1:
            both = [_dot(jnp.concatenate([m[i], x[i]], axis=0), m_st[i]) for i in n]
            m = [both[i][:CHUNK] for i in n]
            x = [x[i] + both[i][CHUNK:] for i in n]
        else:
            x = [x[i] + _dot(x[i], m_st[i]) for i in n]
    from_v = [_dot(a_vk[i], _stack_heads(v[i], hl)) for i in n]
    e_end = [jnp.exp(l_last[i] - l[i]) for i in n]
    upd_v = [_dot(v[i], k[i] * e_end[i], TN) for i in n]
    b_end = [bv[i] * e_end[i] for i in n]
    st_dec = [jnp.exp(l_last[i]) for i in n]
    return [(lhs[i], x[i], a_rb[i], from_v[i], upd_v[i], b_end[i], st_dec[i]) for i in n]


def _rwkv_state_step(parts, sts, bdm, hl):
    n = range(len(parts))
    lhs, x, a_rb, from_v, upd_v, b_end, st_dec = ([p[i] for p in parts] for i in range(7))
    from_state = [_dot(lhs[i], sts[i], NT) for i in n]
    u = [_dot(x[i], _stack_heads(from_state[i][:CHUNK] + from_v[i][:CHUNK], hl)) for i in n]
    upd = [upd_v[i] + _dot(u[i], b_end[i], TN) for i in n]
    new = [sts[i] * st_dec[i] + upd[i] * bdm for i in n]
    y = [from_state[i][CHUNK:] + from_v[i][CHUNK:] + _dot(a_rb[i], _stack_heads(u[i], hl)) for i in n]
    return y, new


def _rwkv_kernel(has_vres, *refs):
    if has_vres:
        (rkv_ref, lw_ref, a_ref, g_ref, vg_ref, vf_ref, last_ref, st0_ref, prm_ref, tril_ref, bdm_ref,
         hl_ref, strict_ref, incl_ref, eye_ref, fold_ref, unfold_ref, o_ref, st_out_ref, st_sc,
         prev_sc) = refs
    else:
        (rkv_ref, lw_ref, a_ref, g_ref, last_ref, st0_ref, prm_ref, tril_ref, bdm_ref,
         hl_ref, strict_ref, incl_ref, eye_ref, fold_ref, unfold_ref, o_ref, v_out_ref, st_out_ref, st_sc,
         prev_sc) = refs
    tb = pl.program_id(1)

    @pl.when(tb == 0)
    def _():
        for s in range(st0_ref.shape[0]):
            st_sc[s] = _state_from_value_rows(st0_ref[s], unfold_ref[...], bdm_ref[...])
        prev_sc[...] = last_ref[...]

    bw = BR_WIDTH
    k_k = prm_ref[1:2, 0:bw]
    k_a = prm_ref[2:3, 0:bw]
    r_k = prm_ref[3:4, 0:bw]
    ln_g = prm_ref[4:5, 0:bw]
    ln_b = prm_ref[5:6, 0:bw]
    bdm = bdm_ref[...]
    ones_bf = bdm.astype(BF16)
    hl = hl_ref[...]
    tril_bf = tril_ref[...]
    strict = strict_ref[...]
    incl = incl_ref[...]
    eye = eye_ref[...]
    n_seq = rkv_ref.shape[0]
    tt = rkv_ref.shape[1]
    seqs = []
    for s in range(n_seq):
        rkv = rkv_ref[s]
        row = lax.broadcasted_iota(jnp.int32, rkv.shape, 0)
        prev = jnp.where(row == 0, prev_sc[s], pltpu.roll(rkv, 1, axis=0))
        prev_sc[s] = rkv[tt - 1:tt, :]
        mixed = rkv + (prev - rkv) * prm_ref[0:1, :]
        r = mixed[:, 0:bw]
        k = mixed[:, bw:2 * bw]
        v = mixed[:, 2 * bw:]
        a = a_ref[s]
        if has_vres:
            v = v + (vf_ref[s] - v) * vg_ref[s]
        else:
            v_out_ref[s] = v
        kk = k * k_k
        kk = kk * lax.rsqrt(jnp.maximum(_head_sum(kk * kk, ones_bf), 1e-24))
        k = k * (1.0 + (a - 1.0) * k_a)
        seqs.append((r, k, v, kk, -(kk * a), lw_ref[s]))
    n_chunks = tt // CHUNK
    tiles = [tuple(z[c * CHUNK:(c + 1) * CHUNK] for z in seqs[s])
             for c in range(n_chunks) for s in range(n_seq)]
    parts = _rwkv_state_free(tiles, tril_bf, hl, strict, incl, eye)
    sts = [st_sc[s] for s in range(n_seq)]
    ys = [[] for _ in range(n_seq)]
    for c in range(n_chunks):
        y_c, sts = _rwkv_state_step(parts[c * n_seq:(c + 1) * n_seq], sts, bdm, hl)
        for s in range(n_seq):
            ys[s].append(y_c[s])
    for s in range(n_seq):
        r, k, v = seqs[s][0:3]
        st_sc[s] = sts[s]
        y = ys[s][0] if len(ys[s]) == 1 else jnp.concatenate(ys[s], axis=0)
        y = _head_layer_norm(y, ones_bf, ln_g, ln_b, RWKV_GN_EPS)
        bonus = _head_sum(r * k * r_k, ones_bf) * v
        o_ref[s] = (y + bonus) * g_ref[s]

    @pl.when(tb == pl.num_programs(1) - 1)
    def _():
        for s in range(n_seq):
            st_out_ref[s] = _state_to_value_rows(sts[s], fold_ref[...])


def _rwkv_mixer(p3, aux3, v_first, rkv_last, st0, prm, consts, tt):
    b, t, _ = p3.shape
    bw = BR_WIDTH
    pb = min(b, SEQ_GROUP)
    has_vres = v_first is not None
    tok = lambda j: pl.BlockSpec((pb, tt, bw), lambda bi, ti, j=j: (bi, ti, j))
    in_specs = [pl.BlockSpec((pb, tt, RWKV_COLS), lambda bi, ti: (bi, ti, 0)),
                tok(0), tok(1), tok(2)]
    args = [p3, aux3, aux3, aux3]
    if has_vres:
        in_specs += [tok(3), pl.BlockSpec((pb, tt, bw), lambda bi, ti: (bi, ti, 0))]
        args += [aux3, v_first]
    st_spec = pl.BlockSpec((pb, bw, HEAD_DIM), lambda bi, ti: (bi, 0, 0))
    in_specs += [pl.BlockSpec((pb, 1, RWKV_COLS), lambda bi, ti: (bi, 0, 0)), st_spec,
                 _const_spec(prm.shape)]
    args += [rkv_last, st0, prm]
    for name in ("tril", "bdm", "head_lanes", "strict", "incl", "eye", "fold", "unfold"):
        in_specs.append(_const_spec(consts[name].shape))
        args.append(consts[name])
    seq = pl.BlockSpec((pb, tt, bw), lambda bi, ti: (bi, ti, 0))
    seq_shape = jax.ShapeDtypeStruct((b, t, bw), F32)
    st_shape = jax.ShapeDtypeStruct((b, bw, HEAD_DIM), F32)
    if has_vres:
        out_specs, out_shape = [seq, st_spec], [seq_shape, st_shape]
    else:
        out_specs, out_shape = [seq, seq, st_spec], [seq_shape, seq_shape, st_shape]
    return pl.pallas_call(
        functools.partial(_rwkv_kernel, has_vres),
        grid=(b // pb, t // tt),
        in_specs=in_specs, out_specs=out_specs, out_shape=out_shape,
        scratch_shapes=[pltpu.VMEM((pb, bw, bw), F32), pltpu.VMEM((pb, 1, RWKV_COLS), F32)],
        compiler_params=_params(("arbitrary", "arbitrary"), 48),
        name="rwkv_mixer",
    )(*args)


def _rot_half(z):
    w = z.shape[1]
    half = HEAD_DIM // 2
    lane = lax.broadcasted_iota(jnp.int32, z.shape, 1)
    first = (lane % HEAD_DIM) < half
    return jnp.where(first, pltpu.roll(z, w - half, axis=1), pltpu.roll(z, half, axis=1))


def _ret_kernel(q_ref, k_ref, v_ref, g_ref, cos_ref, sin_ref, st0_ref, prm_ref, dec_ref, bdm_ref, hl_ref,
                fold_ref, o_ref, st_out_ref, st_sc):
    tb = pl.program_id(1)

    @pl.when(tb == 0)
    def _():
        for s in range(st0_ref.shape[0]):
            st_sc[s] = _state_from_key_rows(st0_ref[s], fold_ref[...], bdm_ref[...])

    cos = cos_ref[...]
    sin = sin_ref[...]
    bdm = bdm_ref[...]
    hl = hl_ref[...]
    ones_bf = bdm.astype(BF16)
    q_dec = dec_ref[0:CHUNK, :]
    k_dec = dec_ref[CHUNK:2 * CHUNK, :]
    d_mat = dec_ref[2 * CHUNK:3 * CHUNK, :]
    s_dec = dec_ref[3 * CHUNK:3 * CHUNK + 1, :]
    n_seq = q_ref.shape[0]
    tt = q_ref.shape[1]
    qs, ks, vs = [], [], []
    for s in range(n_seq):
        q = q_ref[s]
        k = k_ref[s]
        qs.append(q * cos + _rot_half(q) * sin)
        ks.append((k * cos + _rot_half(k) * sin) * (HEAD_DIM ** -0.5))
        vs.append(v_ref[s])
    sts = [st_sc[s] for s in range(n_seq)]
    outs = [[] for _ in range(n_seq)]
    for c in range(tt // CHUNK):
        sl = slice(c * CHUNK, (c + 1) * CHUNK)
        for s in range(n_seq):
            qc, kc, vc = qs[s][sl], ks[s][sl], vs[s][sl]
            att = _dot(qc, _stack_heads(kc, hl), NT) * d_mat
            outs[s].append(_dot(qc * q_dec, sts[s], NT) + _dot(att, _stack_heads(vc, hl)))
            sts[s] = sts[s] * s_dec + bdm * _dot(vc, kc * k_dec, TN)
    for s in range(n_seq):
        st_sc[s] = sts[s]
        o = outs[s][0] if len(outs[s]) == 1 else jnp.concatenate(outs[s], axis=0)
        o = _head_layer_norm(o, ones_bf, prm_ref[0:1, :], prm_ref[1:2, :], LN_EPS)
        o_ref[s] = o * _silu(g_ref[s])

    @pl.when(tb == pl.num_programs(1) - 1)
    def _():
        for s in range(n_seq):
            st_out_ref[s] = _state_to_key_rows(sts[s], fold_ref[...])


def _ret_mixer(p3, cos_t, sin_t, st0, prm, consts, tt):
    b, t, _ = p3.shape
    bw = BR_WIDTH
    pb = min(b, SEQ_GROUP)
    base = RWKV_COLS // bw
    tok = lambda j: pl.BlockSpec((pb, tt, bw), lambda bi, ti, j=j: (bi, ti, base + j))
    tab = pl.BlockSpec((tt, bw), lambda bi, ti: (ti, 0))
    st_spec = pl.BlockSpec((pb, bw, HEAD_DIM), lambda bi, ti: (bi, 0, 0))
    dec, bdm, hl, fold = consts["ret_dec"], consts["bdm"], consts["head_lanes"], consts["fold"]
    return pl.pallas_call(
        _ret_kernel,
        grid=(b // pb, t // tt),
        in_specs=[tok(0), tok(1), tok(2), tok(3), tab, tab, st_spec, _const_spec(prm.shape),
                  _const_spec(dec.shape), _const_spec(bdm.shape), _const_spec(hl.shape),
                  _const_spec(fold.shape)],
        out_specs=[pl.BlockSpec((pb, tt, bw), lambda bi, ti: (bi, ti, 0)), st_spec],
        out_shape=[jax.ShapeDtypeStruct((b, t, bw), F32), jax.ShapeDtypeStruct((b, bw, HEAD_DIM), F32)],
        scratch_shapes=[pltpu.VMEM((pb, bw, bw), F32)],
        compiler_params=_params(("arbitrary", "arbitrary"), 40),
        name="ret_mixer",
    )(p3, p3, p3, p3, cos_t, sin_t, st0, prm, dec, bdm, hl, fold)


def _gated_mixer_tail(seqs, norm_g, st_sc, intra_sc, st_out_ref, o_ref, tril_bf, pair_ones_bf, st_mask,
                      lanes_k, lanes_v, ones_v_bf, fold_bf):
    n_seq = len(seqs)
    tt = seqs[0][0].shape[0]
    n_chunks = tt // CHUNK
    tiles = [tuple(z[c * CHUNK:(c + 1) * CHUNK] for z in seqs[s][0:4])
             for c in range(n_chunks) for s in range(n_seq)]
    b = [_dot_exact_lhs(tril_bf, t[3]) for t in tiles]
    worst = _gla_block_range(b[0])
    for b_i in b[1:]:
        worst = jnp.maximum(worst, _gla_block_range(b_i))
    risky = jnp.max(worst) > FACTOR_LIMIT

    def put_intra(intra):
        for idx, o_i in enumerate(intra):
            c_i, s_i = divmod(idx, n_seq)
            intra_sc[s_i, c_i * CHUNK:(c_i + 1) * CHUNK, :] = o_i

    @pl.when(risky)
    def _():
        put_intra(_gla_intra_pairwise(tiles, b, pair_ones_bf, lanes_k, lanes_v))

    @pl.when(jnp.logical_not(risky))
    def _():
        put_intra(_gla_intra_factored(tiles, b, lanes_k, lanes_v))

    parts = _gla_state_parts(tiles, b, st_mask)
    sts = [st_sc[s] for s in range(n_seq)]
    outs = [[] for _ in range(n_seq)]
    for c in range(n_chunks):
        for s in range(n_seq):
            q_dec, upd, dec = parts[c * n_seq + s]
            outs[s].append(intra_sc[s, c * CHUNK:(c + 1) * CHUNK, :] + _dot(q_dec, sts[s], NT))
            sts[s] = sts[s] * dec + upd
    for s in range(n_seq):
        st_sc[s] = sts[s]
        o = outs[s][0] if len(outs[s]) == 1 else jnp.concatenate(outs[s], axis=0)
        o_ref[s] = _head_rms_norm(o, ones_v_bf, norm_g) * _silu(seqs[s][4])

    @pl.when(pl.program_id(1) == pl.num_programs(1) - 1)
    def _():
        for s in range(n_seq):
            st_out_ref[s] = _state_to_key_rows(sts[s], fold_bf)


def _hgrn_kernel(layer, q_ref, f_ref, i_ref, g_ref, st0_ref, lbl_ref, ng_ref, tril_ref, bdm_ref, hl_ref,
                 fold_ref, o_ref, st_out_ref, st_sc, intra_sc):
    @pl.when(pl.program_id(1) == 0)
    def _():
        for s in range(st0_ref.shape[0]):
            st_sc[s] = _state_from_key_rows(st0_ref[s], fold_ref[...], bdm_ref[...])

    logits = lbl_ref[...]
    ex = jnp.exp(logits - jnp.max(logits, axis=0, keepdims=True))
    sm = ex / jnp.sum(ex, axis=0, keepdims=True)
    lb = jnp.zeros((1, BR_WIDTH), F32)
    for d in range(1, layer + 1):
        lb = lb + sm[d:d + 1, :]
    seqs = []
    for s in range(q_ref.shape[0]):
        fz = f_ref[s]
        f = lb + (1.0 - lb) * _sigmoid(fz)
        k = (1.0 - lb) * _sigmoid(-fz)
        seqs.append((_silu(q_ref[s]), k, i_ref[s], jnp.log(f), g_ref[s]))
    bdm = bdm_ref[...]
    ones_bf = bdm.astype(BF16)
    hl = hl_ref[...]
    _gated_mixer_tail(seqs, ng_ref[...], st_sc, intra_sc, st_out_ref, o_ref, tril_ref[...], ones_bf, bdm,
                      hl, hl, ones_bf, fold_ref[...])


def _hgrn_mixer(p3, st0, lb_logits, norm_g, consts, layer, tt):
    b, t, _ = p3.shape
    bw = BR_WIDTH
    pb = min(b, SEQ_GROUP)
    base = (RWKV_COLS + 4 * bw) // bw
    tok = lambda j: pl.BlockSpec((pb, tt, bw), lambda bi, ti, j=j: (bi, ti, base + j))
    st_spec = pl.BlockSpec((pb, bw, HEAD_DIM), lambda bi, ti: (bi, 0, 0))
    tril, bdm, hl, fold = consts["tril"], consts["bdm"], consts["head_lanes"], consts["fold"]
    return pl.pallas_call(
        functools.partial(_hgrn_kernel, layer),
        grid=(b // pb, t // tt),
        in_specs=[tok(0), tok(1), tok(2), tok(3), st_spec, _const_spec(lb_logits.shape),
                  _const_spec(norm_g.shape), _const_spec(tril.shape), _const_spec(bdm.shape),
                  _const_spec(hl.shape), _const_spec(fold.shape)],
        out_specs=[pl.BlockSpec((pb, tt, bw), lambda bi, ti: (bi, ti, 0)), st_spec],
        out_shape=[jax.ShapeDtypeStruct((b, t, bw), F32), jax.ShapeDtypeStruct((b, bw, HEAD_DIM), F32)],
        scratch_shapes=[pltpu.VMEM((pb, bw, bw), F32), pltpu.VMEM((pb, tt, bw), F32)],
        compiler_params=_params(("arbitrary", "arbitrary"), 48),
        name="hgrn_mixer",
    )(p3, p3, p3, p3, st0, lb_logits, norm_g, tril, bdm, hl, fold)


def _gla_kernel(q_ref, k_ref, v_ref, g_ref, la_ref, st0_ref, ng_ref, tril_ref, pair_ref, mask_ref,
                bdm_ref, hlk_ref, hlv_ref, fold_ref, o_ref, st_out_ref, st_sc, intra_sc):
    @pl.when(pl.program_id(1) == 0)
    def _():
        for s in range(st0_ref.shape[0]):
            st_sc[s] = _state_from_key_rows(st0_ref[s], fold_ref[...], mask_ref[...])

    seqs = [(q_ref[s] * (GLA_DK ** -0.5), k_ref[s], v_ref[s], la_ref[s], g_ref[s])
            for s in range(q_ref.shape[0])]
    _gated_mixer_tail(seqs, ng_ref[...], st_sc, intra_sc, st_out_ref, o_ref, tril_ref[...], pair_ref[...],
                      mask_ref[...], hlk_ref[...], hlv_ref[...], bdm_ref[...].astype(BF16), fold_ref[...])


def _gla_mixer(p3, aux3, st0, norm_g, consts, tt):
    b, t, _ = p3.shape
    bw, kw = BR_WIDTH, GLA_KW
    pb = min(b, SEQ_GROUP)
    gla0 = RWKV_COLS + 8 * bw
    st_spec = pl.BlockSpec((pb, kw, HEAD_DIM), lambda bi, ti: (bi, 0, 0))
    names = ("tril", "gla_pair", "gla_mask", "bdm", "gla_head_lanes", "head_lanes", "fold")
    return pl.pallas_call(
        _gla_kernel,
        grid=(b // pb, t // tt),
        in_specs=[pl.BlockSpec((pb, tt, kw), lambda bi, ti: (bi, ti, gla0 // kw)),
                  pl.BlockSpec((pb, tt, kw), lambda bi, ti: (bi, ti, gla0 // kw + 1)),
                  pl.BlockSpec((pb, tt, bw), lambda bi, ti: (bi, ti, (gla0 + 2 * kw) // bw)),
                  pl.BlockSpec((pb, tt, bw), lambda bi, ti: (bi, ti, (gla0 + 2 * kw) // bw + 1)),
                  pl.BlockSpec((pb, tt, kw), lambda bi, ti: (bi, ti, 4 * bw // kw)),
                  st_spec, _const_spec(norm_g.shape)] + [_const_spec(consts[nm].shape) for nm in names],
        out_specs=[pl.BlockSpec((pb, tt, bw), lambda bi, ti: (bi, ti, 0)), st_spec],
        out_shape=[jax.ShapeDtypeStruct((b, t, bw), F32), jax.ShapeDtypeStruct((b, kw, HEAD_DIM), F32)],
        scratch_shapes=[pltpu.VMEM((pb, bw, kw), F32), pltpu.VMEM((pb, tt, bw), F32)],
        compiler_params=_params(("arbitrary", "arbitrary"), 48),
        name="gla_mixer",
    )(p3, p3, p3, p3, aux3, st0, norm_g, *[consts[nm] for nm in names])


def _merge_kernel(emit_bf16, x_ref, o0_ref, o1_ref, o2_ref, o3_ref, wg_ref, bg_ref, wbr_ref, wo_ref,
                  ln_ref, *out_refs):
    branches = (o0_ref, o1_ref, o2_ref, o3_ref)
    half = x_ref.shape[0] // 2
    rows = [slice(0, half), slice(half, 2 * half)]
    x = [x_ref[r, :] for r in rows]
    xb = [v.astype(BF16) for v in x]
    merged = [None, None]
    for m in range(N_BRANCH):
        pre = [jnp.dot(xb[h], wg_ref[m], preferred_element_type=F32) for h in range(2)]
        proj = [jnp.dot(branches[m][rows[h], :].astype(BF16), wbr_ref[m], preferred_element_type=F32)
                for h in range(2)]
        for h in range(2):
            term = _sigmoid(pre[h] + bg_ref[m:m + 1, :]) * proj[h]
            merged[h] = term if merged[h] is None else merged[h] + term
    y = [ALPHA * x[h] + jnp.dot(merged[h].astype(BF16), wo_ref[...], preferred_element_type=F32)
         for h in range(2)]
    for h in range(2):
        yn = _layer_norm_rows(y[h], ln_ref[0:1, :], ln_ref[1:2, :])
        out_refs[0][rows[h], :] = yn
        if emit_bf16:
            out_refs[1][rows[h], :] = yn.astype(BF16)


def _merge(x2, outs, wg, bg, wbr, wo, ln, tm, emit_bf16):
    n = x2.shape[0]
    row = pl.BlockSpec((tm, D_MODEL), lambda i: (i, 0))
    br = pl.BlockSpec((tm, BR_WIDTH), lambda i: (i, 0))
    out_specs = [row]
    out_shape = [jax.ShapeDtypeStruct((n, D_MODEL), F32)]
    if emit_bf16:
        out_specs.append(row)
        out_shape.append(jax.ShapeDtypeStruct((n, D_MODEL), BF16))
    return pl.pallas_call(
        functools.partial(_merge_kernel, emit_bf16),
        grid=(n // tm,),
        in_specs=[row, br, br, br, br, _const_spec(wg.shape), _const_spec(bg.shape),
                  _const_spec(wbr.shape), _const_spec(wo.shape), _const_spec(ln.shape)],
        out_specs=out_specs, out_shape=out_shape,
        compiler_params=_params(("arbitrary",), 60),
        name="merge",
    )(x2, *outs, wg, bg, wbr, wo, ln)


FF_SPLIT = 2
FF_PART = D_FF // FF_SPLIT
FFN_PARTS = (1024, 1024, 768)


def _ffn_kernel(x_ref, wg_ref, wu_ref, wd_ref, ln_ref, o_ref):
    x = x_ref[...]
    xb = x.astype(BF16)
    acc = ALPHA * x
    lo = 0
    for width in FFN_PARTS:
        cs = slice(lo, lo + width)
        lo += width
        h = (_silu(jnp.dot(xb, wg_ref[:, cs], preferred_element_type=F32))
             * jnp.dot(xb, wu_ref[:, cs], preferred_element_type=F32))
        acc = acc + jnp.dot(h.astype(BF16), wd_ref[cs, :], preferred_element_type=F32)
    o_ref[...] = _layer_norm_rows(acc, ln_ref[0:1, :], ln_ref[1:2, :])


def _ffn(x2, wg, wu, wd, ln, tm):
    n = x2.shape[0]
    row = pl.BlockSpec((tm, D_MODEL), lambda i: (i, 0))
    return pl.pallas_call(
        _ffn_kernel,
        grid=(n // tm,),
        in_specs=[row, _const_spec(wg.shape), _const_spec(wu.shape), _const_spec(wd.shape),
                  _const_spec(ln.shape)],
        out_specs=row,
        out_shape=jax.ShapeDtypeStruct((n, D_MODEL), F32),
        compiler_params=_params(("arbitrary",), 60),
        name="ffn",
    )(x2, wg, wu, wd, ln)


def _router_kernel(x_ref, wr_ref, br_ref, tril_ref, rank_ref, wsel_ref, cnt_ref):
    logits = _dot3(x_ref[...], wr_ref[...]) + br_ref[...]
    lane = lax.broadcasted_iota(jnp.int32, logits.shape, 1)
    neg = jnp.float32(-jnp.inf)
    logits = jnp.where(lane < N_EXPERTS, logits, neg)
    m1 = jnp.max(logits, axis=1, keepdims=True)
    lane_f = lane.astype(F32)
    i1 = jnp.min(jnp.where(logits == m1, lane_f, float(LANE)), axis=1, keepdims=True)
    first = lane_f == i1
    rest = jnp.where(first, neg, logits)
    m2 = jnp.max(rest, axis=1, keepdims=True)
    i2 = jnp.min(jnp.where(rest == m2, lane_f, float(LANE)), axis=1, keepdims=True)
    second = lane_f == i2
    e = jnp.exp(m2 - m1)
    w1 = 1.0 / (1.0 + e)
    w2 = e / (1.0 + e)
    sel = jnp.where(first, 1.0, jnp.where(second, 1.0, 0.0))
    wsel_ref[...] = jnp.where(first, w1, jnp.where(second, w2, 0.0))
    sel_bf = sel.astype(BF16)
    rank = jnp.dot(tril_ref[...], sel_bf, preferred_element_type=F32)
    rank_ref[...] = jnp.where(sel > 0.5, rank, -1.0)
    ones = jnp.ones((SUBLANE, MOE_SUB), BF16)
    row = lax.broadcasted_iota(jnp.int32, (SUBLANE, LANE), 0)
    cnt = jnp.zeros((SUBLANE, LANE), F32)
    for s in range(sel.shape[0] // MOE_SUB):
        part = jnp.dot(ones, sel_bf[s * MOE_SUB:(s + 1) * MOE_SUB, :], preferred_element_type=F32)
        cnt = jnp.where(row == s, part, cnt)
    cnt_ref[0] = cnt.astype(jnp.int32)


def _router(x2, wr, br, tril, tm):
    n = x2.shape[0]
    nt = n // tm
    col = pl.BlockSpec((tm, LANE), lambda i: (i, 0))
    return pl.pallas_call(
        _router_kernel,
        grid=(nt,),
        in_specs=[pl.BlockSpec((tm, D_MODEL), lambda i: (i, 0)), _const_spec(wr.shape),
                  _const_spec(br.shape), _const_spec(tril.shape)],
        out_specs=[col, col, pl.BlockSpec((1, SUBLANE, LANE), lambda i: (i, 0, 0))],
        out_shape=[jax.ShapeDtypeStruct((n, LANE), F32), jax.ShapeDtypeStruct((n, LANE), F32),
                   jax.ShapeDtypeStruct((nt, SUBLANE, LANE), jnp.int32)],
        compiler_params=_params(("arbitrary",), 40),
        name="router",
    )(x2, wr, br, tril)


def _moe_kernel(rows, cnt_ref, off_ref, end_ref, x_ref, xb_ref, rrow_ref, rank_ref, wsel_ref, wg_ref, wu_ref,
                wd_ref, ln_ref, o_ref, xg_sc, yb_sc):
    i = pl.program_id(0)
    e = pl.program_id(1)
    c = pl.program_id(2)
    n_e = pl.num_programs(1)
    n_c = pl.num_programs(2)
    cnt = cnt_ref[i * N_EXPERTS + e]
    n_blk = (cnt + rows - 1) // rows
    tm = xb_ref.shape[0]
    n_sub = tm // MOE_SUB

    def windows(s):
        idx = (i * n_sub + s) * N_EXPERTS + e
        off = off_ref[idx]
        end = end_ref[idx]
        a0 = (off // BF16_ROWS) * BF16_ROWS
        n_win = jnp.where(end > off, (end - a0 + MOE_WIN - 1) // MOE_WIN, 0)
        return a0, n_win

    @pl.when((e == 0) & (c == 0))
    def _():
        o_ref[...] = ALPHA * x_ref[...]

    toks = [slice(s * MOE_SUB, (s + 1) * MOE_SUB) for s in range(n_sub)]

    def gather_piece(s, r0):
        slot = (lax.broadcasted_iota(jnp.int32, (MOE_WIN, MOE_SUB), 0) + r0).astype(F32)
        onehot = jnp.where(rrow_ref[0, :, toks[s]] == slot, 1.0, 0.0).astype(BF16)
        return jnp.dot(onehot, xb_ref[toks[s], :], preferred_element_type=F32)

    def gather_add(r0, piece):
        cur = xg_sc[pl.ds(r0, MOE_WIN), :].astype(F32)
        xg_sc[pl.ds(r0, MOE_WIN), :] = (cur + piece).astype(BF16)

    @pl.when(c == 0)
    def _():
        xg_sc[...] = jnp.zeros(xg_sc.shape, xg_sc.dtype)
        wins = [windows(s) for s in range(n_sub)]
        starts = [pl.multiple_of(a0, BF16_ROWS) for a0, _ in wins]
        pieces = [gather_piece(s, starts[s]) for s in range(n_sub)]
        for s in range(n_sub):
            gather_add(starts[s], pieces[s])
        for s in range(n_sub):
            a0, n_win = wins[s]

            def more(wi, carry, s=s, a0=a0):
                r0 = pl.multiple_of(a0 + wi * MOE_WIN, BF16_ROWS)
                gather_add(r0, gather_piece(s, r0))
                return carry

            lax.fori_loop(1, n_win, more, 0)

    def expert_rows(r0, m):
        xg = xg_sc[pl.ds(r0, m), :]
        h = (_silu(jnp.dot(xg, wg_ref[0], preferred_element_type=F32))
             * jnp.dot(xg, wu_ref[0], preferred_element_type=F32))
        yb = jnp.dot(h.astype(BF16), wd_ref[0], preferred_element_type=F32)

        @pl.when(c == 0)
        def _():
            yb_sc[pl.ds(r0, m), :] = yb

        @pl.when(c > 0)
        def _():
            yb_sc[pl.ds(r0, m), :] = yb_sc[pl.ds(r0, m), :] + yb

    def full_block(blk, carry):
        expert_rows(pl.multiple_of(blk * rows, BF16_ROWS), rows)
        return carry

    lax.fori_loop(0, n_blk - 1, full_block, 0)
    last = pl.multiple_of(jnp.maximum(n_blk - 1, 0) * rows, BF16_ROWS)
    rem = cnt - last
    sizes = [rows - k * BF16_ROWS for k in range(MOE_LAST_SIZES - 1, -1, -1)]
    used = jnp.int32(0)
    lower = 0
    for m in sizes:
        fits = (rem > lower) & (rem <= m)
        pl.when(fits)(functools.partial(expert_rows, last, m))
        used = jnp.where(fits, m, used)
        lower = m

    @pl.when(c == 0)
    def _():
        tail = pl.multiple_of(last + used, BF16_ROWS)
        yb_sc[pl.ds(tail, MOE_TAIL), :] = jnp.zeros((MOE_TAIL, D_MODEL), F32)

    @pl.when(c == n_c - 1)
    def _():
        mine = lax.broadcasted_iota(jnp.int32, (tm, LANE), 1) == e
        rank_col = jnp.sum(jnp.where(mine, rank_ref[...], 0.0), axis=1, keepdims=True)
        w_col = jnp.sum(jnp.where(mine, wsel_ref[...], 0.0), axis=1, keepdims=True)

        def scatter_piece(s, r0):
            slot = (lax.broadcasted_iota(jnp.int32, (MOE_SUB, MOE_WIN), 1) + r0).astype(F32)
            onehot = jnp.where(rank_col[toks[s]] == slot, 1.0, 0.0).astype(BF16)
            return jnp.dot(onehot, yb_sc[pl.ds(r0, MOE_WIN), :].astype(BF16), preferred_element_type=F32)

        wins = [windows(s) for s in range(n_sub)]
        starts = [pl.multiple_of(a0, BF16_ROWS) for a0, _ in wins]
        backs = [scatter_piece(s, starts[s]) for s in range(n_sub)]
        for s in range(n_sub):
            o_ref[toks[s], :] = o_ref[toks[s], :] + w_col[toks[s]] * backs[s]
        for s in range(n_sub):
            a0, n_win = wins[s]

            def more(wi, carry, s=s, a0=a0):
                r0 = pl.multiple_of(a0 + wi * MOE_WIN, BF16_ROWS)
                o_ref[toks[s], :] = o_ref[toks[s], :] + w_col[toks[s]] * scatter_piece(s, r0)
                return carry

            lax.fori_loop(1, n_win, more, 0)

    @pl.when((e == n_e - 1) & (c == n_c - 1))
    def _():
        o_ref[...] = _layer_norm_rows(o_ref[...], ln_ref[0:1, :], ln_ref[1:2, :])


def _moe(x2, xb2, counts, offs, ends, rank_row, rank, wsel, wg, wu, wd, ln, tm, rows):
    n = x2.shape[0]
    nt = n // tm
    cap = -(-tm // rows) * rows + MOE_TAIL
    tile = lambda i, e, c, *_: (i, 0)
    grid_spec = pltpu.PrefetchScalarGridSpec(
        num_scalar_prefetch=3,
        grid=(nt, N_EXPERTS, FF_SPLIT),
        in_specs=[
            pl.BlockSpec((tm, D_MODEL), tile),
            pl.BlockSpec((tm, D_MODEL), tile),
            pl.BlockSpec((1, 1, tm), lambda i, e, c, *_: (e, 0, i)),
            pl.BlockSpec((tm, LANE), tile),
            pl.BlockSpec((tm, LANE), tile),
            pl.BlockSpec((1, D_MODEL, FF_PART), lambda i, e, c, *_: (e, 0, c)),
            pl.BlockSpec((1, D_MODEL, FF_PART), lambda i, e, c, *_: (e, 0, c)),
            pl.BlockSpec((1, FF_PART, D_MODEL), lambda i, e, c, *_: (e, c, 0)),
            pl.BlockSpec((2, D_MODEL), lambda i, e, c, *_: (0, 0)),
        ],
        out_specs=pl.BlockSpec((tm, D_MODEL), tile),
        scratch_shapes=[pltpu.VMEM((cap, D_MODEL), BF16), pltpu.VMEM((cap, D_MODEL), F32)],
    )
    return pl.pallas_call(
        functools.partial(_moe_kernel, rows),
        grid_spec=grid_spec,
        out_shape=jax.ShapeDtypeStruct((n, D_MODEL), F32),
        compiler_params=_params(("arbitrary", "arbitrary", "arbitrary"), 56),
        name="moe",
    )(counts, offs, ends, x2, xb2, rank_row, rank, wsel, wg, wu, wd, ln)


def _tile_sizes(b, t):
    n = b * t
    tm = min(512, n)
    tm_proj = min(512, t)
    tt = min(512, t)
    tm_moe = min(1024, n)
    rows = 288 if tm_moe == 1024 else 160
    return tm, tm_proj, tt, tm_moe, rows


def _head_rows(s):
    b, h, r, c = s.shape
    return s.reshape(b, h * r, c)


def _pad_cols(a, width):
    return jnp.pad(a, ((0, 0), (0, width - a.shape[1])))


def _prep_layer(l, p):
    d = D_MODEL
    bw = BR_WIDTH
    w = {}
    w["win"] = p["w_in"][l].astype(BF16)
    mu = p["rwkv_mu_x"][l]
    if l >= 1:
        v1, v2, v0, mu_v = p["rwkv_v1"][l - 1], p["rwkv_v2"][l - 1], p["rwkv_v0"][l - 1], p["rwkv_mu_v"][l - 1]
    else:
        v1, v2 = jnp.zeros((d, RWKV_V_LORA), F32), jnp.zeros((RWKV_V_LORA, bw), F32)
        v0, mu_v = jnp.zeros((bw,), F32), jnp.zeros((d,), F32)
    first = [p["rwkv_w1"][l], p["rwkv_a1"][l], p["rwkv_g1"][l], v1, p["gla_w1"][l]]
    shift_mu = [mu[0], mu[1], mu[2], mu_v, jnp.zeros((d,), F32)]
    w["wl1"] = _pad_cols(jnp.concatenate(first, axis=1), LORA_COLS).astype(BF16)
    w["wl1mu"] = _pad_cols(jnp.concatenate([m[:, None] * a for m, a in zip(shift_mu, first)], axis=1),
                           LORA_COLS).astype(BF16)
    second = jax.scipy.linalg.block_diag(p["rwkv_w2"][l], p["rwkv_a2"][l], p["rwkv_g2"][l], v2, p["gla_w2"][l])
    w["w2"] = jnp.pad(second, ((0, LORA_COLS - second.shape[0]), (0, 0))).astype(BF16)
    w["bias"] = jnp.concatenate([p["rwkv_w0"][l], p["rwkv_a0"][l], jnp.zeros((bw,), F32), v0,
                                 p["gla_b"][l]])[None]
    rows = [p["rwkv_mu_rkv"][l].reshape(RWKV_COLS)]
    rows += [jnp.pad(p[name][l], (0, RWKV_COLS - bw))
             for name in ("rwkv_k_k", "rwkv_k_a", "rwkv_r_k", "rwkv_ln_g", "rwkv_ln_b")]
    rows += [jnp.zeros((RWKV_COLS,), F32)] * (SUBLANE - len(rows))
    w["rwkv_prm"] = jnp.stack(rows)
    w["ret_prm"] = jnp.stack([p["ret_gn_g"][l], p["ret_gn_b"][l]])
    w["hgrn_ng"] = p["hgrn_norm_g"][l][None]
    w["gla_ng"] = p["gla_norm_g"][l][None]
    w["wg"] = p["w_gate"][l].astype(BF16)
    w["bg"] = p["b_gate"][l]
    w["wbr"] = p["w_br"][l].astype(BF16)
    w["wo"] = p["w_o"][l].astype(BF16)
    w["ln1"] = jnp.stack([p["ln1_g"][l], p["ln1_b"][l]])
    w["ln2"] = jnp.stack([p["ln2_g"][l], p["ln2_b"][l]])
    j = l // 2
    if l % 2 == 0:
        w["ffn"] = (p["ffn_w_gate"][j].astype(BF16), p["ffn_w_up"][j].astype(BF16),
                    p["ffn_w_down"][j].astype(BF16))
    else:
        wr = _pad_cols(p["router_w"][j], LANE)
        br = _pad_cols(p["router_b"][j][None], LANE)
        w["moe"] = (wr, br, p["moe_w_gate"][j].astype(BF16), p["moe_w_up"][j].astype(BF16),
                    p["moe_w_down"][j].astype(BF16))
    return w


def _mixer_consts():
    bdm = _np_block_mask(HEAD_DIM, HEAD_DIM)
    lg = np.log1p(-np.exp2(-5.0 - np.arange(N_HEADS, dtype=np.float64)))
    lg_l = np.repeat(lg, HEAD_DIM)[None, :]
    t = np.arange(CHUNK, dtype=np.float64)[:, None]
    s_side = (np.arange(N_HEADS * CHUNK) % CHUNK)[None, :].astype(np.float64)
    lg_side = np.repeat(lg, CHUNK)[None, :]
    d_mat = np.where(s_side <= t, np.exp((t - s_side) * lg_side), 0.0)
    dec = np.zeros((3 * CHUNK + SUBLANE, BR_WIDTH), np.float64)
    dec[0:CHUNK] = np.exp((t + 1.0) * lg_l)
    dec[CHUNK:2 * CHUNK] = np.exp((CHUNK - 1.0 - t) * lg_l)
    dec[2 * CHUNK:3 * CHUNK] = d_mat
    dec[3 * CHUNK] = np.exp(CHUNK * lg_l[0])
    return {
        "tril": jnp.asarray(np.tril(np.ones((CHUNK, CHUNK), np.float32)), BF16),
        "bdm": jnp.asarray(bdm, F32),
        "head_lanes": jnp.asarray(_np_head_lanes(HEAD_DIM), F32),
        "strict": jnp.asarray(_np_causal_side_by_side(True), F32),
        "incl": jnp.asarray(_np_causal_side_by_side(False), F32),
        "eye": jnp.asarray(np.tile(np.eye(CHUNK, dtype=np.float32), (1, N_HEADS)), F32),
        "ret_dec": jnp.asarray(dec, F32),
        "gla_pair": jnp.asarray(_np_block_mask(GLA_DK, HEAD_DIM), BF16),
        "gla_mask": jnp.asarray(_np_block_mask(HEAD_DIM, GLA_DK), F32),
        "gla_head_lanes": jnp.asarray(_np_head_lanes(GLA_DK), F32),
        "fold": jnp.asarray(np.tile(np.eye(HEAD_DIM, dtype=np.float32), (N_HEADS, 1)), BF16),
        "unfold": jnp.asarray(np.tile(np.eye(HEAD_DIM, dtype=np.float32), (1, N_HEADS)), BF16),
    }


def _rope_tables(pos0, t):
    half = HEAD_DIM // 2
    pos = pos0 + jnp.arange(t, dtype=F32)
    inv = ROPE_THETA ** (-jnp.arange(half, dtype=F32) / half)
    ang = pos[:, None] * inv[None]
    cos = jnp.cos(ang)
    sin = jnp.sin(ang)
    cos_t = jnp.tile(jnp.concatenate([cos, cos], axis=1), (1, N_HEADS))
    sin_t = jnp.tile(jnp.concatenate([-sin, sin], axis=1), (1, N_HEADS))
    return cos_t, sin_t


def _previous_rows(x, x_last, tm):
    b, t, d = x.shape
    per_seq = t // tm
    tails = x.reshape(b, per_seq, tm, d)[:, :, tm - 1, :]
    prev = jnp.concatenate([x_last[:, None, :], tails[:, :per_seq - 1, :]], axis=1)
    return prev.reshape(b * per_seq, 1, d)


def _run_trunk(x, pos0, s_rwkv, c_shift, s_ret, s_hgrn, s_gla, prm, layers, consts):
    b, t, d = x.shape
    n = b * t
    tm, tm_proj, tt, tm_moe, rows = _tile_sizes(b, t)
    cos_t, sin_t = _rope_tables(pos0, t)
    v_first = None
    new_rwkv, new_shift, new_ret, new_hgrn, new_gla = [], [], [], [], []
    for l in range(DEPTH):
        w = layers[l]
        x_in = x
        x_last = c_shift[l]
        p2, aux2 = _in_proj(x.reshape(n, d), _previous_rows(x, x_last, tm_proj), w["win"], w["wl1"],
                            w["wl1mu"], w["w2"], w["bias"], tm_proj)
        p3 = p2.reshape(b, t, IN_COLS)
        aux3 = aux2.reshape(b, t, AUX_COLS)
        pad = (-b) % SUBLANE
        x_last_p = jnp.concatenate([x_last, jnp.zeros((pad, d), F32)], axis=0) if pad else x_last
        rkv_last = _rows_matmul(x_last_p, w["win"][:, :RWKV_COLS])[:b, None, :]

        res = _rwkv_mixer(p3, aux3, v_first, rkv_last, _head_rows(s_rwkv[l]), w["rwkv_prm"], consts, tt)
        if v_first is None:
            o_rwkv, v_first, st_rwkv = res
        else:
            o_rwkv, st_rwkv = res
        o_ret, st_ret = _ret_mixer(p3, cos_t, sin_t, _head_rows(s_ret[l]), w["ret_prm"], consts, tt)
        o_hgrn, st_hgrn = _hgrn_mixer(p3, _head_rows(s_hgrn[l]), prm["hgrn_lb_logits"], w["hgrn_ng"],
                                      consts, l, tt)
        o_gla, st_gla = _gla_mixer(p3, aux3, _head_rows(s_gla[l]), w["gla_ng"], consts, tt)

        outs = [o.reshape(n, BR_WIDTH) for o in (o_rwkv, o_ret, o_hgrn, o_gla)]
        is_moe = l % 2 == 1
        merged = _merge(x.reshape(n, d), outs, w["wg"], w["bg"], w["wbr"], w["wo"], w["ln1"], tm_moe, is_moe)
        if not is_moe:
            x1 = merged[0]
            x2 = _ffn(x1, *w["ffn"], w["ln2"], tm_moe)
        else:
            x1, x1b = merged
            wr, br, mg, mu_, md = w["moe"]
            tril_m = jnp.asarray(np.tril(np.ones((tm_moe, tm_moe), np.float32), -1), BF16)
            rank, wsel, cnt = _router(x1, wr, br, tril_m, tm_moe)
            rank_row = rank[:, :N_EXPERTS].T.reshape(N_EXPERTS, 1, n)
            per_sub = cnt[:, :tm_moe // MOE_SUB, :N_EXPERTS]
            ends = jnp.cumsum(per_sub, axis=1)
            x2 = _moe(x1, x1b, ends[:, -1, :].reshape(-1), (ends - per_sub).reshape(-1), ends.reshape(-1),
                      rank_row, rank, wsel, mg, mu_, md, w["ln2"], tm_moe, rows)
        x = x2.reshape(b, t, d)

        new_rwkv.append(st_rwkv.reshape(b, N_HEADS, HEAD_DIM, HEAD_DIM))
        new_shift.append(x_in[:, -1])
        new_ret.append(st_ret.reshape(b, N_HEADS, HEAD_DIM, HEAD_DIM))
        new_hgrn.append(st_hgrn.reshape(b, N_HEADS, HEAD_DIM, HEAD_DIM))
        new_gla.append(st_gla.reshape(b, N_HEADS, GLA_DK, HEAD_DIM))
    return (x, jnp.stack(new_rwkv), jnp.stack(new_shift), jnp.stack(new_ret), jnp.stack(new_hgrn),
            jnp.stack(new_gla))


def kernel(x_prompt, x_sample, state_rwkv, cache_shift, state_ret, state_hgrn, state_gla, w_in, rwkv_mu_rkv, rwkv_mu_x, rwkv_mu_v, rwkv_w0, rwkv_w1, rwkv_w2, rwkv_a0, rwkv_a1, rwkv_a2, rwkv_v0, rwkv_v1, rwkv_v2, rwkv_g1, rwkv_g2, rwkv_k_k, rwkv_k_a, rwkv_r_k, rwkv_ln_g, rwkv_ln_b, ret_gn_g, ret_gn_b, hgrn_lb_logits, hgrn_norm_g, gla_w1, gla_w2, gla_b, gla_norm_g, w_br, w_gate, b_gate, w_o, ln1_g, ln1_b, ln2_g, ln2_b, ffn_w_gate, ffn_w_up, ffn_w_down, router_w, router_b, moe_w_gate, moe_w_up, moe_w_down):
    prm = {
        'w_in': w_in, 'rwkv_mu_rkv': rwkv_mu_rkv, 'rwkv_mu_x': rwkv_mu_x, 'rwkv_mu_v': rwkv_mu_v,
        'rwkv_w0': rwkv_w0, 'rwkv_w1': rwkv_w1, 'rwkv_w2': rwkv_w2,
        'rwkv_a0': rwkv_a0, 'rwkv_a1': rwkv_a1, 'rwkv_a2': rwkv_a2,
        'rwkv_v0': rwkv_v0, 'rwkv_v1': rwkv_v1, 'rwkv_v2': rwkv_v2,
        'rwkv_g1': rwkv_g1, 'rwkv_g2': rwkv_g2, 'rwkv_k_k': rwkv_k_k, 'rwkv_k_a': rwkv_k_a,
        'rwkv_r_k': rwkv_r_k, 'rwkv_ln_g': rwkv_ln_g, 'rwkv_ln_b': rwkv_ln_b,
        'ret_gn_g': ret_gn_g, 'ret_gn_b': ret_gn_b, 'hgrn_lb_logits': hgrn_lb_logits,
        'hgrn_norm_g': hgrn_norm_g, 'gla_w1': gla_w1, 'gla_w2': gla_w2, 'gla_b': gla_b,
        'gla_norm_g': gla_norm_g, 'w_br': w_br, 'w_gate': w_gate, 'b_gate': b_gate, 'w_o': w_o,
        'ln1_g': ln1_g, 'ln1_b': ln1_b, 'ln2_g': ln2_g, 'ln2_b': ln2_b,
        'ffn_w_gate': ffn_w_gate, 'ffn_w_up': ffn_w_up, 'ffn_w_down': ffn_w_down,
        'router_w': router_w, 'router_b': router_b,
        'moe_w_gate': moe_w_gate, 'moe_w_up': moe_w_up, 'moe_w_down': moe_w_down,
    }
    layers = [_prep_layer(l, prm) for l in range(DEPTH)]
    consts = _mixer_consts()
    bp = x_prompt.shape[0]
    zero_hd = jnp.zeros((DEPTH, bp, N_HEADS, HEAD_DIM, HEAD_DIM), F32)
    zero_shift = jnp.zeros((DEPTH, bp, D_MODEL), F32)
    zero_gla = jnp.zeros((DEPTH, bp, N_HEADS, GLA_DK, HEAD_DIM), F32)
    prompt = _run_trunk(x_prompt, 0.0, zero_hd, zero_shift, zero_hd, zero_hd, zero_gla, prm, layers, consts)
    sample = _run_trunk(x_sample, float(PAST_LEN), state_rwkv, cache_shift, state_ret, state_hgrn,
                        state_gla, prm, layers, consts)
    y_p, p_rwkv, p_shift, p_ret, p_hgrn, p_gla = prompt
    y_s, s_rwkv, s_shift, s_ret, s_hgrn, s_gla = sample
    return (y_p, y_s, p_rwkv, p_shift, p_ret, p_hgrn, p_gla, s_rwkv, s_shift, s_ret, s_hgrn, s_gla)
```

```python
import functools
import math

import numpy as np
import jax
import jax.numpy as jnp
import jax.scipy.linalg
from jax import lax
from jax.experimental import pallas as pl
from jax.experimental.pallas import tpu as pltpu

F32 = jnp.float32
BF16 = jnp.bfloat16

D_MODEL = 1024
DEPTH = 2
PAST_LEN = 4096
CHUNK = 64
SUB = 16
FSUB = 32
N_BRANCH = 4
BR_WIDTH = D_MODEL // N_BRANCH
HEAD_DIM = 64
N_HEADS = BR_WIDTH // HEAD_DIM
GLA_DK = HEAD_DIM // 2
GLA_KW = N_HEADS * GLA_DK
GLA_GATE_RANK = 16
GLA_TAU = 16.0
RWKV_W_LORA = 32
RWKV_A_LORA = 32
RWKV_V_LORA = 16
RWKV_G_LORA = 64
RWKV_GN_EPS = 64e-5
ROPE_THETA = 10000.0
LN_EPS = 1e-5
D_FF = 2816
N_EXPERTS = 8
ALPHA = (2.0 * DEPTH) ** 0.25
RWKV_COLS = 3 * BR_WIDTH
IN_COLS = 3584
LORA_COLS = 256
AUX_COLS = 4 * BR_WIDTH + GLA_KW
SEQ_GROUP = 8
FACTOR_LIMIT = 80.0
MOE_LAST_SIZES = 5

MOE_SUB = 256
MOE_WIN = 112
MOE_TAIL = 128

LANE = 128
SUBLANE = 8
BF16_ROWS = 16
LOG2_E = 1.4426950408889634

NN = (((1,), (0,)), ((), ()))
NT = (((1,), (1,)), ((), ()))
TN = (((0,), (0,)), ((), ()))


def _params(sem, vmem_mib):
    return pltpu.CompilerParams(dimension_semantics=sem, vmem_limit_bytes=vmem_mib * 1024 * 1024)


def _const_spec(shape):
    nd = len(shape)
    return pl.BlockSpec(shape, lambda *_: (0,) * nd, pipeline_mode=pl.Buffered(1))


def _dot(a, b, dims=NN):
    return lax.dot_general(a.astype(BF16), b.astype(BF16), dims, preferred_element_type=F32)


def _split(x):
    hi = x.astype(BF16)
    lo = (x - hi.astype(F32)).astype(BF16)
    return hi, lo


def _dot_exact_lhs(a_bf, x):
    hi, lo = _split(x)
    return (jnp.dot(a_bf, hi, preferred_element_type=F32)
            + jnp.dot(a_bf, lo, preferred_element_type=F32))


def _dot_exact_rhs(x, b_bf):
    hi, lo = _split(x)
    return (jnp.dot(hi, b_bf, preferred_element_type=F32)
            + jnp.dot(lo, b_bf, preferred_element_type=F32))


def _dot3(a, b, dims=NN):
    ah, al = _split(a)
    bh, bl = _split(b)
    d = functools.partial(lax.dot_general, dimension_numbers=dims, preferred_element_type=F32)
    return d(ah, bh) + (d(ah, bl) + d(al, bh))


def _select_dot(x, sel_bf, dims, x_first):
    hi = x.astype(BF16)
    rest = x - hi.astype(F32)
    mid = rest.astype(BF16)
    lo = (rest - mid.astype(F32)).astype(BF16)
    d = functools.partial(lax.dot_general, dimension_numbers=dims, preferred_element_type=F32)
    out = None
    for piece in (hi, mid, lo):
        term = d(piece, sel_bf) if x_first else d(sel_bf, piece)
        out = term if out is None else out + term
    return out


def _state_from_value_rows(nat, unfold_bf, mask):
    return _select_dot(nat, unfold_bf, NN, True) * mask


def _state_to_value_rows(st, fold_bf):
    return _select_dot(st, fold_bf, NN, True)


def _state_from_key_rows(nat, fold_bf, mask):
    return _select_dot(nat, fold_bf, NT, False) * mask


def _state_to_key_rows(st, fold_bf):
    return _select_dot(st, fold_bf, TN, True)


def _sigmoid(x):
    return 1.0 / (1.0 + jnp.exp(-x))


def _softplus(x):
    return jnp.maximum(x, 0.0) + jnp.log(1.0 + jnp.exp(-jnp.abs(x)))


def _silu(x):
    return x * _sigmoid(x)


def _layer_norm_rows(y, g, b):
    mu = jnp.mean(y, axis=-1, keepdims=True)
    yc = y - mu
    var = jnp.mean(yc * yc, axis=-1, keepdims=True)
    return yc * lax.rsqrt(var + LN_EPS) * g + b


def _np_block_mask(rows_per_head, cols_per_head):
    r = np.arange(N_HEADS * rows_per_head)[:, None] // rows_per_head
    c = np.arange(N_HEADS * cols_per_head)[None, :] // cols_per_head
    return (r == c).astype(np.float32)


def _np_head_lanes(cols_per_head):
    m = np.zeros((SUBLANE, N_HEADS * cols_per_head), np.float32)
    for h in range(N_HEADS):
        m[h, h * cols_per_head:(h + 1) * cols_per_head] = 1.0
    return m


def _np_causal_side_by_side(strict):
    t = np.arange(CHUNK)[:, None]
    s = np.arange(N_HEADS * CHUNK)[None, :] % CHUNK
    return ((s < t) if strict else (s <= t)).astype(np.float32)


def _in_proj_kernel(x_ref, prev_ref, win_ref, wl1_ref, wl1mu_ref, w2_ref, bias_ref, p_ref, aux_ref):
    x = x_ref[...]
    row = lax.broadcasted_iota(jnp.int32, x.shape, 0)
    xx = jnp.where(row == 0, prev_ref[0], pltpu.roll(x, 1, axis=0)) - x
    xb = x.astype(BF16)
    h = (jnp.dot(xb, wl1_ref[...], preferred_element_type=F32)
         + jnp.dot(xx.astype(BF16), wl1mu_ref[...], preferred_element_type=F32))
    lane = lax.broadcasted_iota(jnp.int32, h.shape, 1)
    act = jnp.where(lane < RWKV_W_LORA, jnp.tanh(h), h)
    g_lo = RWKV_W_LORA + RWKV_A_LORA
    in_g = jnp.where(lane >= g_lo, jnp.where(lane < g_lo + RWKV_G_LORA, 1.0, 0.0), 0.0)
    act = jnp.where(in_g > 0.5, _sigmoid(h), act)
    z = jnp.dot(act.astype(BF16), w2_ref[...], preferred_element_type=F32) + bias_ref[...]
    p_ref[...] = jnp.dot(xb, win_ref[...], preferred_element_type=F32)
    bw = BR_WIDTH
    w_log = -_softplus(-z[:, 0:bw]) - 0.5
    aux_ref[:, 0:bw] = -jnp.exp(w_log)
    aux_ref[:, bw:2 * bw] = _sigmoid(z[:, bw:2 * bw])
    aux_ref[:, 2 * bw:3 * bw] = z[:, 2 * bw:3 * bw]
    aux_ref[:, 3 * bw:4 * bw] = _sigmoid(z[:, 3 * bw:4 * bw])
    zg = z[:, 4 * bw:]
    aux_ref[:, 4 * bw:] = (jnp.minimum(zg, 0.0) - jnp.log(1.0 + jnp.exp(-jnp.abs(zg)))) * (1.0 / GLA_TAU)


def _in_proj(x2, prev_rows, win, wl1, wl1mu, w2, bias, tm):
    n = x2.shape[0]
    return pl.pallas_call(
        _in_proj_kernel,
        grid=(n // tm,),
        in_specs=[
            pl.BlockSpec((tm, D_MODEL), lambda i: (i, 0)),
            pl.BlockSpec((1, 1, D_MODEL), lambda i: (i, 0, 0)),
            _const_spec(win.shape), _const_spec(wl1.shape), _const_spec(wl1mu.shape),
            _const_spec(w2.shape), _const_spec(bias.shape),
        ],
        out_specs=[pl.BlockSpec((tm, IN_COLS), lambda i: (i, 0)),
                   pl.BlockSpec((tm, AUX_COLS), lambda i: (i, 0))],
        out_shape=[jax.ShapeDtypeStruct((n, IN_COLS), F32), jax.ShapeDtypeStruct((n, AUX_COLS), F32)],
        compiler_params=_params(("arbitrary",), 56),
        name="in_proj",
    )(x2, prev_rows, win, wl1, wl1mu, w2, bias)


def _rows_matmul_kernel(x_ref, w_ref, o_ref):
    o_ref[...] = jnp.dot(x_ref[...].astype(BF16), w_ref[...], preferred_element_type=F32)


def _rows_matmul(x, w):
    return pl.pallas_call(
        _rows_matmul_kernel,
        out_shape=jax.ShapeDtypeStruct((x.shape[0], w.shape[1]), F32),
        name="rows_matmul",
    )(x, w)


def _stack_heads(x, head_lanes):
    xb = x.astype(BF16)
    return jnp.concatenate([xb * head_lanes[h:h + 1, :].astype(BF16) for h in range(N_HEADS)], axis=0)


def _head_sum(x, ones_bf):
    return _dot_exact_rhs(x, ones_bf)


def _head_layer_norm(y, ones_bf, g, b, eps):
    inv = 1.0 / HEAD_DIM
    mu = _head_sum(y, ones_bf) * inv
    yc = y - mu
    var = _head_sum(yc * yc, ones_bf) * inv
    return yc * lax.rsqrt(var + eps) * g + b


def _head_rms_norm(y, ones_bf, g):
    ms = _head_sum(y * y, ones_bf) * (1.0 / HEAD_DIM)
    return y * lax.rsqrt(ms + 1e-6) * g


def _pairwise_block(q, k, v, b2, pair_ones_bf):
    parts = []
    for j in range(SUB):
        lo = (j // SUBLANE) * SUBLANE
        p = q[lo:] * jnp.exp2(b2[lo:] - b2[j:j + 1]) * k[j:j + 1]
        if j % SUBLANE:
            rid = lax.broadcasted_iota(jnp.int32, p.shape, 0) + lo
            p = jnp.where(rid >= j, p, 0.0)
        parts.append(p)
    att = jnp.dot(jnp.concatenate(parts, axis=0).astype(BF16), pair_ones_bf, preferred_element_type=F32)
    outs = []
    off = 0
    for g in range(SUB // SUBLANE):
        rows = SUB - g * SUBLANE
        acc = None
        for j in range(g * SUBLANE, (g + 1) * SUBLANE):
            term = att[off:off + rows] * v[j:j + 1]
            acc = term if acc is None else acc + term
            off += rows
        if g:
            acc = jnp.concatenate([jnp.zeros((g * SUBLANE, v.shape[1]), F32), acc], axis=0)
        outs.append(acc)
    total = outs[0]
    for extra in outs[1:]:
        total = total + extra
    return total


def _gla_intra_pairwise(tiles, b, pair_ones_bf, lanes_k, lanes_v):
    n = range(len(tiles))
    q, k, v, glog = ([t[i] for t in tiles] for i in range(4))
    b2 = [b[i] * LOG2_E for i in n]
    blocks = [[] for _ in n]
    for blk in range(CHUNK // SUB):
        r0 = blk * SUB
        sl = slice(r0, r0 + SUB)
        o_blk = [_pairwise_block(q[i][sl], k[i][sl], v[i][sl], b2[i][sl], pair_ones_bf) for i in n]
        if blk:
            c0 = [b[i][r0 - 1:r0] for i in n]
            q_t = [q[i][sl] * jnp.exp(b[i][sl] - c0[i]) for i in n]
            k_t = [k[i][:r0] * jnp.exp(c0[i] - b[i][:r0]) for i in n]
            att = [_dot(q_t[i], _stack_heads(k_t[i], lanes_k), NT) for i in n]
            o_blk = [o_blk[i] + _dot(att[i], _stack_heads(v[i][:r0], lanes_v)) for i in n]
        for i in n:
            blocks[i].append(o_blk[i])
    return [jnp.concatenate(blocks[i], axis=0) for i in n]


def _gla_intra_factored(tiles, b, lanes_k, lanes_v):
    n = range(len(tiles))
    q, k, v, _ = ([t[i] for t in tiles] for i in range(4))
    blocks = [[] for _ in n]
    for blk in range(CHUNK // FSUB):
        r0 = blk * FSUB
        upto = r0 + FSUB
        sl = slice(r0, upto)
        if blk:
            c0 = [b[i][r0 - 1:r0] for i in n]
            q_t = [q[i][sl] * jnp.exp(b[i][sl] - c0[i]) for i in n]
            k_t = [k[i][:upto] * jnp.exp(c0[i] - b[i][:upto]) for i in n]
        else:
            q_t = [q[i][sl] * jnp.exp(b[i][sl]) for i in n]
            k_t = [k[i][:upto] * jnp.exp(-b[i][:upto]) for i in n]
        att = [_dot(q_t[i], _stack_heads(k_t[i], lanes_k), NT) for i in n]
        t_id = lax.broadcasted_iota(jnp.int32, (FSUB, N_HEADS * upto), 0) + r0
        s_id = lax.broadcasted_iota(jnp.int32, (FSUB, N_HEADS * upto), 1) % upto
        causal = s_id <= t_id
        o_blk = [_dot(jnp.where(causal, att[i], 0.0), _stack_heads(v[i][:upto], lanes_v)) for i in n]
        for i in n:
            blocks[i].append(o_blk[i])
    return [jnp.concatenate(blocks[i], axis=0) for i in n]


def _gla_block_range(b):
    worst = -b[FSUB - 1:FSUB, :]
    for blk in range(1, CHUNK // FSUB):
        r0 = blk * FSUB
        worst = jnp.maximum(worst, b[r0 - 1:r0, :] - b[r0 + FSUB - 1:r0 + FSUB, :])
    return worst


def _gla_state_parts(tiles, b, st_mask):
    n = range(len(tiles))
    q, k, v, _ = ([t[i] for t in tiles] for i in range(4))
    b_last = [b[i][CHUNK - 1:CHUNK, :] for i in n]
    upd = [st_mask * _dot(v[i], k[i] * jnp.exp(b_last[i] - b[i]), TN) for i in n]
    return [(q[i] * jnp.exp(b[i]), upd[i], jnp.exp(b_last[i])) for i in n]


def _rwkv_state_free(tiles, tril_bf, hl, strict, incl, eye):
    n = range(len(tiles))
    r, k, v, kk, bv, lw = ([t[i] for t in tiles] for i in range(6))
    l = [_dot_exact_lhs(tril_bf, lw[i]) for i in n]
    l_last = [l[i][CHUNK - 1:CHUNK, :] for i in n]
    e_neg = [jnp.exp(-l[i]) for i in n]
    lhs = [jnp.concatenate([kk[i] * jnp.exp(l[i] - lw[i]), r[i] * jnp.exp(l[i])], axis=0) for i in n]
    rhs = [jnp.concatenate([_stack_heads(k[i] * e_neg[i], hl), _stack_heads(bv[i] * e_neg[i], hl)], axis=0)
           for i in n]
    amat = [_dot(lhs[i], rhs[i], NT) for i in n]
    w = N_HEADS * CHUNK
    a_ab = [amat[i][:CHUNK, w:] * strict for i in n]
    a_vk = [jnp.concatenate([amat[i][:CHUNK, :w] * strict, amat[i][CHUNK:, :w] * incl], axis=0) for i in n]
    a_rb = [amat[i][CHUNK:, w:] * incl for i in n]
    x = [eye + a_ab[i] for i in n]
    m_st = [_stack_heads(a_ab[i], hl) for i in n]
    m = [_dot(a_ab[i], m_st[i]) for i in n]
    levels = int(math.log2(CHUNK))
    for level in range(1, levels):
        m_st = [_stack_heads(m[i], hl) for i in n]
        if level < levels - 1:
            both = [_dot(jnp.concatenate([m[i], x[i]], axis=0), m_st[i]) for i in n]
            m = [both[i][:CHUNK] for i in n]
            x = [x[i] + both[i][CHUNK:] for i in n]
        else:
            x = [x[i] + _dot(x[i], m_st[i]) for i in n]
    from_v = [_dot(a_vk[i], _stack_heads(v[i], hl)) for i in n]
    e_end = [jnp.exp(l_last[i] - l[i]) for i in n]
    upd_v = [_dot(v[i], k[i] * e_end[i], TN) for i in n]
    b_end = [bv[i] * e_end[i] for i in n]
    st_dec = [jnp.exp(l_last[i]) for i in n]
    return [(lhs[i], x[i], a_rb[i], from_v[i], upd_v[i], b_end[i], st_dec[i]) for i in n]


def _rwkv_state_step(parts, sts, bdm, hl):
    n = range(len(parts))
    lhs, x, a_rb, from_v, upd_v, b_end, st_dec = ([p[i] for p in parts] for i in range(7))
    from_state = [_dot(lhs[i], sts[i], NT) for i in n]
    u = [_dot(x[i], _stack_heads(from_state[i][:CHUNK] + from_v[i][:CHUNK], hl)) for i in n]
    upd = [upd_v[i] + _dot(u[i], b_end[i], TN) for i in n]
    new = [sts[i] * st_dec[i] + upd[i] * bdm for i in n]
    y = [from_state[i][CHUNK:] + from_v[i][CHUNK:] + _dot(a_rb[i], _stack_heads(u[i], hl)) for i in n]
    return y, new


def _rwkv_kernel(has_vres, *refs):
    if has_vres:
        (rkv_ref, lw_ref, a_ref, g_ref, vg_ref, vf_ref, last_ref, st0_ref, prm_ref, tril_ref, bdm_ref,
         hl_ref, strict_ref, incl_ref, eye_ref, fold_ref, unfold_ref, o_ref, st_out_ref, st_sc,
         prev_sc) = refs
    else:
        (rkv_ref, lw_ref, a_ref, g_ref, last_ref, st0_ref, prm_ref, tril_ref, bdm_ref,
         hl_ref, strict_ref, incl_ref, eye_ref, fold_ref, unfold_ref, o_ref, v_out_ref, st_out_ref, st_sc,
         prev_sc) = refs
    tb = pl.program_id(1)

    @pl.when(tb == 0)
    def _():
        for s in range(st0_ref.shape[0]):
            st_sc[s] = _state_from_value_rows(st0_ref[s], unfold_ref[...], bdm_ref[...])
        prev_sc[...] = last_ref[...]

    bw = BR_WIDTH
    k_k = prm_ref[1:2, 0:bw]
    k_a = prm_ref[2:3, 0:bw]
    r_k = prm_ref[3:4, 0:bw]
    ln_g = prm_ref[4:5, 0:bw]
    ln_b = prm_ref[5:6, 0:bw]
    bdm = bdm_ref[...]
    ones_bf = bdm.astype(BF16)
    hl = hl_ref[...]
    tril_bf = tril_ref[...]
    strict = strict_ref[...]
    incl = incl_ref[...]
    eye = eye_ref[...]
    n_seq = rkv_ref.shape[0]
    tt = rkv_ref.shape[1]
    seqs = []
    for s in range(n_seq):
        rkv = rkv_ref[s]
        row = lax.broadcasted_iota(jnp.int32, rkv.shape, 0)
        prev = jnp.where(row == 0, prev_sc[s], pltpu.roll(rkv, 1, axis=0))
        prev_sc[s] = rkv[tt - 1:tt, :]
        mixed = rkv + (prev - rkv) * prm_ref[0:1, :]
        r = mixed[:, 0:bw]
        k = mixed[:, bw:2 * bw]
        v = mixed[:, 2 * bw:]
        a = a_ref[s]
        if has_vres:
            v = v + (vf_ref[s] - v) * vg_ref[s]
        else:
            v_out_ref[s] = v
        kk = k * k_k
        kk = kk * lax.rsqrt(jnp.maximum(_head_sum(kk * kk, ones_bf), 1e-24))
        k = k * (1.0 + (a - 1.0) * k_a)
        seqs.append((r, k, v, kk, -(kk * a), lw_ref[s]))
    n_chunks = tt // CHUNK
    tiles = [tuple(z[c * CHUNK:(c + 1) * CHUNK] for z in seqs[s])
             for c in range(n_chunks) for s in range(n_seq)]
    parts = _rwkv_state_free(tiles, tril_bf, hl, strict, incl, eye)
    sts = [st_sc[s] for s in range(n_seq)]
    ys = [[] for _ in range(n_seq)]
    for c in range(n_chunks):
        y_c, sts = _rwkv_state_step(parts[c * n_seq:(c + 1) * n_seq], sts, bdm, hl)
        for s in range(n_seq):
            ys[s].append(y_c[s])
    for s in range(n_seq):
        r, k, v = seqs[s][0:3]
        st_sc[s] = sts[s]
        y = ys[s][0] if len(ys[s]) == 1 else jnp.concatenate(ys[s], axis=0)
        y = _head_layer_norm(y, ones_bf, ln_g, ln_b, RWKV_GN_EPS)
        bonus = _head_sum(r * k * r_k, ones_bf) * v
        o_ref[s] = (y + bonus) * g_ref[s]

    @pl.when(tb == pl.num_programs(1) - 1)
    def _():
        for s in range(n_seq):
            st_out_ref[s] = _state_to_value_rows(sts[s], fold_ref[...])


def _rwkv_mixer(p3, aux3, v_first, rkv_last, st0, prm, consts, tt):
    b, t, _ = p3.shape
    bw = BR_WIDTH
    pb = min(b, SEQ_GROUP)
    has_vres = v_first is not None
    tok = lambda j: pl.BlockSpec((pb, tt, bw), lambda bi, ti, j=j: (bi, ti, j))
    in_specs = [pl.BlockSpec((pb, tt, RWKV_COLS), lambda bi, ti: (bi, ti, 0)),
                tok(0), tok(1), tok(2)]
    args = [p3, aux3, aux3, aux3]
    if has_vres:
        in_specs += [tok(3), pl.BlockSpec((pb, tt, bw), lambda bi, ti: (bi, ti, 0))]
        args += [aux3, v_first]
    st_spec = pl.BlockSpec((pb, bw, HEAD_DIM), lambda bi, ti: (bi, 0, 0))
    in_specs += [pl.BlockSpec((pb, 1, RWKV_COLS), lambda bi, ti: (bi, 0, 0)), st_spec,
                 _const_spec(prm.shape)]
    args += [rkv_last, st0, prm]
    for name in ("tril", "bdm", "head_lanes", "strict", "incl", "eye", "fold", "unfold"):
        in_specs.append(_const_spec(consts[name].shape))
        args.append(consts[name])
    seq = pl.BlockSpec((pb, tt, bw), lambda bi, ti: (bi, ti, 0))
    seq_shape = jax.ShapeDtypeStruct((b, t, bw), F32)
    st_shape = jax.ShapeDtypeStruct((b, bw, HEAD_DIM), F32)
    if has_vres:
        out_specs, out_shape = [seq, st_spec], [seq_shape, st_shape]
    else:
        out_specs, out_shape = [seq, seq, st_spec], [seq_shape, seq_shape, st_shape]
    return pl.pallas_call(
        functools.partial(_rwkv_kernel, has_vres),
        grid=(b // pb, t // tt),
        in_specs=in_specs, out_specs=out_specs, out_shape=out_shape,
        scratch_shapes=[pltpu.VMEM((pb, bw, bw), F32), pltpu.VMEM((pb, 1, RWKV_COLS), F32)],
        compiler_params=_params(("arbitrary", "arbitrary"), 48),
        name="rwkv_mixer",
    )(*args)


def _rot_half(z):
    w = z.shape[1]
    half = HEAD_DIM // 2
    lane = lax.broadcasted_iota(jnp.int32, z.shape, 1)
    first = (lane % HEAD_DIM) < half
    return jnp.where(first, pltpu.roll(z, w - half, axis=1), pltpu.roll(z, half, axis=1))


def _ret_kernel(q_ref, k_ref, v_ref, g_ref, cos_ref, sin_ref, st0_ref, prm_ref, dec_ref, bdm_ref, hl_ref,
                fold_ref, o_ref, st_out_ref, st_sc):
    tb = pl.program_id(1)

    @pl.when(tb == 0)
    def _():
        for s in range(st0_ref.shape[0]):
            st_sc[s] = _state_from_key_rows(st0_ref[s], fold_ref[...], bdm_ref[...])

    cos = cos_ref[...]
    sin = sin_ref[...]
    bdm = bdm_ref[...]
    hl = hl_ref[...]
    ones_bf = bdm.astype(BF16)
    q_dec = dec_ref[0:CHUNK, :]
    k_dec = dec_ref[CHUNK:2 * CHUNK, :]
    d_mat = dec_ref[2 * CHUNK:3 * CHUNK, :]
    s_dec = dec_ref[3 * CHUNK:3 * CHUNK + 1, :]
    n_seq = q_ref.shape[0]
    tt = q_ref.shape[1]
    qs, ks, vs = [], [], []
    for s in range(n_seq):
        q = q_ref[s]
        k = k_ref[s]
        qs.append(q * cos + _rot_half(q) * sin)
        ks.append((k * cos + _rot_half(k) * sin) * (HEAD_DIM ** -0.5))
        vs.append(v_ref[s])
    sts = [st_sc[s] for s in range(n_seq)]
    outs = [[] for _ in range(n_seq)]
    for c in range(tt // CHUNK):
        sl = slice(c * CHUNK, (c + 1) * CHUNK)
        for s in range(n_seq):
            qc, kc, vc = qs[s][sl], ks[s][sl], vs[s][sl]
            att = _dot(qc, _stack_heads(kc, hl), NT) * d_mat
            outs[s].append(_dot(qc * q_dec, sts[s], NT) + _dot(att, _stack_heads(vc, hl)))
            sts[s] = sts[s] * s_dec + bdm * _dot(vc, kc * k_dec, TN)
    for s in range(n_seq):
        st_sc[s] = sts[s]
        o = outs[s][0] if len(outs[s]) == 1 else jnp.concatenate(outs[s], axis=0)
        o = _head_layer_norm(o, ones_bf, prm_ref[0:1, :], prm_ref[1:2, :], LN_EPS)
        o_ref[s] = o * _silu(g_ref[s])

    @pl.when(tb == pl.num_programs(1) - 1)
    def _():
        for s in range(n_seq):
            st_out_ref[s] = _state_to_key_rows(sts[s], fold_ref[...])


def _ret_mixer(p3, cos_t, sin_t, st0, prm, consts, tt):
    b, t, _ = p3.shape
    bw = BR_WIDTH
    pb = min(b, SEQ_GROUP)
    base = RWKV_COLS // bw
    tok = lambda j: pl.BlockSpec((pb, tt, bw), lambda bi, ti, j=j: (bi, ti, base + j))
    tab = pl.BlockSpec((tt, bw), lambda bi, ti: (ti, 0))
    st_spec = pl.BlockSpec((pb, bw, HEAD_DIM), lambda bi, ti: (bi, 0, 0))
    dec, bdm, hl, fold = consts["ret_dec"], consts["bdm"], consts["head_lanes"], consts["fold"]
    return pl.pallas_call(
        _ret_kernel,
        grid=(b // pb, t // tt),
        in_specs=[tok(0), tok(1), tok(2), tok(3), tab, tab, st_spec, _const_spec(prm.shape),
                  _const_spec(dec.shape), _const_spec(bdm.shape), _const_spec(hl.shape),
                  _const_spec(fold.shape)],
        out_specs=[pl.BlockSpec((pb, tt, bw), lambda bi, ti: (bi, ti, 0)), st_spec],
        out_shape=[jax.ShapeDtypeStruct((b, t, bw), F32), jax.ShapeDtypeStruct((b, bw, HEAD_DIM), F32)],
        scratch_shapes=[pltpu.VMEM((pb, bw, bw), F32)],
        compiler_params=_params(("arbitrary", "arbitrary"), 40),
        name="ret_mixer",
    )(p3, p3, p3, p3, cos_t, sin_t, st0, prm, dec, bdm, hl, fold)


def _gated_mixer_tail(seqs, norm_g, st_sc, intra_sc, st_out_ref, o_ref, tril_bf, pair_ones_bf, st_mask,
                      lanes_k, lanes_v, ones_v_bf, fold_bf):
    n_seq = len(seqs)
    tt = seqs[0][0].shape[0]
    n_chunks = tt // CHUNK
    tiles = [tuple(z[c * CHUNK:(c + 1) * CHUNK] for z in seqs[s][0:4])
             for c in range(n_chunks) for s in range(n_seq)]
    b = [_dot_exact_lhs(tril_bf, t[3]) for t in tiles]
    worst = _gla_block_range(b[0])
    for b_i in b[1:]:
        worst = jnp.maximum(worst, _gla_block_range(b_i))
    risky = jnp.max(worst) > FACTOR_LIMIT

    def put_intra(intra):
        for idx, o_i in enumerate(intra):
            c_i, s_i = divmod(idx, n_seq)
            intra_sc[s_i, c_i * CHUNK:(c_i + 1) * CHUNK, :] = o_i

    @pl.when(risky)
    def _():
        put_intra(_gla_intra_pairwise(tiles, b, pair_ones_bf, lanes_k, lanes_v))

    @pl.when(jnp.logical_not(risky))
    def _():
        put_intra(_gla_intra_factored(tiles, b, lanes_k, lanes_v))

    parts = _gla_state_parts(tiles, b, st_mask)
    sts = [st_sc[s] for s in range(n_seq)]
    outs = [[] for _ in range(n_seq)]
    for c in range(n_chunks):
        for s in range(n_seq):
            q_dec, upd, dec = parts[c * n_seq + s]
            outs[s].append(intra_sc[s, c * CHUNK:(c + 1) * CHUNK, :] + _dot(q_dec, sts[s], NT))
            sts[s] = sts[s] * dec + upd
    for s in range(n_seq):
        st_sc[s] = sts[s]
        o = outs[s][0] if len(outs[s]) == 1 else jnp.concatenate(outs[s], axis=0)
        o_ref[s] = _head_rms_norm(o, ones_v_bf, norm_g) * _silu(seqs[s][4])

    @pl.when(pl.program_id(1) == pl.num_programs(1) - 1)
    def _():
        for s in range(n_seq):
            st_out_ref[s] = _state_to_key_rows(sts[s], fold_bf)


def _hgrn_kernel(layer, q_ref, f_ref, i_ref, g_ref, st0_ref, lbl_ref, ng_ref, tril_ref, bdm_ref, hl_ref,
                 fold_ref, o_ref, st_out_ref, st_sc, intra_sc):
    @pl.when(pl.program_id(1) == 0)
    def _():
        for s in range(st0_ref.shape[0]):
            st_sc[s] = _state_from_key_rows(st0_ref[s], fold_ref[...], bdm_ref[...])

    logits = lbl_ref[...]
    ex = jnp.exp(logits - jnp.max(logits, axis=0, keepdims=True))
    sm = ex / jnp.sum(ex, axis=0, keepdims=True)
    lb = jnp.zeros((1, BR_WIDTH), F32)
    for d in range(1, layer + 1):
        lb = lb + sm[d:d + 1, :]
    seqs = []
    for s in range(q_ref.shape[0]):
        fz = f_ref[s]
        f = lb + (1.0 - lb) * _sigmoid(fz)
        k = (1.0 - lb) * _sigmoid(-fz)
        seqs.append((_silu(q_ref[s]), k, i_ref[s], jnp.log(f), g_ref[s]))
    bdm = bdm_ref[...]
    ones_bf = bdm.astype(BF16)
    hl = hl_ref[...]
    _gated_mixer_tail(seqs, ng_ref[...], st_sc, intra_sc, st_out_ref, o_ref, tril_ref[...], ones_bf, bdm,
                      hl, hl, ones_bf, fold_ref[...])


def _hgrn_mixer(p3, st0, lb_logits, norm_g, consts, layer, tt):
    b, t, _ = p3.shape
    bw = BR_WIDTH
    pb = min(b, SEQ_GROUP)
    base = (RWKV_COLS + 4 * bw) // bw
    tok = lambda j: pl.BlockSpec((pb, tt, bw), lambda bi, ti, j=j: (bi, ti, base + j))
    st_spec = pl.BlockSpec((pb, bw, HEAD_DIM), lambda bi, ti: (bi, 0, 0))
    tril, bdm, hl, fold = consts["tril"], consts["bdm"], consts["head_lanes"], consts["fold"]
    return pl.pallas_call(
        functools.partial(_hgrn_kernel, layer),
        grid=(b // pb, t // tt),
        in_specs=[tok(0), tok(1), tok(2), tok(3), st_spec, _const_spec(lb_logits.shape),
                  _const_spec(norm_g.shape), _const_spec(tril.shape), _const_spec(bdm.shape),
                  _const_spec(hl.shape), _const_spec(fold.shape)],
        out_specs=[pl.BlockSpec((pb, tt, bw), lambda bi, ti: (bi, ti, 0)), st_spec],
        out_shape=[jax.ShapeDtypeStruct((b, t, bw), F32), jax.ShapeDtypeStruct((b, bw, HEAD_DIM), F32)],
        scratch_shapes=[pltpu.VMEM((pb, bw, bw), F32), pltpu.VMEM((pb, tt, bw), F32)],
        compiler_params=_params(("arbitrary", "arbitrary"), 48),
        name="hgrn_mixer",
    )(p3, p3, p3, p3, st0, lb_logits, norm_g, tril, bdm, hl, fold)


def _gla_kernel(q_ref, k_ref, v_ref, g_ref, la_ref, st0_ref, ng_ref, tril_ref, pair_ref, mask_ref,
                bdm_ref, hlk_ref, hlv_ref, fold_ref, o_ref, st_out_ref, st_sc, intra_sc):
    @pl.when(pl.program_id(1) == 0)
    def _():
        for s in range(st0_ref.shape[0]):
            st_sc[s] = _state_from_key_rows(st0_ref[s], fold_ref[...], mask_ref[...])

    seqs = [(q_ref[s] * (GLA_DK ** -0.5), k_ref[s], v_ref[s], la_ref[s], g_ref[s])
            for s in range(q_ref.shape[0])]
    _gated_mixer_tail(seqs, ng_ref[...], st_sc, intra_sc, st_out_ref, o_ref, tril_ref[...], pair_ref[...],
                      mask_ref[...], hlk_ref[...], hlv_ref[...], bdm_ref[...].astype(BF16), fold_ref[...])


def _gla_mixer(p3, aux3, st0, norm_g, consts, tt):
    b, t, _ = p3.shape
    bw, kw = BR_WIDTH, GLA_KW
    pb = min(b, SEQ_GROUP)
    gla0 = RWKV_COLS + 8 * bw
    st_spec = pl.BlockSpec((pb, kw, HEAD_DIM), lambda bi, ti: (bi, 0, 0))
    names = ("tril", "gla_pair", "gla_mask", "bdm", "gla_head_lanes", "head_lanes", "fold")
    return pl.pallas_call(
        _gla_kernel,
        grid=(b // pb, t // tt),
        in_specs=[pl.BlockSpec((pb, tt, kw), lambda bi, ti: (bi, ti, gla0 // kw)),
                  pl.BlockSpec((pb, tt, kw), lambda bi, ti: (bi, ti, gla0 // kw + 1)),
                  pl.BlockSpec((pb, tt, bw), lambda bi, ti: (bi, ti, (gla0 + 2 * kw) // bw)),
                  pl.BlockSpec((pb, tt, bw), lambda bi, ti: (bi, ti, (gla0 + 2 * kw) // bw + 1)),
                  pl.BlockSpec((pb, tt, kw), lambda bi, ti: (bi, ti, 4 * bw // kw)),
                  st_spec, _const_spec(norm_g.shape)] + [_const_spec(consts[nm].shape) for nm in names],
        out_specs=[pl.BlockSpec((pb, tt, bw), lambda bi, ti: (bi, ti, 0)), st_spec],
        out_shape=[jax.ShapeDtypeStruct((b, t, bw), F32), jax.ShapeDtypeStruct((b, kw, HEAD_DIM), F32)],
        scratch_shapes=[pltpu.VMEM((pb, bw, kw), F32), pltpu.VMEM((pb, tt, bw), F32)],
        compiler_params=_params(("arbitrary", "arbitrary"), 48),
        name="gla_mixer",
    )(p3, p3, p3, p3, aux3, st0, norm_g, *[consts[nm] for nm in names])


def _merge_kernel(emit_bf16, x_ref, o0_ref, o1_ref, o2_ref, o3_ref, wg_ref, bg_ref, wbr_ref, wo_ref,
                  ln_ref, *out_refs):
    branches = (o0_ref, o1_ref, o2_ref, o3_ref)
    half = x_ref.shape[0] // 2
    rows = [slice(0, half), slice(half, 2 * half)]
    x = [x_ref[r, :] for r in rows]
    xb = [v.astype(BF16) for v in x]
    merged = [None, None]
    for m in range(N_BRANCH):
        pre = [jnp.dot(xb[h], wg_ref[m], preferred_element_type=F32) for h in range(2)]
        proj = [jnp.dot(branches[m][rows[h], :].astype(BF16), wbr_ref[m], preferred_element_type=F32)
                for h in range(2)]
        for h in range(2):
            term = _sigmoid(pre[h] + bg_ref[m:m + 1, :]) * proj[h]
            merged[h] = term if merged[h] is None else merged[h] + term
    y = [ALPHA * x[h] + jnp.dot(merged[h].astype(BF16), wo_ref[...], preferred_element_type=F32)
         for h in range(2)]
    for h in range(2):
        yn = _layer_norm_rows(y[h], ln_ref[0:1, :], ln_ref[1:2, :])
        out_refs[0][rows[h], :] = yn
        if emit_bf16:
            out_refs[1][rows[h], :] = yn.astype(BF16)


def _merge(x2, outs, wg, bg, wbr, wo, ln, tm, emit_bf16):
    n = x2.shape[0]
    row = pl.BlockSpec((tm, D_MODEL), lambda i: (i, 0))
    br = pl.BlockSpec((tm, BR_WIDTH), lambda i: (i, 0))
    out_specs = [row]
    out_shape = [jax.ShapeDtypeStruct((n, D_MODEL), F32)]
    if emit_bf16:
        out_specs.append(row)
        out_shape.append(jax.ShapeDtypeStruct((n, D_MODEL), BF16))
    return pl.pallas_call(
        functools.partial(_merge_kernel, emit_bf16),
        grid=(n // tm,),
        in_specs=[row, br, br, br, br, _const_spec(wg.shape), _const_spec(bg.shape),
                  _const_spec(wbr.shape), _const_spec(wo.shape), _const_spec(ln.shape)],
        out_specs=out_specs, out_shape=out_shape,
        compiler_params=_params(("arbitrary",), 60),
        name="merge",
    )(x2, *outs, wg, bg, wbr, wo, ln)


FF_SPLIT = 2
FF_PART = D_FF // FF_SPLIT
FFN_PARTS = (1024, 1024, 768)


def _ffn_kernel(x_ref, wg_ref, wu_ref, wd_ref, ln_ref, o_ref):
    x = x_ref[...]
    xb = x.astype(BF16)
    acc = ALPHA * x
    lo = 0
    for width in FFN_PARTS:
        cs = slice(lo, lo + width)
        lo += width
        h = (_silu(jnp.dot(xb, wg_ref[:, cs], preferred_element_type=F32))
             * jnp.dot(xb, wu_ref[:, cs], preferred_element_type=F32))
        acc = acc + jnp.dot(h.astype(BF16), wd_ref[cs, :], preferred_element_type=F32)
    o_ref[...] = _layer_norm_rows(acc, ln_ref[0:1, :], ln_ref[1:2, :])


def _ffn(x2, wg, wu, wd, ln, tm):
    n = x2.shape[0]
    row = pl.BlockSpec((tm, D_MODEL), lambda i: (i, 0))
    return pl.pallas_call(
        _ffn_kernel,
        grid=(n // tm,),
        in_specs=[row, _const_spec(wg.shape), _const_spec(wu.shape), _const_spec(wd.shape),
                  _const_spec(ln.shape)],
        out_specs=row,
        out_shape=jax.ShapeDtypeStruct((n, D_MODEL), F32),
        compiler_params=_params(("arbitrary",), 60),
        name="ffn",
    )(x2, wg, wu, wd, ln)


def _router_kernel(x_ref, wr_ref, br_ref, tril_ref, rank_ref, wsel_ref, cnt_ref):
    logits = _dot3(x_ref[...], wr_ref[...]) + br_ref[...]
    lane = lax.broadcasted_iota(jnp.int32, logits.shape, 1)
    neg = jnp.float32(-jnp.inf)
    logits = jnp.where(lane < N_EXPERTS, logits, neg)
    m1 = jnp.max(logits, axis=1, keepdims=True)
    lane_f = lane.astype(F32)
    i1 = jnp.min(jnp.where(logits == m1, lane_f, float(LANE)), axis=1, keepdims=True)
    first = lane_f == i1
    rest = jnp.where(first, neg, logits)
    m2 = jnp.max(rest, axis=1, keepdims=True)
    i2 = jnp.min(jnp.where(rest == m2, lane_f, float(LANE)), axis=1, keepdims=True)
    second = lane_f == i2
    e = jnp.exp(m2 - m1)
    w1 = 1.0 / (1.0 + e)
    w2 = e / (1.0 + e)
    sel = jnp.where(first, 1.0, jnp.where(second, 1.0, 0.0))
    wsel_ref[...] = jnp.where(first, w1, jnp.where(second, w2, 0.0))
    sel_bf = sel.astype(BF16)
    rank = jnp.dot(tril_ref[...], sel_bf, preferred_element_type=F32)
    rank_ref[...] = jnp.where(sel > 0.5, rank, -1.0)
    ones = jnp.ones((SUBLANE, MOE_SUB), BF16)
    row = lax.broadcasted_iota(jnp.int32, (SUBLANE, LANE), 0)
    cnt = jnp.zeros((SUBLANE, LANE), F32)
    for s in range(sel.shape[0] // MOE_SUB):
        part = jnp.dot(ones, sel_bf[s * MOE_SUB:(s + 1) * MOE_SUB, :], preferred_element_type=F32)
        cnt = jnp.where(row == s, part, cnt)
    cnt_ref[0] = cnt.astype(jnp.int32)


def _router(x2, wr, br, tril, tm):
    n = x2.shape[0]
    nt = n // tm
    col = pl.BlockSpec((tm, LANE), lambda i: (i, 0))
    return pl.pallas_call(
        _router_kernel,
        grid=(nt,),
        in_specs=[pl.BlockSpec((tm, D_MODEL), lambda i: (i, 0)), _const_spec(wr.shape),
                  _const_spec(br.shape), _const_spec(tril.shape)],
        out_specs=[col, col, pl.BlockSpec((1, SUBLANE, LANE), lambda i: (i, 0, 0))],
        out_shape=[jax.ShapeDtypeStruct((n, LANE), F32), jax.ShapeDtypeStruct((n, LANE), F32),
                   jax.ShapeDtypeStruct((nt, SUBLANE, LANE), jnp.int32)],
        compiler_params=_params(("arbitrary",), 40),
        name="router",
    )(x2, wr, br, tril)


def _moe_kernel(rows, cnt_ref, off_ref, end_ref, x_ref, xb_ref, rrow_ref, rank_ref, wsel_ref, wg_ref, wu_ref,
                wd_ref, ln_ref, o_ref, xg_sc, yb_sc):
    i = pl.program_id(0)
    e = pl.program_id(1)
    c = pl.program_id(2)
    n_e = pl.num_programs(1)
    n_c = pl.num_programs(2)
    cnt = cnt_ref[i * N_EXPERTS + e]
    n_blk = (cnt + rows - 1) // rows
    tm = xb_ref.shape[0]
    n_sub = tm // MOE_SUB

    def windows(s):
        idx = (i * n_sub + s) * N_EXPERTS + e
        off = off_ref[idx]
        end = end_ref[idx]
        a0 = (off // BF16_ROWS) * BF16_ROWS
        n_win = jnp.where(end > off, (end - a0 + MOE_WIN - 1) // MOE_WIN, 0)
        return a0, n_win

    @pl.when((e == 0) & (c == 0))
    def _():
        o_ref[...] = ALPHA * x_ref[...]

    toks = [slice(s * MOE_SUB, (s + 1) * MOE_SUB) for s in range(n_sub)]

    def gather_piece(s, r0):
        slot = (lax.broadcasted_iota(jnp.int32, (MOE_WIN, MOE_SUB), 0) + r0).astype(F32)
        onehot = jnp.where(rrow_ref[0, :, toks[s]] == slot, 1.0, 0.0).astype(BF16)
        return jnp.dot(onehot, xb_ref[toks[s], :], preferred_element_type=F32)

    def gather_add(r0, piece):
        cur = xg_sc[pl.ds(r0, MOE_WIN), :].astype(F32)
        xg_sc[pl.ds(r0, MOE_WIN), :] = (cur + piece).astype(BF16)

    @pl.when(c == 0)
    def _():
        xg_sc[...] = jnp.zeros(xg_sc.shape, xg_sc.dtype)
        wins = [windows(s) for s in range(n_sub)]
        starts = [pl.multiple_of(a0, BF16_ROWS) for a0, _ in wins]
        pieces = [gather_piece(s, starts[s]) for s in range(n_sub)]
        for s in range(n_sub):
            gather_add(starts[s], pieces[s])
        for s in range(n_sub):
            a0, n_win = wins[s]

            def more(wi, carry, s=s, a0=a0):
                r0 = pl.multiple_of(a0 + wi * MOE_WIN, BF16_ROWS)
                gather_add(r0, gather_piece(s, r0))
                return carry

            lax.fori_loop(1, n_win, more, 0)

    def expert_rows(r0, m):
        xg = xg_sc[pl.ds(r0, m), :]
        h = (_silu(jnp.dot(xg, wg_ref[0], preferred_element_type=F32))
             * jnp.dot(xg, wu_ref[0], preferred_element_type=F32))
        yb = jnp.dot(h.astype(BF16), wd_ref[0], preferred_element_type=F32)

        @pl.when(c == 0)
        def _():
            yb_sc[pl.ds(r0, m), :] = yb

        @pl.when(c > 0)
        def _():
            yb_sc[pl.ds(r0, m), :] = yb_sc[pl.ds(r0, m), :] + yb

    def full_block(blk, carry):
        expert_rows(pl.multiple_of(blk * rows, BF16_ROWS), rows)
        return carry

    lax.fori_loop(0, n_blk - 1, full_block, 0)
    last = pl.multiple_of(jnp.maximum(n_blk - 1, 0) * rows, BF16_ROWS)
    rem = cnt - last
    sizes = [rows - k * BF16_ROWS for k in range(MOE_LAST_SIZES - 1, -1, -1)]
    used = jnp.int32(0)
    lower = 0
    for m in sizes:
        fits = (rem > lower) & (rem <= m)
        pl.when(fits)(functools.partial(expert_rows, last, m))
        used = jnp.where(fits, m, used)
        lower = m

    @pl.when(c == 0)
    def _():
        tail = pl.multiple_of(last + used, BF16_ROWS)
        yb_sc[pl.ds(tail, MOE_TAIL), :] = jnp.zeros((MOE_TAIL, D_MODEL), F32)

    @pl.when(c == n_c - 1)
    def _():
        mine = lax.broadcasted_iota(jnp.int32, (tm, LANE), 1) == e
        rank_col = jnp.sum(jnp.where(mine, rank_ref[...], 0.0), axis=1, keepdims=True)
        w_col = jnp.sum(jnp.where(mine, wsel_ref[...], 0.0), axis=1, keepdims=True)

        def scatter_piece(s, r0):
            slot = (lax.broadcasted_iota(jnp.int32, (MOE_SUB, MOE_WIN), 1) + r0).astype(F32)
            onehot = jnp.where(rank_col[toks[s]] == slot, 1.0, 0.0).astype(BF16)
            return jnp.dot(onehot, yb_sc[pl.ds(r0, MOE_WIN), :].astype(BF16), preferred_element_type=F32)

        wins = [windows(s) for s in range(n_sub)]
        starts = [pl.multiple_of(a0, BF16_ROWS) for a0, _ in wins]
        backs = [scatter_piece(s, starts[s]) for s in range(n_sub)]
        for s in range(n_sub):
            o_ref[toks[s], :] = o_ref[toks[s], :] + w_col[toks[s]] * backs[s]
        for s in range(n_sub):
            a0, n_win = wins[s]

            def more(wi, carry, s=s, a0=a0):
                r0 = pl.multiple_of(a0 + wi * MOE_WIN, BF16_ROWS)
                o_ref[toks[s], :] = o_ref[toks[s], :] + w_col[toks[s]] * scatter_piece(s, r0)
                return carry

            lax.fori_loop(1, n_win, more, 0)

    @pl.when((e == n_e - 1) & (c == n_c - 1))
    def _():
        o_ref[...] = _layer_norm_rows(o_ref[...], ln_ref[0:1, :], ln_ref[1:2, :])


def _moe(x2, xb2, counts, offs, ends, rank_row, rank, wsel, wg, wu, wd, ln, tm, rows):
    n = x2.shape[0]
    nt = n // tm
    cap = -(-tm // rows) * rows + MOE_TAIL
    tile = lambda i, e, c, *_: (i, 0)
    grid_spec = pltpu.PrefetchScalarGridSpec(
        num_scalar_prefetch=3,
        grid=(nt, N_EXPERTS, FF_SPLIT),
        in_specs=[
            pl.BlockSpec((tm, D_MODEL), tile),
            pl.BlockSpec((tm, D_MODEL), tile),
            pl.BlockSpec((1, 1, tm), lambda i, e, c, *_: (e, 0, i)),
            pl.BlockSpec((tm, LANE), tile),
            pl.BlockSpec((tm, LANE), tile),
            pl.BlockSpec((1, D_MODEL, FF_PART), lambda i, e, c, *_: (e, 0, c)),
            pl.BlockSpec((1, D_MODEL, FF_PART), lambda i, e, c, *_: (e, 0, c)),
            pl.BlockSpec((1, FF_PART, D_MODEL), lambda i, e, c, *_: (e, c, 0)),
            pl.BlockSpec((2, D_MODEL), lambda i, e, c, *_: (0, 0)),
        ],
        out_specs=pl.BlockSpec((tm, D_MODEL), tile),
        scratch_shapes=[pltpu.VMEM((cap, D_MODEL), BF16), pltpu.VMEM((cap, D_MODEL), F32)],
    )
    return pl.pallas_call(
        functools.partial(_moe_kernel, rows),
        grid_spec=grid_spec,
        out_shape=jax.ShapeDtypeStruct((n, D_MODEL), F32),
        compiler_params=_params(("arbitrary", "arbitrary", "arbitrary"), 56),
        name="moe",
    )(counts, offs, ends, x2, xb2, rank_row, rank, wsel, wg, wu, wd, ln)


def _tile_sizes(b, t):
    n = b * t
    tm = min(512, n)
    tm_proj = min(512, t)
    tt = min(512, t)
    tm_moe = min(1024, n)
    rows = 288 if tm_moe == 1024 else 160
    return tm, tm_proj, tt, tm_moe, rows


def _head_rows(s):
    b, h, r, c = s.shape
    return s.reshape(b, h * r, c)


def _pad_cols(a, width):
    return jnp.pad(a, ((0, 0), (0, width - a.shape[1])))


def _prep_layer(l, p):
    d = D_MODEL
    bw = BR_WIDTH
    w = {}
    w["win"] = p["w_in"][l].astype(BF16)
    mu = p["rwkv_mu_x"][l]
    if l >= 1:
        v1, v2, v0, mu_v = p["rwkv_v1"][l - 1], p["rwkv_v2"][l - 1], p["rwkv_v0"][l - 1], p["rwkv_mu_v"][l - 1]
    else:
        v1, v2 = jnp.zeros((d, RWKV_V_LORA), F32), jnp.zeros((RWKV_V_LORA, bw), F32)
        v0, mu_v = jnp.zeros((bw,), F32), jnp.zeros((d,), F32)
    first = [p["rwkv_w1"][l], p["rwkv_a1"][l], p["rwkv_g1"][l], v1, p["gla_w1"][l]]
    shift_mu = [mu[0], mu[1], mu[2], mu_v, jnp.zeros((d,), F32)]
    w["wl1"] = _pad_cols(jnp.concatenate(first, axis=1), LORA_COLS).astype(BF16)
    w["wl1mu"] = _pad_cols(jnp.concatenate([m[:, None] * a for m, a in zip(shift_mu, first)], axis=1),
                           LORA_COLS).astype(BF16)
    second = jax.scipy.linalg.block_diag(p["rwkv_w2"][l], p["rwkv_a2"][l], p["rwkv_g2"][l], v2, p["gla_w2"][l])
    w["w2"] = jnp.pad(second, ((0, LORA_COLS - second.shape[0]), (0, 0))).astype(BF16)
    w["bias"] = jnp.concatenate([p["rwkv_w0"][l], p["rwkv_a0"][l], jnp.zeros((bw,), F32), v0,
                                 p["gla_b"][l]])[None]
    rows = [p["rwkv_mu_rkv"][l].reshape(RWKV_COLS)]
    rows += [jnp.pad(p[name][l], (0, RWKV_COLS - bw))
             for name in ("rwkv_k_k", "rwkv_k_a", "rwkv_r_k", "rwkv_ln_g", "rwkv_ln_b")]
    rows += [jnp.zeros((RWKV_COLS,), F32)] * (SUBLANE - len(rows))
    w["rwkv_prm"] = jnp.stack(rows)
    w["ret_prm"] = jnp.stack([p["ret_gn_g"][l], p["ret_gn_b"][l]])
    w["hgrn_ng"] = p["hgrn_norm_g"][l][None]
    w["gla_ng"] = p["gla_norm_g"][l][None]
    w["wg"] = p["w_gate"][l].astype(BF16)
    w["bg"] = p["b_gate"][l]
    w["wbr"] = p["w_br"][l].astype(BF16)
    w["wo"] = p["w_o"][l].astype(BF16)
    w["ln1"] = jnp.stack([p["ln1_g"][l], p["ln1_b"][l]])
    w["ln2"] = jnp.stack([p["ln2_g"][l], p["ln2_b"][l]])
    j = l // 2
    if l % 2 == 0:
        w["ffn"] = (p["ffn_w_gate"][j].astype(BF16), p["ffn_w_up"][j].astype(BF16),
                    p["ffn_w_down"][j].astype(BF16))
    else:
        wr = _pad_cols(p["router_w"][j], LANE)
        br = _pad_cols(p["router_b"][j][None], LANE)
        w["moe"] = (wr, br, p["moe_w_gate"][j].astype(BF16), p["moe_w_up"][j].astype(BF16),
                    p["moe_w_down"][j].astype(BF16))
    return w


def _mixer_consts():
    bdm = _np_block_mask(HEAD_DIM, HEAD_DIM)
    lg = np.log1p(-np.exp2(-5.0 - np.arange(N_HEADS, dtype=np.float64)))
    lg_l = np.repeat(lg, HEAD_DIM)[None, :]
    t = np.arange(CHUNK, dtype=np.float64)[:, None]
    s_side = (np.arange(N_HEADS * CHUNK) % CHUNK)[None, :].astype(np.float64)
    lg_side = np.repeat(lg, CHUNK)[None, :]
    d_mat = np.where(s_side <= t, np.exp((t - s_side) * lg_side), 0.0)
    dec = np.zeros((3 * CHUNK + SUBLANE, BR_WIDTH), np.float64)
    dec[0:CHUNK] = np.exp((t + 1.0) * lg_l)
    dec[CHUNK:2 * CHUNK] = np.exp((CHUNK - 1.0 - t) * lg_l)
    dec[2 * CHUNK:3 * CHUNK] = d_mat
    dec[3 * CHUNK] = np.exp(CHUNK * lg_l[0])
    return {
        "tril": jnp.asarray(np.tril(np.ones((CHUNK, CHUNK), np.float32)), BF16),
        "bdm": jnp.asarray(bdm, F32),
        "head_lanes": jnp.asarray(_np_head_lanes(HEAD_DIM), F32),
        "strict": jnp.asarray(_np_causal_side_by_side(True), F32),
        "incl": jnp.asarray(_np_causal_side_by_side(False), F32),
        "eye": jnp.asarray(np.tile(np.eye(CHUNK, dtype=np.float32), (1, N_HEADS)), F32),
        "ret_dec": jnp.asarray(dec, F32),
        "gla_pair": jnp.asarray(_np_block_mask(GLA_DK, HEAD_DIM), BF16),
        "gla_mask": jnp.asarray(_np_block_mask(HEAD_DIM, GLA_DK), F32),
        "gla_head_lanes": jnp.asarray(_np_head_lanes(GLA_DK), F32),
        "fold": jnp.asarray(np.tile(np.eye(HEAD_DIM, dtype=np.float32), (N_HEADS, 1)), BF16),
        "unfold": jnp.asarray(np.tile(np.eye(HEAD_DIM, dtype=np.float32), (1, N_HEADS)), BF16),
    }


def _rope_tables(pos0, t):
    half = HEAD_DIM // 2
    pos = pos0 + jnp.arange(t, dtype=F32)
    inv = ROPE_THETA ** (-jnp.arange(half, dtype=F32) / half)
    ang = pos[:, None] * inv[None]
    cos = jnp.cos(ang)
    sin = jnp.sin(ang)
    cos_t = jnp.tile(jnp.concatenate([cos, cos], axis=1), (1, N_HEADS))
    sin_t = jnp.tile(jnp.concatenate([-sin, sin], axis=1), (1, N_HEADS))
    return cos_t, sin_t


def _previous_rows(x, x_last, tm):
    b, t, d = x.shape
    per_seq = t // tm
    tails = x.reshape(b, per_seq, tm, d)[:, :, tm - 1, :]
    prev = jnp.concatenate([x_last[:, None, :], tails[:, :per_seq - 1, :]], axis=1)
    return prev.reshape(b * per_seq, 1, d)


def _run_trunk(x, pos0, s_rwkv, c_shift, s_ret, s_hgrn, s_gla, prm, layers, consts):
    b, t, d = x.shape
    n = b * t
    tm, tm_proj, tt, tm_moe, rows = _tile_sizes(b, t)
    cos_t, sin_t = _rope_tables(pos0, t)
    v_first = None
    new_rwkv, new_shift, new_ret, new_hgrn, new_gla = [], [], [], [], []
    for l in range(DEPTH):
        w = layers[l]
        x_in = x
        x_last = c_shift[l]
        p2, aux2 = _in_proj(x.reshape(n, d), _previous_rows(x, x_last, tm_proj), w["win"], w["wl1"],
                            w["wl1mu"], w["w2"], w["bias"], tm_proj)
        p3 = p2.reshape(b, t, IN_COLS)
        aux3 = aux2.reshape(b, t, AUX_COLS)
        pad = (-b) % SUBLANE
        x_last_p = jnp.concatenate([x_last, jnp.zeros((pad, d), F32)], axis=0) if pad else x_last
        rkv_last = _rows_matmul(x_last_p, w["win"][:, :RWKV_COLS])[:b, None, :]

        res = _rwkv_mixer(p3, aux3, v_first, rkv_last, _head_rows(s_rwkv[l]), w["rwkv_prm"], consts, tt)
        if v_first is None:
            o_rwkv, v_first, st_rwkv = res
        else:
            o_rwkv, st_rwkv = res
        o_ret, st_ret = _ret_mixer(p3, cos_t, sin_t, _head_rows(s_ret[l]), w["ret_prm"], consts, tt)
        o_hgrn, st_hgrn = _hgrn_mixer(p3, _head_rows(s_hgrn[l]), prm["hgrn_lb_logits"], w["hgrn_ng"],
                                      consts, l, tt)
        o_gla, st_gla = _gla_mixer(p3, aux3, _head_rows(s_gla[l]), w["gla_ng"], consts, tt)

        outs = [o.reshape(n, BR_WIDTH) for o in (o_rwkv, o_ret, o_hgrn, o_gla)]
        is_moe = l % 2 == 1
        merged = _merge(x.reshape(n, d), outs, w["wg"], w["bg"], w["wbr"], w["wo"], w["ln1"], tm_moe, is_moe)
        if not is_moe:
            x1 = merged[0]
            x2 = _ffn(x1, *w["ffn"], w["ln2"], tm_moe)
        else:
            x1, x1b = merged
            wr, br, mg, mu_, md = w["moe"]
            tril_m = jnp.asarray(np.tril(np.ones((tm_moe, tm_moe), np.float32), -1), BF16)
            rank, wsel, cnt = _router(x1, wr, br, tril_m, tm_moe)
            rank_row = rank[:, :N_EXPERTS].T.reshape(N_EXPERTS, 1, n)
            per_sub = cnt[:, :tm_moe // MOE_SUB, :N_EXPERTS]
            ends = jnp.cumsum(per_sub, axis=1)
            x2 = _moe(x1, x1b, ends[:, -1, :].reshape(-1), (ends - per_sub).reshape(-1), ends.reshape(-1),
                      rank_row, rank, wsel, mg, mu_, md, w["ln2"], tm_moe, rows)
        x = x2.reshape(b, t, d)

        new_rwkv.append(st_rwkv.reshape(b, N_HEADS, HEAD_DIM, HEAD_DIM))
        new_shift.append(x_in[:, -1])
        new_ret.append(st_ret.reshape(b, N_HEADS, HEAD_DIM, HEAD_DIM))
        new_hgrn.append(st_hgrn.reshape(b, N_HEADS, HEAD_DIM, HEAD_DIM))
        new_gla.append(st_gla.reshape(b, N_HEADS, GLA_DK, HEAD_DIM))
    return (x, jnp.stack(new_rwkv), jnp.stack(new_shift), jnp.stack(new_ret), jnp.stack(new_hgrn),
            jnp.stack(new_gla))


def kernel(x_prompt, x_sample, state_rwkv, cache_shift, state_ret, state_hgrn, state_gla, w_in, rwkv_mu_rkv, rwkv_mu_x, rwkv_mu_v, rwkv_w0, rwkv_w1, rwkv_w2, rwkv_a0, rwkv_a1, rwkv_a2, rwkv_v0, rwkv_v1, rwkv_v2, rwkv_g1, rwkv_g2, rwkv_k_k, rwkv_k_a, rwkv_r_k, rwkv_ln_g, rwkv_ln_b, ret_gn_g, ret_gn_b, hgrn_lb_logits, hgrn_norm_g, gla_w1, gla_w2, gla_b, gla_norm_g, w_br, w_gate, b_gate, w_o, ln1_g, ln1_b, ln2_g, ln2_b, ffn_w_gate, ffn_w_up, ffn_w_down, router_w, router_b, moe_w_gate, moe_w_up, moe_w_down):
    prm = {
        'w_in': w_in, 'rwkv_mu_rkv': rwkv_mu_rkv, 'rwkv_mu_x': rwkv_mu_x, 'rwkv_mu_v': rwkv_mu_v,
        'rwkv_w0': rwkv_w0, 'rwkv_w1': rwkv_w1, 'rwkv_w2': rwkv_w2,
        'rwkv_a0': rwkv_a0, 'rwkv_a1': rwkv_a1, 'rwkv_a2': rwkv_a2,
        'rwkv_v0': rwkv_v0, 'rwkv_v1': rwkv_v1, 'rwkv_v2': rwkv_v2,
        'rwkv_g1': rwkv_g1, 'rwkv_g2': rwkv_g2, 'rwkv_k_k': rwkv_k_k, 'rwkv_k_a': rwkv_k_a,
        'rwkv_r_k': rwkv_r_k, 'rwkv_ln_g': rwkv_ln_g, 'rwkv_ln_b': rwkv_ln_b,
        'ret_gn_g': ret_gn_g, 'ret_gn_b': ret_gn_b, 'hgrn_lb_logits': hgrn_lb_logits,
        'hgrn_norm_g': hgrn_norm_g, 'gla_w1': gla_w1, 'gla_w2': gla_w2, 'gla_b': gla_b,
        'gla_norm_g': gla_norm_g, 'w_br': w_br, 'w_gate': w_gate, 'b_gate': b_gate, 'w_o': w_o,
        'ln1_g': ln1_g, 'ln1_b': ln1_b, 'ln2_g': ln2_g, 'ln2_b': ln2_b,
        'ffn_w_gate': ffn_w_gate, 'ffn_w_up': ffn_w_up, 'ffn_w_down': ffn_w_down,
        'router_w': router_w, 'router_b': router_b,
        'moe_w_gate': moe_w_gate, 'moe_w_up': moe_w_up, 'moe_w_down': moe_w_down,
    }
    layers = [_prep_layer(l, prm) for l in range(DEPTH)]
    consts = _mixer_consts()
    bp = x_prompt.shape[0]
    zero_hd = jnp.zeros((DEPTH, bp, N_HEADS, HEAD_DIM, HEAD_DIM), F32)
    zero_shift = jnp.zeros((DEPTH, bp, D_MODEL), F32)
    zero_gla = jnp.zeros((DEPTH, bp, N_HEADS, GLA_DK, HEAD_DIM), F32)
    prompt = _run_trunk(x_prompt, 0.0, zero_hd, zero_shift, zero_hd, zero_hd, zero_gla, prm, layers, consts)
    sample = _run_trunk(x_sample, float(PAST_LEN), state_rwkv, cache_shift, state_ret, state_hgrn,
                        state_gla, prm, layers, consts)
    y_p, p_rwkv, p_shift, p_ret, p_hgrn, p_gla = prompt
    y_s, s_rwkv, s_shift, s_ret, s_hgrn, s_gla = sample
    return (y_p, y_s, p_rwkv, p_shift, p_ret, p_hgrn, p_gla, s_rwkv, s_shift, s_ret, s_hgrn, s_gla)
```
